```python
import functools
import jax, jax.numpy as jnp
from jax import lax
import numpy as np

D_MODEL = 4096
BATCH = 4
SEQ = 2048
DEPTH = 2
DEC_BATCH = 8
DEC_SEQ = 4
PAST_LEN = 16384
PAGE_SIZE = 128

N_EVEN = (DEPTH + 1) // 2
N_ODD = DEPTH // 2
PLE_DIM = 256
HEAD_DIM = 128
ROT_DIM = HEAD_DIM // 4
ROPE_THETA = 500000.0
NORM_EPS = 1e-6
F32 = jnp.float32

A_WIDTH = D_MODEL // 2
A_HEAD = 64
A_HEADS = A_WIDTH // A_HEAD
DECAY_LORA = 96
AAA_LORA = 96
GN_EPS = 64e-5
SHIFT_W = 3 * A_WIDTH + DECAY_LORA + AAA_LORA

B_WIDTH = D_MODEL // 2
B_HEADS = B_WIDTH // HEAD_DIM
B_KV_HEADS = 4
IDX_HEADS = 16
IDX_DIM = 128
IDX_Q_RANK = 512
IDX_TOPK_MAX = 256
QBLOCK = 128

C_GROUPS = ((128, 1), (512, 4), (2048, 16))
C_HEADS = 16
C_WIDTH = C_HEADS * HEAD_DIM

IN_A = SHIFT_W + A_WIDTH
IN_B = B_WIDTH + 2 * B_KV_HEADS * HEAD_DIM + IDX_Q_RANK + IDX_DIM + IDX_HEADS + B_WIDTH
IN_EVEN = IN_A + IN_B
OUT_EVEN = A_WIDTH + B_WIDTH
IN_ODD = 3 * len(C_GROUPS) * C_WIDTH + C_WIDTH

kernel_name = 'hybrid_rwkv7_dsa_dilated_step'


def _offsets(sizes):
    out, acc = [], 0
    for s in sizes[:-1]:
        acc += s
        out.append(acc)
    return out


def _rms(x, eps=NORM_EPS):
    xf = x.astype(F32)
    return xf * lax.rsqrt(jnp.mean(xf * xf, axis=-1, keepdims=True) + eps)


def rms_norm(x, g):
    return (_rms(x) * g.astype(F32)).astype(x.dtype)


def layer_norm(x, g, b, eps=NORM_EPS):
    xf = x.astype(F32)
    mu = jnp.mean(xf, axis=-1, keepdims=True)
    xc = xf - mu
    y = xc * lax.rsqrt(jnp.mean(xc * xc, axis=-1, keepdims=True) + eps)
    return (y * g.astype(F32) + b.astype(F32)).astype(x.dtype)


def partial_rope(x, pos):
    half = ROT_DIM // 2
    freqs = ROPE_THETA ** (-jnp.arange(half, dtype=F32) / half)
    ang = pos.astype(F32)[:, None] * freqs[None, :]
    ang = ang.reshape((1, ang.shape[0]) + (1,) * (x.ndim - 3) + (half,))
    cos, sin = jnp.cos(ang), jnp.sin(ang)
    xf = x.astype(F32)
    x1, x2 = xf[..., :half], xf[..., half:ROT_DIM]
    out = jnp.concatenate([x1 * cos - x2 * sin, x2 * cos + x1 * sin, xf[..., ROT_DIM:]], axis=-1)
    return out.astype(x.dtype)


def rwkv7_mix(sh, prev_row, s0, mu, w0, w_lora, a0, a_lora, k_k, k_a, r_k, gn_g, gn_b):
    bn, t, _ = sh.shape
    prev = jnp.concatenate([prev_row[:, None, :].astype(sh.dtype), sh[:, :-1]], axis=1)
    xm = sh + (prev - sh) * mu
    r, k, v, xw, xa = jnp.split(xm, [A_WIDTH, 2 * A_WIDTH, 3 * A_WIDTH, 3 * A_WIDTH + DECAY_LORA], axis=-1)
    wlog = -jax.nn.softplus(-(w0 + jnp.tanh(xw) @ w_lora)) - 0.5
    decay = jnp.exp(-jnp.exp(wlog.astype(F32)))
    a = jax.nn.sigmoid((a0 + xa @ a_lora).astype(F32))

    def heads(z):
        return z.astype(F32).reshape(bn, t, A_HEADS, A_HEAD)

    kk = heads(k * k_k)
    kk = kk / jnp.maximum(jnp.sqrt(jnp.sum(kk * kk, axis=-1, keepdims=True)), 1e-12)
    a_h = heads(a)
    k_h = heads(k) * (1.0 + (a_h - 1.0) * k_a.astype(F32).reshape(A_HEADS, A_HEAD))
    r_h, v_h, w_h = heads(r), heads(v), heads(decay)

    def step(S, inp):
        r_t, w_t, k_t, v_t, kk_t, a_t = inp
        sa = jnp.einsum('bhij,bhj->bhi', S, -kk_t)
        S = (S * w_t[:, :, None, :] + sa[..., None] * (kk_t * a_t)[:, :, None, :]
             + v_t[..., None] * k_t[:, :, None, :])
        return S, jnp.einsum('bhij,bhj->bhi', S, r_t)

    seq = tuple(jnp.moveaxis(z, 1, 0) for z in (r_h, w_h, k_h, v_h, kk, a_h))
    s_fin, y = lax.scan(step, s0.astype(F32), seq)
    y = jnp.moveaxis(y, 0, 1)
    ym = jnp.mean(y, axis=-1, keepdims=True)
    yc = y - ym
    yn = yc * lax.rsqrt(jnp.mean(yc * yc, axis=-1, keepdims=True) + GN_EPS)
    yn = yn.reshape(bn, t, A_WIDTH) * gn_g.astype(F32) + gn_b.astype(F32)
    bonus = (jnp.sum(r_h * k_h * r_k.astype(F32), axis=-1, keepdims=True) * v_h).reshape(bn, t, A_WIDTH)
    return (yn + bonus).astype(sh.dtype), s_fin.astype(s0.dtype), sh[:, -1]


def dsa_project(u, pos, q_norm, k_norm, qi_norm, w_qi, kidx_g, kidx_b):
    bn, t, _ = u.shape
    sizes = (B_WIDTH, B_KV_HEADS * HEAD_DIM, B_KV_HEADS * HEAD_DIM, IDX_Q_RANK, IDX_DIM, IDX_HEADS)
    q, k, v, cqi, kidx, wi = jnp.split(u, _offsets(sizes), axis=-1)
    q = partial_rope(rms_norm(q.reshape(bn, t, B_HEADS, HEAD_DIM), q_norm), pos)
    k = partial_rope(rms_norm(k.reshape(bn, t, B_KV_HEADS, HEAD_DIM), k_norm), pos)
    v = v.reshape(bn, t, B_KV_HEADS, HEAD_DIM)
    qi = (rms_norm(cqi, qi_norm) @ w_qi).reshape(bn, t, IDX_HEADS, IDX_DIM)
    qi = partial_rope(qi, pos)
    kidx = partial_rope(layer_norm(kidx, kidx_g, kidx_b), pos)
    wi = wi * IDX_HEADS ** -0.5
    return q, k, v, qi, kidx, wi


def index_topk(qi, wi, kidx, qpos, topk):
    s = jnp.einsum('bqhd,bkd->bqhk', qi.astype(F32), kidx.astype(F32)) * IDX_DIM ** -0.5
    score = jnp.einsum('bqhk,bqh->bqk', jax.nn.relu(s), wi.astype(F32))
    allowed = jnp.arange(kidx.shape[1])[None, :] <= qpos[:, None]
    score = jnp.where(allowed[None], score, -jnp.inf)
    _, idx = lax.top_k(score, topk)
    return idx, idx <= qpos[None, :, None]


def sparse_attn(q, kg, vg, valid):
    bn, tq, h, d = q.shape
    qg = q.reshape(bn, tq, B_KV_HEADS, h // B_KV_HEADS, d)
    s = jnp.einsum('bqhgd,bqkhd->bqhgk', qg, kg).astype(F32) * d ** -0.5
    s = jnp.where(valid[:, :, None, None, :], s, -jnp.inf)
    p = jax.nn.softmax(s, axis=-1)
    o = jnp.einsum('bqhgk,bqkhd->bqhgd', p.astype(vg.dtype), vg)
    return o.reshape(bn, tq, h * d)


def dsa_prompt(q, k, v, qi, wi, kidx):
    bn, s_len = q.shape[:2]
    topk = min(IDX_TOPK_MAX, s_len // 4)
    nb = s_len // QBLOCK
    bidx = jnp.arange(bn)[:, None, None]

    def blocks(z):
        return jnp.moveaxis(z.reshape((bn, nb, QBLOCK) + z.shape[2:]), 1, 0)

    def one(args):
        qb, qib, wib, start = args
        qpos = start + jnp.arange(QBLOCK)
        idx, valid = index_topk(qib, wib, kidx, qpos, topk)
        return sparse_attn(qb, k[bidx, idx], v[bidx, idx], valid)

    starts = jnp.arange(nb) * QBLOCK
    o = lax.map(one, (blocks(q), blocks(qi), blocks(wi), starts))
    return jnp.moveaxis(o, 0, 1).reshape(bn, s_len, -1)


def dsa_sample(q, k, v, qi, wi, kidx, cache_k, cache_v, cache_kidx, page_table, layer):
    bn, t = q.shape[:2]
    past = page_table.shape[1] * PAGE_SIZE
    topk = min(IDX_TOPK_MAX, (past + t) // 4)
    kidx_past = cache_kidx[layer, page_table].reshape(bn, past, IDX_DIM)
    kidx_all = jnp.concatenate([kidx_past, kidx.astype(kidx_past.dtype)], axis=1)
    qpos = past + jnp.arange(t)
    idx, valid = index_topk(qi, wi, kidx_all, qpos, topk)
    bidx = jnp.arange(bn)[:, None, None]
    pidx = jnp.minimum(idx, past - 1)
    phys = page_table[bidx, pidx // PAGE_SIZE]
    off = pidx % PAGE_SIZE
    nidx = jnp.clip(idx - past, 0, t - 1)
    is_new = (idx >= past)[..., None, None]
    kg = jnp.where(is_new, k[bidx, nidx], cache_k[layer, phys, off].astype(k.dtype))
    vg = jnp.where(is_new, v[bidx, nidx], cache_v[layer, phys, off].astype(v.dtype))
    return sparse_attn(q, kg, vg, valid)


def even_mixer(xn, pos, prev_row, s0, attend, ep):
    u = xn @ ep['w_in']
    u_a, g_a, u_b, g_b = jnp.split(u, [SHIFT_W, IN_A, IN_A + IN_B - B_WIDTH], axis=-1)
    y_a, s_fin, last_row = rwkv7_mix(u_a, prev_row, s0, ep['shift_mu'], ep['w0'], ep['w_lora'],
                                     ep['a0'], ep['a_lora'], ep['k_k'], ep['k_a'], ep['r_k'],
                                     ep['gn_gain'], ep['gn_bias'])
    q, k, v, qi, kidx, wi = dsa_project(u_b, pos, ep['q_norm'], ep['k_norm'], ep['qi_norm'],
                                        ep['w_qi'], ep['kidx_gain'], ep['kidx_bias'])
    y_b = attend(q, k, v, qi, wi, kidx)
    y = jnp.concatenate([y_a * jax.nn.silu(g_a), y_b * jax.nn.silu(g_b)], axis=-1)
    return y @ ep['w_out'], (s_fin, last_row, k, v, kidx)


def dilated_prompt(q, k, v, window, dil):
    bn, s_len, h, d = q.shape
    wm = window // dil
    m_len = s_len // dil
    nb = -(-m_len // wm)
    mp = nb * wm

    def residues(z):
        z = jnp.moveaxis(z.reshape(bn, m_len, dil, h, d), 2, 1)
        return jnp.pad(z, ((0, 0), (0, 0), (0, mp - m_len), (0, 0), (0, 0)))

    def kblocks(z):
        z = jnp.pad(residues(z), ((0, 0), (0, 0), (wm, 0), (0, 0), (0, 0)))
        z = z.reshape(bn, dil, nb + 1, wm, h, d)
        return jnp.concatenate([z[:, :, :-1], z[:, :, 1:]], axis=3)

    qr = residues(q).reshape(bn, dil, nb, wm, h, d)
    kb, vb = kblocks(k), kblocks(v)
    s = jnp.einsum('brnqhd,brnkhd->brnhqk', qr, kb).astype(F32) * d ** -0.5
    iq = jnp.arange(wm)[:, None]
    ik = jnp.arange(2 * wm)[None, :]
    dist = iq + wm - ik
    mk = jnp.arange(nb)[:, None, None] * wm - wm + ik[None]
    mask = ((dist >= 0) & (dist <= wm))[None] & (mk >= 0)
    s = jnp.where(mask[None, None, :, None], s, -jnp.inf)
    lse = jax.nn.logsumexp(s, axis=-1)
    p = jnp.exp(s - lse[..., None])
    o = jnp.einsum('brnhqk,brnkhd->brnqhd', p.astype(v.dtype), vb)
    o = o.reshape(bn, dil, mp, h, d)[:, :, :m_len]
    o = jnp.moveaxis(o, 1, 2).reshape(bn, s_len, h, d)
    lse = jnp.moveaxis(lse, -1, -2).reshape(bn, dil, mp, h)[:, :, :m_len]
    lse = jnp.moveaxis(lse, 1, 2).reshape(bn, s_len, h)
    return o, lse


def dilated_sample(q, k, v, buf_k, buf_v, window, dil):
    t = q.shape[1]
    d = q.shape[-1]
    wb = buf_k.shape[1]
    wm = window // dil
    kc = jnp.concatenate([buf_k.astype(k.dtype), k], axis=1)
    vc = jnp.concatenate([buf_v.astype(v.dtype), v], axis=1)
    idx = wb + jnp.arange(t)[:, None] - jnp.arange(wm + 1)[None, :] * dil
    valid = idx >= 0
    idxc = jnp.maximum(idx, 0)
    kg, vg = kc[:, idxc], vc[:, idxc]
    s = jnp.einsum('bthd,btjhd->bthj', q, kg).astype(F32) * d ** -0.5
    s = jnp.where(valid[None, :, None, :], s, -jnp.inf)
    lse = jax.nn.logsumexp(s, axis=-1)
    p = jnp.exp(s - lse[..., None])
    o = jnp.einsum('bthj,btjhd->bthd', p.astype(vg.dtype), vg)
    return o, lse, kc[:, -wb:], vc[:, -wb:]


def window_attend_prompt(g, q, k, v, win, dil):
    o, lse = dilated_prompt(q, k, v, win, dil)
    keep = min(win, q.shape[1])
    return o, lse, k[:, -keep:], v[:, -keep:]


def window_attend_sample(g, q, k, v, win, dil, bufs_k, bufs_v, layer):
    return dilated_sample(q, k, v, bufs_k[g][layer], bufs_v[g][layer], win, dil)


def odd_mixer(xn, pos, attend, op):
    u = xn @ op['w_in']
    bn, t, _ = u.shape
    n_g = len(C_GROUPS)
    qkv, gate = jnp.split(u, [3 * n_g * C_WIDTH], axis=-1)
    qkv = qkv.reshape(bn, t, n_g, 3, C_HEADS, HEAD_DIM)
    outs, lses, bufs = [], [], []
    for g, (win, dil) in enumerate(C_GROUPS):
        q = partial_rope(rms_norm(qkv[:, :, g, 0], op['q_norm'][g]), pos)
        k = partial_rope(rms_norm(qkv[:, :, g, 1], op['k_norm'][g]), pos)
        v = qkv[:, :, g, 2]
        o, lse, kb, vb = attend(g, q, k, v, win, dil)
        outs.append(o.astype(F32))
        lses.append(lse)
        bufs += [kb, vb]
    alpha = jax.nn.softmax(jnp.stack(lses, axis=0), axis=0)
    o = jnp.einsum('gbth,gbthd->bthd', alpha, jnp.stack(outs, axis=0))
    y = o.reshape(bn, t, C_WIDTH).astype(xn.dtype) * jax.nn.silu(gate)
    return y @ op['w_out'], tuple(bufs)


def ple_add(h, p_l, w_proj, w_gate):
    gate = jax.nn.sigmoid(_rms(h).astype(h.dtype) @ w_gate)
    return h + gate * (p_l @ w_proj)


def setup_inputs(seed: int = 0):
    key = jax.random.key(seed)
    keys = jax.random.split(key, 64)
    cnt = [0]

    def nk():
        cnt[0] += 1
        return keys[cnt[0] - 1]

    def nrm(shape, scale=1.0):
        return scale * jax.random.normal(nk(), shape, F32)

    def gain(shape):
        return 1.0 + nrm(shape, 0.05)

    n_pages = PAST_LEN // PAGE_SIZE
    n_pool = (5 * DEC_BATCH * n_pages + 3) // 4
    page_table = jax.random.permutation(nk(), n_pool)[: DEC_BATCH * n_pages]
    page_table = page_table.reshape(DEC_BATCH, n_pages).astype(jnp.int32)
    wb = [min(w, PAST_LEN) for w, _ in C_GROUPS]
    return {
        'x_prompt': nrm((BATCH, SEQ, D_MODEL)),
        'x_sample': nrm((DEC_BATCH, DEC_SEQ, D_MODEL)),
        'p_prompt': nrm((DEPTH, BATCH, SEQ, PLE_DIM)),
        'p_sample': nrm((DEPTH, DEC_BATCH, DEC_SEQ, PLE_DIM)),
        'state_wkv': nrm((N_EVEN, DEC_BATCH, A_HEADS, A_HEAD, A_HEAD), 0.5),
        'state_shift': nrm((N_EVEN, DEC_BATCH, SHIFT_W)),
        'cache_k': nrm((N_EVEN, n_pool, PAGE_SIZE, B_KV_HEADS, HEAD_DIM)),
        'cache_v': nrm((N_EVEN, n_pool, PAGE_SIZE, B_KV_HEADS, HEAD_DIM)),
        'cache_kidx': nrm((N_EVEN, n_pool, PAGE_SIZE, IDX_DIM)),
        'page_table': page_table,
        'cache_win_k0': nrm((N_ODD, DEC_BATCH, wb[0], C_HEADS, HEAD_DIM)),
        'cache_win_v0': nrm((N_ODD, DEC_BATCH, wb[0], C_HEADS, HEAD_DIM)),
        'cache_win_k1': nrm((N_ODD, DEC_BATCH, wb[1], C_HEADS, HEAD_DIM)),
        'cache_win_v1': nrm((N_ODD, DEC_BATCH, wb[1], C_HEADS, HEAD_DIM)),
        'cache_win_k2': nrm((N_ODD, DEC_BATCH, wb[2], C_HEADS, HEAD_DIM)),
        'cache_win_v2': nrm((N_ODD, DEC_BATCH, wb[2], C_HEADS, HEAD_DIM)),
        'ln_gain': gain((DEPTH, D_MODEL)),
        'e_w_in': nrm((N_EVEN, D_MODEL, IN_EVEN), D_MODEL ** -0.5),
        'e_shift_mu': jax.random.uniform(nk(), (N_EVEN, SHIFT_W), F32),
        'e_w0': nrm((N_EVEN, A_WIDTH), 0.5) - 1.0,
        'e_w_lora': nrm((N_EVEN, DECAY_LORA, A_WIDTH), 0.5 * DECAY_LORA ** -0.5),
        'e_a0': nrm((N_EVEN, A_WIDTH), 0.5),
        'e_a_lora': nrm((N_EVEN, AAA_LORA, A_WIDTH), 0.5 * AAA_LORA ** -0.5),
        'e_k_k': 1.0 + nrm((N_EVEN, A_WIDTH), 0.1),
        'e_k_a': 1.0 + nrm((N_EVEN, A_WIDTH), 0.1),
        'e_r_k': nrm((N_EVEN, A_HEADS, A_HEAD), 0.1),
        'e_gn_gain': gain((N_EVEN, A_WIDTH)),
        'e_gn_bias': nrm((N_EVEN, A_WIDTH), 0.02),
        'e_q_norm': gain((N_EVEN, HEAD_DIM)),
        'e_k_norm': gain((N_EVEN, HEAD_DIM)),
        'e_qi_norm': gain((N_EVEN, IDX_Q_RANK)),
        'e_w_qi': nrm((N_EVEN, IDX_Q_RANK, IDX_HEADS * IDX_DIM), IDX_Q_RANK ** -0.5),
        'e_kidx_gain': gain((N_EVEN, IDX_DIM)),
        'e_kidx_bias': nrm((N_EVEN, IDX_DIM), 0.02),
        'e_w_out': nrm((N_EVEN, OUT_EVEN, D_MODEL), OUT_EVEN ** -0.5),
        'o_w_in': nrm((N_ODD, D_MODEL, IN_ODD), D_MODEL ** -0.5),
        'o_q_norm': gain((N_ODD, len(C_GROUPS), HEAD_DIM)),
        'o_k_norm': gain((N_ODD, len(C_GROUPS), HEAD_DIM)),
        'o_w_out': nrm((N_ODD, C_WIDTH, D_MODEL), C_WIDTH ** -0.5),
        'ple_w_proj': nrm((DEPTH, PLE_DIM, D_MODEL), PLE_DIM ** -0.5),
        'ple_w_gate': nrm((DEPTH, D_MODEL, D_MODEL), D_MODEL ** -0.5),
    }


def reference(x_prompt, x_sample, p_prompt, p_sample, state_wkv, state_shift, cache_k, cache_v,
              cache_kidx, page_table, cache_win_k0, cache_win_v0, cache_win_k1, cache_win_v1,
              cache_win_k2, cache_win_v2, ln_gain, e_w_in, e_shift_mu, e_w0, e_w_lora, e_a0,
              e_a_lora, e_k_k, e_k_a, e_r_k, e_gn_gain, e_gn_bias, e_q_norm, e_k_norm, e_qi_norm,
              e_w_qi, e_kidx_gain, e_kidx_bias, e_w_out, o_w_in, o_q_norm, o_k_norm, o_w_out,
              ple_w_proj, ple_w_gate):
    bp, s_len, _ = x_prompt.shape
    t_len = x_sample.shape[1]
    past = page_table.shape[1] * PAGE_SIZE
    pos_p = jnp.arange(s_len, dtype=jnp.int32)
    pos_s = past + jnp.arange(t_len, dtype=jnp.int32)
    bufs_k = (cache_win_k0, cache_win_k1, cache_win_k2)
    bufs_v = (cache_win_v0, cache_win_v1, cache_win_v2)
    hp, hs = x_prompt, x_sample
    ev_p, ev_s, od_p, od_s = [], [], [], []
    for i in range(DEPTH):
        l = i // 2
        xp = rms_norm(hp, ln_gain[i])
        xs = rms_norm(hs, ln_gain[i])
        if i % 2 == 0:
            ep = {'w_in': e_w_in[l], 'shift_mu': e_shift_mu[l], 'w0': e_w0[l], 'w_lora': e_w_lora[l],
                  'a0': e_a0[l], 'a_lora': e_a_lora[l], 'k_k': e_k_k[l], 'k_a': e_k_a[l],
                  'r_k': e_r_k[l], 'gn_gain': e_gn_gain[l], 'gn_bias': e_gn_bias[l],
                  'q_norm': e_q_norm[l], 'k_norm': e_k_norm[l], 'qi_norm': e_qi_norm[l],
                  'w_qi': e_w_qi[l], 'kidx_gain': e_kidx_gain[l], 'kidx_bias': e_kidx_bias[l],
                  'w_out': e_w_out[l]}
            row0 = jnp.zeros((bp, SHIFT_W), hp.dtype)
            st0 = jnp.zeros((bp, A_HEADS, A_HEAD, A_HEAD), hp.dtype)
            mp, stp = even_mixer(xp, pos_p, row0, st0, dsa_prompt, ep)
            att_s = functools.partial(dsa_sample, cache_k=cache_k, cache_v=cache_v,
                                      cache_kidx=cache_kidx, page_table=page_table, layer=l)
            ms, sts = even_mixer(xs, pos_s, state_shift[l], state_wkv[l], att_s, ep)
            ev_p.append(stp)
            ev_s.append(sts)
        else:
            op = {'w_in': o_w_in[l], 'q_norm': o_q_norm[l], 'k_norm': o_k_norm[l], 'w_out': o_w_out[l]}
            mp, stp = odd_mixer(xp, pos_p, window_attend_prompt, op)
            att_s = functools.partial(window_attend_sample, bufs_k=bufs_k, bufs_v=bufs_v, layer=l)
            ms, sts = odd_mixer(xs, pos_s, att_s, op)
            od_p.append(stp)
            od_s.append(sts)
        hp = ple_add(hp + mp, p_prompt[i], ple_w_proj[i], ple_w_gate[i])
        hs = ple_add(hs + ms, p_sample[i], ple_w_proj[i], ple_w_gate[i])

    def st(lst, j):
        return jnp.stack([e[j] for e in lst], axis=0)

    wkv_p, shift_p, k_p, v_p, kidx_p = st(ev_p, 0), st(ev_p, 1), st(ev_p, 2), st(ev_p, 3), st(ev_p, 4)
    wkv_s, shift_s, k_s, v_s, kidx_s = st(ev_s, 0), st(ev_s, 1), st(ev_s, 2), st(ev_s, 3), st(ev_s, 4)
    wk0_p, wv0_p, wk1_p, wv1_p, wk2_p, wv2_p = (st(od_p, 0), st(od_p, 1), st(od_p, 2),
                                                st(od_p, 3), st(od_p, 4), st(od_p, 5))
    wk0_s, wv0_s, wk1_s, wv1_s, wk2_s, wv2_s = (st(od_s, 0), st(od_s, 1), st(od_s, 2),
                                                st(od_s, 3), st(od_s, 4), st(od_s, 5))
    return (hp, hs, wkv_p, wkv_s, shift_p, shift_s, k_p, v_p, kidx_p, k_s, v_s, kidx_s,
            wk0_p, wv0_p, wk1_p, wv1_p, wk2_p, wv2_p, wk0_s, wv0_s, wk1_s, wv1_s, wk2_s, wv2_s)
```

```python
import functools

import jax
import jax.numpy as jnp
from jax import lax
from jax.experimental import pallas as pl
from jax.experimental.pallas import tpu as pltpu

F32 = jnp.float32
BF16 = jnp.bfloat16

D_MODEL = 4096
PAGE_SIZE = 128
HEAD_DIM = 128
ROT_DIM = HEAD_DIM // 4
ROPE_THETA = 500000.0
NORM_EPS = 1e-6

A_WIDTH = D_MODEL // 2
A_HEAD = 64
A_HEADS = A_WIDTH // A_HEAD
DECAY_LORA = 96
AAA_LORA = 96
GN_EPS = 64e-5
SHIFT_W = 3 * A_WIDTH + DECAY_LORA + AAA_LORA

B_WIDTH = D_MODEL // 2
B_HEADS = B_WIDTH // HEAD_DIM
B_KV_HEADS = 4
IDX_HEADS = 16
IDX_DIM = 128
IDX_Q_RANK = 512
IDX_TOPK_MAX = 256
QBLOCK = 128

C_GROUPS = ((128, 1), (512, 4), (2048, 16))
C_HEADS = 16
C_WIDTH = C_HEADS * HEAD_DIM

IN_A = SHIFT_W + A_WIDTH
IN_B = B_WIDTH + 2 * B_KV_HEADS * HEAD_DIM + IDX_Q_RANK + IDX_DIM + IDX_HEADS + B_WIDTH

V7X_LANES = 128
V7X_VMEM_LIMIT_BYTES = 56 * 1024 * 1024


def _matmul_kernel(a_ref, b_ref, o_ref, bq_ref):
    @pl.when(pl.program_id(1) == 0)
    def _():
        bq_ref[...] = b_ref[...].astype(BF16)

    o_ref[...] = jnp.dot(a_ref[...].astype(BF16), bq_ref[...], preferred_element_type=F32)


def _pick_tile(n, cands):
    for c in cands:
        if n % c == 0:
            return c
    return n


def matmul(a, b):
    m, k = a.shape
    _, n = b.shape
    n_pad = -(-n // V7X_LANES) * V7X_LANES
    if n_pad != n:
        b = jnp.pad(b, ((0, 0), (0, n_pad - n)))
    tn = _pick_tile(n_pad, (512, 256, 128))
    tm = _pick_tile(m, (512, 256, 128, 64, 32, 16, 8))
    out = pl.pallas_call(
        _matmul_kernel,
        grid=(n_pad // tn, m // tm),
        in_specs=[pl.BlockSpec((tm, k), lambda j, i: (i, 0)),
                  pl.BlockSpec((k, tn), lambda j, i: (0, j))],
        out_specs=pl.BlockSpec((tm, tn), lambda j, i: (i, j)),
        out_shape=jax.ShapeDtypeStruct((m, n_pad), F32),
        scratch_shapes=[pltpu.VMEM((k, tn), BF16)],
        compiler_params=pltpu.CompilerParams(
            dimension_semantics=("arbitrary", "arbitrary"),
            vmem_limit_bytes=V7X_VMEM_LIMIT_BYTES),
        name="matmul",
    )(a, b)
    return out[:, :n] if n_pad != n else out


def mm(x, w):
    lead = x.shape[:-1]
    return matmul(x.reshape(-1, x.shape[-1]), w).reshape(lead + (w.shape[-1],))


def _offsets(sizes):
    out, acc = [], 0
    for s in sizes[:-1]:
        acc += s
        out.append(acc)
    return out


def _rms(x, eps=NORM_EPS):
    xf = x.astype(F32)
    return xf * lax.rsqrt(jnp.mean(xf * xf, axis=-1, keepdims=True) + eps)


def rms_norm(x, g):
    return (_rms(x) * g.astype(F32)).astype(x.dtype)


def layer_norm(x, g, b, eps=NORM_EPS):
    xf = x.astype(F32)
    mu = jnp.mean(xf, axis=-1, keepdims=True)
    xc = xf - mu
    y = xc * lax.rsqrt(jnp.mean(xc * xc, axis=-1, keepdims=True) + eps)
    return (y * g.astype(F32) + b.astype(F32)).astype(x.dtype)


def partial_rope(x, pos):
    half = ROT_DIM // 2
    freqs = ROPE_THETA ** (-jnp.arange(half, dtype=F32) / half)
    ang = pos.astype(F32)[:, None] * freqs[None, :]
    ang = ang.reshape((1, ang.shape[0]) + (1,) * (x.ndim - 3) + (half,))
    cos, sin = jnp.cos(ang), jnp.sin(ang)
    xf = x.astype(F32)
    x1, x2 = xf[..., :half], xf[..., half:ROT_DIM]
    out = jnp.concatenate([x1 * cos - x2 * sin, x2 * cos + x1 * sin, xf[..., ROT_DIM:]], axis=-1)
    return out.astype(x.dtype)


def rwkv7_mix(sh, prev_row, s0, mu, w0, w_lora, a0, a_lora, k_k, k_a, r_k, gn_g, gn_b):
    bn, t, _ = sh.shape
    prev = jnp.concatenate([prev_row[:, None, :].astype(sh.dtype), sh[:, :-1]], axis=1)
    xm = sh + (prev - sh) * mu
    r, k, v, xw, xa = jnp.split(xm, [A_WIDTH, 2 * A_WIDTH, 3 * A_WIDTH, 3 * A_WIDTH + DECAY_LORA], axis=-1)
    wlog = -jax.nn.softplus(-(w0 + jnp.tanh(xw) @ w_lora)) - 0.5
    decay = jnp.exp(-jnp.exp(wlog.astype(F32)))
    a = jax.nn.sigmoid((a0 + xa @ a_lora).astype(F32))

    def heads(z):
        return z.astype(F32).reshape(bn, t, A_HEADS, A_HEAD)

    kk = heads(k * k_k)
    kk = kk / jnp.maximum(jnp.sqrt(jnp.sum(kk * kk, axis=-1, keepdims=True)), 1e-12)
    a_h = heads(a)
    k_h = heads(k) * (1.0 + (a_h - 1.0) * k_a.astype(F32).reshape(A_HEADS, A_HEAD))
    r_h, v_h, w_h = heads(r), heads(v), heads(decay)

    def step(S, inp):
        r_t, w_t, k_t, v_t, kk_t, a_t = inp
        sa = jnp.einsum('bhij,bhj->bhi', S, -kk_t)
        S = (S * w_t[:, :, None, :] + sa[..., None] * (kk_t * a_t)[:, :, None, :]
             + v_t[..., None] * k_t[:, :, None, :])
        return S, jnp.einsum('bhij,bhj->bhi', S, r_t)

    seq = tuple(jnp.moveaxis(z, 1, 0) for z in (r_h, w_h, k_h, v_h, kk, a_h))
    s_fin, y = lax.scan(step, s0.astype(F32), seq)
    y = jnp.moveaxis(y, 0, 1)
    ym = jnp.mean(y, axis=-1, keepdims=True)
    yc = y - ym
    yn = yc * lax.rsqrt(jnp.mean(yc * yc, axis=-1, keepdims=True) + GN_EPS)
    yn = yn.reshape(bn, t, A_WIDTH) * gn_g.astype(F32) + gn_b.astype(F32)
    bonus = (jnp.sum(r_h * k_h * r_k.astype(F32), axis=-1, keepdims=True) * v_h).reshape(bn, t, A_WIDTH)
    return (yn + bonus).astype(sh.dtype), s_fin.astype(s0.dtype), sh[:, -1]


def dsa_project(u, pos, q_norm, k_norm, qi_norm, w_qi, kidx_g, kidx_b):
    bn, t, _ = u.shape
    sizes = (B_WIDTH, B_KV_HEADS * HEAD_DIM, B_KV_HEADS * HEAD_DIM, IDX_Q_RANK, IDX_DIM, IDX_HEADS)
    q, k, v, cqi, kidx, wi = jnp.split(u, _offsets(sizes), axis=-1)
    q = partial_rope(rms_norm(q.reshape(bn, t, B_HEADS, HEAD_DIM), q_norm), pos)
    k = partial_rope(rms_norm(k.reshape(bn, t, B_KV_HEADS, HEAD_DIM), k_norm), pos)
    v = v.reshape(bn, t, B_KV_HEADS, HEAD_DIM)
    qi = mm(rms_norm(cqi, qi_norm), w_qi).reshape(bn, t, IDX_HEADS, IDX_DIM)
    qi = partial_rope(qi, pos)
    kidx = partial_rope(layer_norm(kidx, kidx_g, kidx_b), pos)
    wi = wi * IDX_HEADS ** -0.5
    return q, k, v, qi, kidx, wi


def index_topk(qi, wi, kidx, qpos, topk):
    s = jnp.einsum('bqhd,bkd->bqhk', qi.astype(F32), kidx.astype(F32)) * IDX_DIM ** -0.5
    score = jnp.einsum('bqhk,bqh->bqk', jax.nn.relu(s), wi.astype(F32))
    allowed = jnp.arange(kidx.shape[1])[None, :] <= qpos[:, None]
    score = jnp.where(allowed[None], score, -jnp.inf)
    _, idx = lax.top_k(score, topk)
    return idx, idx <= qpos[None, :, None]


def sparse_attn(q, kg, vg, valid):
    bn, tq, h, d = q.shape
    qg = q.reshape(bn, tq, B_KV_HEADS, h // B_KV_HEADS, d)
    s = jnp.einsum('bqhgd,bqkhd->bqhgk', qg, kg).astype(F32) * d ** -0.5
    s = jnp.where(valid[:, :, None, None, :], s, -jnp.inf)
    p = jax.nn.softmax(s, axis=-1)
    o = jnp.einsum('bqhgk,bqkhd->bqhgd', p.astype(vg.dtype), vg)
    return o.reshape(bn, tq, h * d)


def dsa_prompt(q, k, v, qi, wi, kidx):
    bn, s_len = q.shape[:2]
    topk = min(IDX_TOPK_MAX, s_len // 4)
    nb = s_len // QBLOCK
    bidx = jnp.arange(bn)[:, None, None]

    def blocks(z):
        return jnp.moveaxis(z.reshape((bn, nb, QBLOCK) + z.shape[2:]), 1, 0)

    def one(args):
        qb, qib, wib, start = args
        qpos = start + jnp.arange(QBLOCK)
        idx, valid = index_topk(qib, wib, kidx, qpos, topk)
        return sparse_attn(qb, k[bidx, idx], v[bidx, idx], valid)

    starts = jnp.arange(nb) * QBLOCK
    o = lax.map(one, (blocks(q), blocks(qi), blocks(wi), starts))
    return jnp.moveaxis(o, 0, 1).reshape(bn, s_len, -1)


def dsa_sample(q, k, v, qi, wi, kidx, cache_k, cache_v, cache_kidx, page_table, layer):
    bn, t = q.shape[:2]
    past = page_table.shape[1] * PAGE_SIZE
    topk = min(IDX_TOPK_MAX, (past + t) // 4)
    kidx_past = cache_kidx[layer, page_table].reshape(bn, past, IDX_DIM)
    kidx_all = jnp.concatenate([kidx_past, kidx.astype(kidx_past.dtype)], axis=1)
    qpos = past + jnp.arange(t)
    idx, valid = index_topk(qi, wi, kidx_all, qpos, topk)
    bidx = jnp.arange(bn)[:, None, None]
    pidx = jnp.minimum(idx, past - 1)
    phys = page_table[bidx, pidx // PAGE_SIZE]
    off = pidx % PAGE_SIZE
    nidx = jnp.clip(idx - past, 0, t - 1)
    is_new = (idx >= past)[..., None, None]
    kg = jnp.where(is_new, k[bidx, nidx], cache_k[layer, phys, off].astype(k.dtype))
    vg = jnp.where(is_new, v[bidx, nidx], cache_v[layer, phys, off].astype(v.dtype))
    return sparse_attn(q, kg, vg, valid)


def even_mixer(xn, pos, prev_row, s0, attend, ep):
    u = mm(xn, ep['w_in'])
    u_a, g_a, u_b, g_b = jnp.split(u, [SHIFT_W, IN_A, IN_A + IN_B - B_WIDTH], axis=-1)
    y_a, s_fin, last_row = rwkv7_mix(u_a, prev_row, s0, ep['shift_mu'], ep['w0'], ep['w_lora'],
                                     ep['a0'], ep['a_lora'], ep['k_k'], ep['k_a'], ep['r_k'],
                                     ep['gn_gain'], ep['gn_bias'])
    q, k, v, qi, kidx, wi = dsa_project(u_b, pos, ep['q_norm'], ep['k_norm'], ep['qi_norm'],
                                        ep['w_qi'], ep['kidx_gain'], ep['kidx_bias'])
    y_b = attend(q, k, v, qi, wi, kidx)
    y = jnp.concatenate([y_a * jax.nn.silu(g_a), y_b * jax.nn.silu(g_b)], axis=-1)
    return mm(y, ep['w_out']), (s_fin, last_row, k, v, kidx)


def dilated_prompt(q, k, v, window, dil):
    bn, s_len, h, d = q.shape
    wm = window // dil
    m_len = s_len // dil
    nb = -(-m_len // wm)
    mp = nb * wm

    def residues(z):
        z = jnp.moveaxis(z.reshape(bn, m_len, dil, h, d), 2, 1)
        return jnp.pad(z, ((0, 0), (0, 0), (0, mp - m_len), (0, 0), (0, 0)))

    def kblocks(z):
        z = jnp.pad(residues(z), ((0, 0), (0, 0), (wm, 0), (0, 0), (0, 0)))
        z = z.reshape(bn, dil, nb + 1, wm, h, d)
        return jnp.concatenate([z[:, :, :-1], z[:, :, 1:]], axis=3)

    qr = residues(q).reshape(bn, dil, nb, wm, h, d)
    kb, vb = kblocks(k), kblocks(v)
    s = jnp.einsum('brnqhd,brnkhd->brnhqk', qr, kb).astype(F32) * d ** -0.5
    iq = jnp.arange(wm)[:, None]
    ik = jnp.arange(2 * wm)[None, :]
    dist = iq + wm - ik
    mk = jnp.arange(nb)[:, None, None] * wm - wm + ik[None]
    mask = ((dist >= 0) & (dist <= wm))[None] & (mk >= 0)
    s = jnp.where(mask[None, None, :, None], s, -jnp.inf)
    lse = jax.nn.logsumexp(s, axis=-1)
    p = jnp.exp(s - lse[..., None])
    o = jnp.einsum('brnhqk,brnkhd->brnqhd', p.astype(v.dtype), vb)
    o = o.reshape(bn, dil, mp, h, d)[:, :, :m_len]
    o = jnp.moveaxis(o, 1, 2).reshape(bn, s_len, h, d)
    lse = jnp.moveaxis(lse, -1, -2).reshape(bn, dil, mp, h)[:, :, :m_len]
    lse = jnp.moveaxis(lse, 1, 2).reshape(bn, s_len, h)
    return o, lse


def dilated_sample(q, k, v, buf_k, buf_v, window, dil):
    t = q.shape[1]
    d = q.shape[-1]
    wb = buf_k.shape[1]
    wm = window // dil
    kc = jnp.concatenate([buf_k.astype(k.dtype), k], axis=1)
    vc = jnp.concatenate([buf_v.astype(v.dtype), v], axis=1)
    idx = wb + jnp.arange(t)[:, None] - jnp.arange(wm + 1)[None, :] * dil
    valid = idx >= 0
    idxc = jnp.maximum(idx, 0)
    kg, vg = kc[:, idxc], vc[:, idxc]
    s = jnp.einsum('bthd,btjhd->bthj', q, kg).astype(F32) * d ** -0.5
    s = jnp.where(valid[None, :, None, :], s, -jnp.inf)
    lse = jax.nn.logsumexp(s, axis=-1)
    p = jnp.exp(s - lse[..., None])
    o = jnp.einsum('bthj,btjhd->bthd', p.astype(vg.dtype), vg)
    return o, lse, kc[:, -wb:], vc[:, -wb:]


def window_attend_prompt(g, q, k, v, win, dil):
    o, lse = dilated_prompt(q, k, v, win, dil)
    keep = min(win, q.shape[1])
    return o, lse, k[:, -keep:], v[:, -keep:]


def window_attend_sample(g, q, k, v, win, dil, bufs_k, bufs_v, layer):
    return dilated_sample(q, k, v, bufs_k[g][layer], bufs_v[g][layer], win, dil)


def odd_mixer(xn, pos, attend, op):
    u = mm(xn, op['w_in'])
    bn, t, _ = u.shape
    n_g = len(C_GROUPS)
    qkv, gate = jnp.split(u, [3 * n_g * C_WIDTH], axis=-1)
    qkv = qkv.reshape(bn, t, n_g, 3, C_HEADS, HEAD_DIM)
    outs, lses, bufs = [], [], []
    for g, (win, dil) in enumerate(C_GROUPS):
        q = partial_rope(rms_norm(qkv[:, :, g, 0], op['q_norm'][g]), pos)
        k = partial_rope(rms_norm(qkv[:, :, g, 1], op['k_norm'][g]), pos)
        v = qkv[:, :, g, 2]
        o, lse, kb, vb = attend(g, q, k, v, win, dil)
        outs.append(o.astype(F32))
        lses.append(lse)
        bufs += [kb, vb]
    alpha = jax.nn.softmax(jnp.stack(lses, axis=0), axis=0)
    o = jnp.einsum('gbth,gbthd->bthd', alpha, jnp.stack(outs, axis=0))
    y = o.reshape(bn, t, C_WIDTH).astype(xn.dtype) * jax.nn.silu(gate)
    return mm(y, op['w_out']), tuple(bufs)


def ple_add(h, p_l, w_proj, w_gate):
    gate = jax.nn.sigmoid(mm(_rms(h).astype(h.dtype), w_gate))
    return h + gate * mm(p_l, w_proj)


def kernel(x_prompt, x_sample, p_prompt, p_sample, state_wkv, state_shift, cache_k, cache_v,
           cache_kidx, page_table, cache_win_k0, cache_win_v0, cache_win_k1, cache_win_v1,
           cache_win_k2, cache_win_v2, ln_gain, e_w_in, e_shift_mu, e_w0, e_w_lora, e_a0,
           e_a_lora, e_k_k, e_k_a, e_r_k, e_gn_gain, e_gn_bias, e_q_norm, e_k_norm, e_qi_norm,
           e_w_qi, e_kidx_gain, e_kidx_bias, e_w_out, o_w_in, o_q_norm, o_k_norm, o_w_out,
           ple_w_proj, ple_w_gate):
    depth = ln_gain.shape[0]
    bp, s_len, _ = x_prompt.shape
    t_len = x_sample.shape[1]
    past = page_table.shape[1] * PAGE_SIZE
    pos_p = jnp.arange(s_len, dtype=jnp.int32)
    pos_s = past + jnp.arange(t_len, dtype=jnp.int32)
    bufs_k = (cache_win_k0, cache_win_k1, cache_win_k2)
    bufs_v = (cache_win_v0, cache_win_v1, cache_win_v2)
    hp, hs = x_prompt, x_sample
    ev_p, ev_s, od_p, od_s = [], [], [], []
    for i in range(depth):
        l = i // 2
        xp = rms_norm(hp, ln_gain[i])
        xs = rms_norm(hs, ln_gain[i])
        if i % 2 == 0:
            ep = {'w_in': e_w_in[l], 'shift_mu': e_shift_mu[l], 'w0': e_w0[l], 'w_lora': e_w_lora[l],
                  'a0': e_a0[l], 'a_lora': e_a_lora[l], 'k_k': e_k_k[l], 'k_a': e_k_a[l],
                  'r_k': e_r_k[l], 'gn_gain': e_gn_gain[l], 'gn_bias': e_gn_bias[l],
                  'q_norm': e_q_norm[l], 'k_norm': e_k_norm[l], 'qi_norm': e_qi_norm[l],
                  'w_qi': e_w_qi[l], 'kidx_gain': e_kidx_gain[l], 'kidx_bias': e_kidx_bias[l],
                  'w_out': e_w_out[l]}
            row0 = jnp.zeros((bp, SHIFT_W), hp.dtype)
            st0 = jnp.zeros((bp, A_HEADS, A_HEAD, A_HEAD), hp.dtype)
            mp, stp = even_mixer(xp, pos_p, row0, st0, dsa_prompt, ep)
            att_s = functools.partial(dsa_sample, cache_k=cache_k, cache_v=cache_v,
                                      cache_kidx=cache_kidx, page_table=page_table, layer=l)
            ms, sts = even_mixer(xs, pos_s, state_shift[l], state_wkv[l], att_s, ep)
            ev_p.append(stp)
            ev_s.append(sts)
        else:
            op = {'w_in': o_w_in[l], 'q_norm': o_q_norm[l], 'k_norm': o_k_norm[l], 'w_out': o_w_out[l]}
            mp, stp = odd_mixer(xp, pos_p, window_attend_prompt, op)
            att_s = functools.partial(window_attend_sample, bufs_k=bufs_k, bufs_v=bufs_v, layer=l)
            ms, sts = odd_mixer(xs, pos_s, att_s, op)
            od_p.append(stp)
            od_s.append(sts)
        hp = ple_add(hp + mp, p_prompt[i], ple_w_proj[i], ple_w_gate[i])
        hs = ple_add(hs + ms, p_sample[i], ple_w_proj[i], ple_w_gate[i])

    def st(lst, j):
        return jnp.stack([e[j] for e in lst], axis=0)

    outs = [hp, hs, st(ev_p, 0), st(ev_s, 0), st(ev_p, 1), st(ev_s, 1)]
    outs += [st(ev_p, j) for j in (2, 3, 4)] + [st(ev_s, j) for j in (2, 3, 4)]
    outs += [st(od_p, j) for j in range(6)] + [st(od_s, j) for j in range(6)]
    return tuple(outs)
```

```python
import functools

import jax
import jax.numpy as jnp
from jax import lax
from jax.experimental import pallas as pl
from jax.experimental.pallas import tpu as pltpu

F32 = jnp.float32
BF16 = jnp.bfloat16
I32 = jnp.int32

D_MODEL = 4096
PAGE_SIZE = 128
HEAD_DIM = 128
ROT_DIM = HEAD_DIM // 4
ROPE_THETA = 500000.0
NORM_EPS = 1e-6

A_WIDTH = D_MODEL // 2
A_HEAD = 64
A_HEADS = A_WIDTH // A_HEAD
DECAY_LORA = 96
AAA_LORA = 96
GN_EPS = 64e-5
SHIFT_W = 3 * A_WIDTH + DECAY_LORA + AAA_LORA

B_WIDTH = D_MODEL // 2
B_HEADS = B_WIDTH // HEAD_DIM
B_KV_HEADS = 4
IDX_HEADS = 16
IDX_DIM = 128
IDX_Q_RANK = 512
IDX_TOPK_MAX = 256
QBLOCK = 128

C_GROUPS = ((128, 1), (512, 4), (2048, 16))
C_HEADS = 16
C_WIDTH = C_HEADS * HEAD_DIM

IN_A = SHIFT_W + A_WIDTH
IN_B = B_WIDTH + 2 * B_KV_HEADS * HEAD_DIM + IDX_Q_RANK + IDX_DIM + IDX_HEADS + B_WIDTH

V7X_LANES = 128
V7X_VMEM_LIMIT_BYTES = 56 * 1024 * 1024
INT_MIN = -2 ** 31

_NT = (((1,), (1,)), ((), ()))


def _matmul_kernel(a_ref, b_ref, o_ref, bq_ref):
    @pl.when(pl.program_id(1) == 0)
    def _():
        bq_ref[...] = b_ref[...].astype(BF16)

    o_ref[...] = jnp.dot(a_ref[...].astype(BF16), bq_ref[...], preferred_element_type=F32)


def _pick_tile(n, cands):
    for c in cands:
        if n % c == 0:
            return c
    return n


def matmul(a, b):
    m, k = a.shape
    _, n = b.shape
    n_pad = -(-n // V7X_LANES) * V7X_LANES
    if n_pad != n:
        b = jnp.pad(b, ((0, 0), (0, n_pad - n)))
    tn = _pick_tile(n_pad, (512, 256, 128))
    tm = _pick_tile(m, (1024, 512, 256, 128, 64, 32, 16, 8))
    out = pl.pallas_call(
        _matmul_kernel,
        grid=(n_pad // tn, m // tm),
        in_specs=[pl.BlockSpec((tm, k), lambda j, i: (i, 0)),
                  pl.BlockSpec((k, tn), lambda j, i: (0, j))],
        out_specs=pl.BlockSpec((tm, tn), lambda j, i: (i, j)),
        out_shape=jax.ShapeDtypeStruct((m, n_pad), F32),
        scratch_shapes=[pltpu.VMEM((k, tn), BF16)],
        compiler_params=pltpu.CompilerParams(
            dimension_semantics=("arbitrary", "arbitrary"),
            vmem_limit_bytes=V7X_VMEM_LIMIT_BYTES),
        name="matmul",
    )(a, b)
    return out[:, :n] if n_pad != n else out


def mm(x, w):
    lead = x.shape[:-1]
    return matmul(x.reshape(-1, x.shape[-1]).astype(BF16), w).reshape(lead + (w.shape[-1],))


def _rwkv_scan_kernel(nkk_ref, w_ref, kka_ref, k_ref, v_ref, q_ref, vc_ref, s0_ref,
                      yt_ref, sout_ref, s_scr, *, pairs, steps):
    tchunk = pl.program_id(2)

    @pl.when(tchunk == 0)
    def _():
        for p in range(pairs):
            s_scr[p] = jnp.concatenate([s0_ref[0, 2 * p], s0_ref[0, 2 * p + 1]], axis=1)

    lane = lax.broadcasted_iota(I32, (A_HEAD, V7X_LANES), 1)
    row = lax.broadcasted_iota(I32, (A_HEAD, V7X_LANES), 0)
    lo = lane < A_HEAD
    eye_lo = lane == row
    eye_hi = lane == row + A_HEAD
    eye = jnp.logical_or(eye_lo, eye_hi)
    lane_t = lax.broadcasted_iota(I32, (A_HEAD, steps), 1)
    yt_ref[...] = jnp.zeros(yt_ref.shape, F32)

    def seg_sum(x):
        s_lo = jnp.sum(jnp.where(lo, x, 0.0), axis=1, keepdims=True)
        s_hi = jnp.sum(jnp.where(lo, 0.0, x), axis=1, keepdims=True)
        return s_lo, s_hi

    def step(t, carry):
        for p in range(pairs):
            def rowvec(ref):
                return jnp.broadcast_to(ref[p, pl.ds(t, 1), :], (A_HEAD, V7X_LANES))

            s = s_scr[p]
            sa_lo, sa_hi = seg_sum(s * rowvec(nkk_ref))
            y_lo, y_hi = seg_sum(s * rowvec(q_ref) + jnp.where(eye, rowvec(vc_ref), 0.0))
            vrow = rowvec(v_ref)
            v_lo = jnp.sum(jnp.where(eye_lo, vrow, 0.0), axis=1, keepdims=True)
            v_hi = jnp.sum(jnp.where(eye_hi, vrow, 0.0), axis=1, keepdims=True)
            sa_b = jnp.where(lo, sa_lo, sa_hi)
            v_b = jnp.where(lo, v_lo, v_hi)
            s_scr[p] = s * rowvec(w_ref) + sa_b * rowvec(kka_ref) + v_b * rowvec(k_ref)
            r0 = p * V7X_LANES
            yt_ref[0, r0:r0 + A_HEAD, :] = jnp.where(lane_t == t, y_lo, yt_ref[0, r0:r0 + A_HEAD, :])
            yt_ref[0, r0 + A_HEAD:r0 + V7X_LANES, :] = jnp.where(
                lane_t == t, y_hi, yt_ref[0, r0 + A_HEAD:r0 + V7X_LANES, :])
        return carry

    lax.fori_loop(0, steps, step, 0)

    @pl.when(tchunk == pl.num_programs(2) - 1)
    def _():
        for p in range(pairs):
            s = s_scr[p]
            sout_ref[0, 2 * p] = s[:, :A_HEAD]
            sout_ref[0, 2 * p + 1] = s[:, A_HEAD:]


RWKV_PAIRS_PER_STEP = 4


def rwkv_scan(nkk, w, kka, k, v, q, vc, s0):
    b, npairs, t, _ = nkk.shape
    pairs = RWKV_PAIRS_PER_STEP
    c = npairs * V7X_LANES
    tc = min(t, V7X_LANES)
    assert t % tc == 0 and npairs % pairs == 0
    bw = pairs * V7X_LANES
    row_spec = pl.BlockSpec((None, pairs, tc, V7X_LANES), lambda bi, hi, ti: (bi, hi, ti, 0))
    st_spec = pl.BlockSpec((1, 2 * pairs, A_HEAD, A_HEAD), lambda bi, hi, ti: (bi, hi, 0, 0))
    return pl.pallas_call(
        functools.partial(_rwkv_scan_kernel, pairs=pairs, steps=tc),
        grid=(b, npairs // pairs, t // tc),
        in_specs=[row_spec] * 7 + [st_spec],
        out_specs=[pl.BlockSpec((1, bw, tc), lambda bi, hi, ti: (bi, hi, ti)), st_spec],
        out_shape=[jax.ShapeDtypeStruct((b, c, t), F32), jax.ShapeDtypeStruct(s0.shape, F32)],
        scratch_shapes=[pltpu.VMEM((pairs, A_HEAD, V7X_LANES), F32)],
        compiler_params=pltpu.CompilerParams(dimension_semantics=("arbitrary", "arbitrary", "arbitrary")),
        name="rwkv_scan",
    )(nkk, w, kka, k, v, q, vc, s0)


def _select_topk_mask(score, allowed, topk):
    r, l = score.shape
    score = jnp.where(score == 0.0, 0.0, score)
    bits = pltpu.bitcast(score, I32)
    key = jnp.where(bits < 0, bits ^ jnp.int32(0x7FFFFFFF), bits)
    key = jnp.where(allowed, key, jnp.int32(INT_MIN))
    kf = jnp.float32(topk)

    def count(pred):
        return jnp.sum(jnp.where(pred, 1.0, 0.0), axis=1, keepdims=True)

    def bit_step(i, prefix):
        cand = prefix | lax.shift_left(jnp.int32(1), jnp.int32(31) - i)
        ok = count(key >= (cand ^ jnp.int32(INT_MIN))) >= kf
        return jnp.where(ok, cand, prefix)

    prefix = lax.fori_loop(0, 32, bit_step, jnp.zeros((r, 1), I32))
    thr = prefix ^ jnp.int32(INT_MIN)
    gt = key > thr
    eq = jnp.logical_and(key == thr, allowed)
    need = kf - count(gt)
    li = lax.broadcasted_iota(I32, (V7X_LANES, 2 * V7X_LANES), 0)
    lj = lax.broadcasted_iota(I32, (V7X_LANES, 2 * V7X_LANES), 1)
    tri_ones = jnp.where(jnp.logical_or(lj >= V7X_LANES, li < lj), 1.0, 0.0).astype(BF16)
    running = jnp.zeros((r, V7X_LANES), F32)
    sel = []
    for c in range(l // V7X_LANES):
        sl = slice(c * V7X_LANES, (c + 1) * V7X_LANES)
        eq_c = eq[:, sl]
        res = jnp.dot(jnp.where(eq_c, 1.0, 0.0).astype(BF16), tri_ones, preferred_element_type=F32)
        before = res[:, :V7X_LANES] + running
        running = running + res[:, V7X_LANES:]
        sel.append(jnp.logical_or(gt[:, sl], jnp.logical_and(eq_c, before < need)))
    return jnp.concatenate(sel, axis=1)


def _dsa_prompt_kernel(qi_ref, qil_ref, wi_ref, kidx_ref, kidxl_ref, q_ref, k_ref, v_ref, o_ref, *, topk):
    qb = pl.program_id(1)
    l = kidx_ref.shape[0]
    kidx = kidx_ref[...]
    kidx_lo = kidxl_ref[...]
    score = jnp.zeros((QBLOCK, l), F32)
    for h in range(IDX_HEADS):
        hs = slice(h * IDX_DIM, (h + 1) * IDX_DIM)
        s = (lax.dot_general(qi_ref[:, hs], kidx, _NT, preferred_element_type=F32)
             + lax.dot_general(qi_ref[:, hs], kidx_lo, _NT, preferred_element_type=F32)
             + lax.dot_general(qil_ref[:, hs], kidx, _NT, preferred_element_type=F32))
        s = jnp.maximum(s * IDX_DIM ** -0.5, 0.0)
        score = score + s * wi_ref[:, h:h + 1]
    qpos = qb * QBLOCK + lax.broadcasted_iota(I32, (QBLOCK, l), 0)
    kpos = lax.broadcasted_iota(I32, (QBLOCK, l), 1)
    sel = _select_topk_mask(score, kpos <= qpos, topk)
    bias = jnp.where(sel, 0.0, -jnp.inf)
    group = B_HEADS // B_KV_HEADS
    for g in range(B_KV_HEADS):
        kg = k_ref[:, g * HEAD_DIM:(g + 1) * HEAD_DIM]
        vg = v_ref[:, g * HEAD_DIM:(g + 1) * HEAD_DIM]
        for j in range(group):
            h = g * group + j
            s = lax.dot_general(q_ref[:, h * HEAD_DIM:(h + 1) * HEAD_DIM], kg, _NT, preferred_element_type=F32)
            s = s * HEAD_DIM ** -0.5 + bias
            m = jnp.max(s, axis=1, keepdims=True)
            p = jnp.exp(s - m)
            denom = jnp.sum(p, axis=1, keepdims=True)
            o = jnp.dot(p.astype(BF16), vg, preferred_element_type=F32)
            o_ref[:, h * HEAD_DIM:(h + 1) * HEAD_DIM] = o / denom


def dsa_prompt_attend(q, k, v, qi, wi, kidx):
    b, s_len = q.shape[:2]
    topk = min(IDX_TOPK_MAX, s_len // 4)

    def flat16(z):
        return z.reshape(b, s_len, -1).astype(BF16)

    def flat16_lo(z):
        z = z.reshape(b, s_len, -1)
        return (z - z.astype(BF16).astype(F32)).astype(BF16)

    def qspec(w):
        return pl.BlockSpec((None, QBLOCK, w), lambda bi, qb: (bi, qb, 0))

    def kspec(w):
        return pl.BlockSpec((None, s_len, w), lambda bi, qb: (bi, 0, 0))

    return pl.pallas_call(
        functools.partial(_dsa_prompt_kernel, topk=topk),
        grid=(b, s_len // QBLOCK),
        in_specs=[qspec(B_WIDTH), qspec(B_WIDTH), qspec(IDX_HEADS), kspec(IDX_DIM), kspec(IDX_DIM),
                  qspec(B_WIDTH), kspec(B_KV_HEADS * HEAD_DIM), kspec(B_KV_HEADS * HEAD_DIM)],
        out_specs=qspec(B_WIDTH),
        out_shape=jax.ShapeDtypeStruct((b, s_len, B_WIDTH), F32),
        compiler_params=pltpu.CompilerParams(dimension_semantics=("arbitrary", "arbitrary"),
                                             vmem_limit_bytes=V7X_VMEM_LIMIT_BYTES),
        name="dsa_prompt",
    )(flat16(qi), flat16_lo(qi), wi, flat16(kidx), flat16_lo(kidx), flat16(q), flat16(k), flat16(v))


def _offsets(sizes):
    out, acc = [], 0
    for s in sizes[:-1]:
        acc += s
        out.append(acc)
    return out


def _rms(x, eps=NORM_EPS):
    xf = x.astype(F32)
    return xf * lax.rsqrt(jnp.mean(xf * xf, axis=-1, keepdims=True) + eps)


def rms_norm(x, g):
    return (_rms(x) * g.astype(F32)).astype(x.dtype)


def layer_norm(x, g, b, eps=NORM_EPS):
    xf = x.astype(F32)
    mu = jnp.mean(xf, axis=-1, keepdims=True)
    xc = xf - mu
    y = xc * lax.rsqrt(jnp.mean(xc * xc, axis=-1, keepdims=True) + eps)
    return (y * g.astype(F32) + b.astype(F32)).astype(x.dtype)


def partial_rope(x, pos):
    half = ROT_DIM // 2
    freqs = ROPE_THETA ** (-jnp.arange(half, dtype=F32) / half)
    ang = pos.astype(F32)[:, None] * freqs[None, :]
    ang = ang.reshape((1, ang.shape[0]) + (1,) * (x.ndim - 3) + (half,))
    cos, sin = jnp.cos(ang), jnp.sin(ang)
    xf = x.astype(F32)
    x1, x2 = xf[..., :half], xf[..., half:ROT_DIM]
    out = jnp.concatenate([x1 * cos - x2 * sin, x2 * cos + x1 * sin, xf[..., ROT_DIM:]], axis=-1)
    return out.astype(x.dtype)


def rwkv7_mix(sh, prev_row, s0, mu, w0, w_lora, a0, a_lora, k_k, k_a, r_k, gn_g, gn_b):
    bn, t, _ = sh.shape
    prev = jnp.concatenate([prev_row[:, None, :].astype(sh.dtype), sh[:, :-1]], axis=1)
    xm = sh + (prev - sh) * mu
    r, k, v, xw, xa = jnp.split(xm, [A_WIDTH, 2 * A_WIDTH, 3 * A_WIDTH, 3 * A_WIDTH + DECAY_LORA], axis=-1)
    wlog = -jax.nn.softplus(-(w0 + mm(jnp.tanh(xw), w_lora))) - 0.5
    decay = jnp.exp(-jnp.exp(wlog.astype(F32)))
    a = jax.nn.sigmoid((a0 + mm(xa, a_lora)).astype(F32))

    def heads(z):
        return z.astype(F32).reshape(bn, t, A_HEADS, A_HEAD)

    def head_sum(z):
        return jnp.broadcast_to(jnp.sum(heads(z), axis=-1, keepdims=True),
                                (bn, t, A_HEADS, A_HEAD)).reshape(bn, t, A_WIDTH)

    kk = heads(k * k_k)
    kk = kk / jnp.maximum(jnp.sqrt(jnp.sum(kk * kk, axis=-1, keepdims=True)), 1e-12)
    kk = kk.reshape(bn, t, A_WIDTH)
    k2 = k * (1.0 + (a - 1.0) * k_a.astype(F32))
    kka = kk * a
    q = decay * r - kk * head_sum(kka * r)
    vc = v * head_sum(k2 * r)

    def pair_major(z):
        return jnp.swapaxes(z.reshape(bn, t, A_WIDTH // V7X_LANES, V7X_LANES), 1, 2)

    yt, s_fin = rwkv_scan(*[pair_major(z) for z in (-kk, decay, kka, k2, v, q, vc)], s0.astype(F32))
    y = heads(jnp.swapaxes(yt, 1, 2))
    ym = jnp.mean(y, axis=-1, keepdims=True)
    yc = y - ym
    yn = yc * lax.rsqrt(jnp.mean(yc * yc, axis=-1, keepdims=True) + GN_EPS)
    yn = yn.reshape(bn, t, A_WIDTH) * gn_g.astype(F32) + gn_b.astype(F32)
    bonus = (jnp.sum(heads(r * k2) * r_k.astype(F32), axis=-1, keepdims=True) * heads(v)).reshape(bn, t, A_WIDTH)
    return (yn + bonus).astype(sh.dtype), s_fin.astype(s0.dtype), sh[:, -1]


def dsa_project(u, pos, q_norm, k_norm, qi_norm, w_qi, kidx_g, kidx_b):
    bn, t, _ = u.shape
    sizes = (B_WIDTH, B_KV_HEADS * HEAD_DIM, B_KV_HEADS * HEAD_DIM, IDX_Q_RANK, IDX_DIM, IDX_HEADS)
    q, k, v, cqi, kidx, wi = jnp.split(u, _offsets(sizes), axis=-1)
    q = partial_rope(rms_norm(q.reshape(bn, t, B_HEADS, HEAD_DIM), q_norm), pos)
    k = partial_rope(rms_norm(k.reshape(bn, t, B_KV_HEADS, HEAD_DIM), k_norm), pos)
    v = v.reshape(bn, t, B_KV_HEADS, HEAD_DIM)
    qi = mm(rms_norm(cqi, qi_norm), w_qi).reshape(bn, t, IDX_HEADS, IDX_DIM)
    qi = partial_rope(qi, pos)
    kidx = partial_rope(layer_norm(kidx, kidx_g, kidx_b), pos)
    wi = wi * IDX_HEADS ** -0.5
    return q, k, v, qi, kidx, wi


def index_topk(qi, wi, kidx, qpos, topk):
    s = jnp.einsum('bqhd,bkd->bqhk', qi.astype(F32), kidx.astype(F32)) * IDX_DIM ** -0.5
    score = jnp.einsum('bqhk,bqh->bqk', jax.nn.relu(s), wi.astype(F32))
    allowed = jnp.arange(kidx.shape[1])[None, :] <= qpos[:, None]
    score = jnp.where(allowed[None], score, -jnp.inf)
    _, idx = lax.top_k(score, topk)
    return idx, idx <= qpos[None, :, None]


def sparse_attn(q, kg, vg, valid):
    bn, tq, h, d = q.shape
    qg = q.reshape(bn, tq, B_KV_HEADS, h // B_KV_HEADS, d)
    s = jnp.einsum('bqhgd,bqkhd->bqhgk', qg, kg).astype(F32) * d ** -0.5
    s = jnp.where(valid[:, :, None, None, :], s, -jnp.inf)
    p = jax.nn.softmax(s, axis=-1)
    o = jnp.einsum('bqhgk,bqkhd->bqhgd', p.astype(vg.dtype), vg)
    return o.reshape(bn, tq, h * d)


def dsa_sample(q, k, v, qi, wi, kidx, cache_k, cache_v, cache_kidx, page_table, layer):
    bn, t = q.shape[:2]
    past = page_table.shape[1] * PAGE_SIZE
    topk = min(IDX_TOPK_MAX, (past + t) // 4)
    kidx_past = cache_kidx[layer, page_table].reshape(bn, past, IDX_DIM)
    kidx_all = jnp.concatenate([kidx_past, kidx.astype(kidx_past.dtype)], axis=1)
    qpos = past + jnp.arange(t)
    idx, valid = index_topk(qi, wi, kidx_all, qpos, topk)
    bidx = jnp.arange(bn)[:, None, None]
    pidx = jnp.minimum(idx, past - 1)
    phys = page_table[bidx, pidx // PAGE_SIZE]
    off = pidx % PAGE_SIZE
    nidx = jnp.clip(idx - past, 0, t - 1)
    is_new = (idx >= past)[..., None, None]
    kg = jnp.where(is_new, k[bidx, nidx], cache_k[layer, phys, off].astype(k.dtype))
    vg = jnp.where(is_new, v[bidx, nidx], cache_v[layer, phys, off].astype(v.dtype))
    return sparse_attn(q, kg, vg, valid)


def even_mixer(xn, pos, prev_row, s0, attend, ep):
    u = mm(xn, ep['w_in'])
    u_a, g_a, u_b, g_b = jnp.split(u, [SHIFT_W, IN_A, IN_A + IN_B - B_WIDTH], axis=-1)
    y_a, s_fin, last_row = rwkv7_mix(u_a, prev_row, s0, ep['shift_mu'], ep['w0'], ep['w_lora'],
                                     ep['a0'], ep['a_lora'], ep['k_k'], ep['k_a'], ep['r_k'],
                                     ep['gn_gain'], ep['gn_bias'])
    q, k, v, qi, kidx, wi = dsa_project(u_b, pos, ep['q_norm'], ep['k_norm'], ep['qi_norm'],
                                        ep['w_qi'], ep['kidx_gain'], ep['kidx_bias'])
    y_b = attend(q, k, v, qi, wi, kidx)
    y = jnp.concatenate([y_a * jax.nn.silu(g_a), y_b * jax.nn.silu(g_b)], axis=-1)
    return mm(y, ep['w_out']), (s_fin, last_row, k, v, kidx)


def dilated_prompt(q, k, v, window, dil):
    bn, s_len, h, d = q.shape
    wm = window // dil
    m_len = s_len // dil
    nb = -(-m_len // wm)
    mp = nb * wm

    def residues(z):
        z = jnp.moveaxis(z.reshape(bn, m_len, dil, h, d), 2, 1)
        return jnp.pad(z, ((0, 0), (0, 0), (0, mp - m_len), (0, 0), (0, 0)))

    def kblocks(z):
        z = jnp.pad(residues(z), ((0, 0), (0, 0), (wm, 0), (0, 0), (0, 0)))
        z = z.reshape(bn, dil, nb + 1, wm, h, d)
        return jnp.concatenate([z[:, :, :-1], z[:, :, 1:]], axis=3)

    qr = residues(q).reshape(bn, dil, nb, wm, h, d)
    kb, vb = kblocks(k), kblocks(v)
    s = jnp.einsum('brnqhd,brnkhd->brnhqk', qr, kb).astype(F32) * d ** -0.5
    iq = jnp.arange(wm)[:, None]
    ik = jnp.arange(2 * wm)[None, :]
    dist = iq + wm - ik
    mk = jnp.arange(nb)[:, None, None] * wm - wm + ik[None]
    mask = ((dist >= 0) & (dist <= wm))[None] & (mk >= 0)
    s = jnp.where(mask[None, None, :, None], s, -jnp.inf)
    lse = jax.nn.logsumexp(s, axis=-1)
    p = jnp.exp(s - lse[..., None])
    o = jnp.einsum('brnhqk,brnkhd->brnqhd', p.astype(v.dtype), vb)
    o = o.reshape(bn, dil, mp, h, d)[:, :, :m_len]
    o = jnp.moveaxis(o, 1, 2).reshape(bn, s_len, h, d)
    lse = jnp.moveaxis(lse, -1, -2).reshape(bn, dil, mp, h)[:, :, :m_len]
    lse = jnp.moveaxis(lse, 1, 2).reshape(bn, s_len, h)
    return o, lse


def dilated_sample(q, k, v, buf_k, buf_v, window, dil):
    t = q.shape[1]
    d = q.shape[-1]
    wb = buf_k.shape[1]
    wm = window // dil
    kc = jnp.concatenate([buf_k.astype(k.dtype), k], axis=1)
    vc = jnp.concatenate([buf_v.astype(v.dtype), v], axis=1)
    idx = wb + jnp.arange(t)[:, None] - jnp.arange(wm + 1)[None, :] * dil
    valid = idx >= 0
    idxc = jnp.maximum(idx, 0)
    kg, vg = kc[:, idxc], vc[:, idxc]
    s = jnp.einsum('bthd,btjhd->bthj', q, kg).astype(F32) * d ** -0.5
    s = jnp.where(valid[None, :, None, :], s, -jnp.inf)
    lse = jax.nn.logsumexp(s, axis=-1)
    p = jnp.exp(s - lse[..., None])
    o = jnp.einsum('bthj,btjhd->bthd', p.astype(vg.dtype), vg)
    return o, lse, kc[:, -wb:], vc[:, -wb:]


def window_attend_prompt(g, q, k, v, win, dil):
    o, lse = dilated_prompt(q, k, v, win, dil)
    keep = min(win, q.shape[1])
    return o, lse, k[:, -keep:], v[:, -keep:]


def window_attend_sample(g, q, k, v, win, dil, bufs_k, bufs_v, layer):
    return dilated_sample(q, k, v, bufs_k[g][layer], bufs_v[g][layer], win, dil)


def odd_mixer(xn, pos, attend, op):
    u = mm(xn, op['w_in'])
    bn, t, _ = u.shape
    n_g = len(C_GROUPS)
    qkv, gate = jnp.split(u, [3 * n_g * C_WIDTH], axis=-1)
    qkv = qkv.reshape(bn, t, n_g, 3, C_HEADS, HEAD_DIM)
    outs, lses, bufs = [], [], []
    for g, (win, dil) in enumerate(C_GROUPS):
        q = partial_rope(rms_norm(qkv[:, :, g, 0], op['q_norm'][g]), pos)
        k = partial_rope(rms_norm(qkv[:, :, g, 1], op['k_norm'][g]), pos)
        v = qkv[:, :, g, 2]
        o, lse, kb, vb = attend(g, q, k, v, win, dil)
        outs.append(o.astype(F32))
        lses.append(lse)
        bufs += [kb, vb]
    alpha = jax.nn.softmax(jnp.stack(lses, axis=0), axis=0)
    o = jnp.einsum('gbth,gbthd->bthd', alpha, jnp.stack(outs, axis=0))
    y = o.reshape(bn, t, C_WIDTH).astype(xn.dtype) * jax.nn.silu(gate)
    return mm(y, op['w_out']), tuple(bufs)


def ple_add(h, p_l, w_proj, w_gate):
    gate = jax.nn.sigmoid(mm(_rms(h).astype(h.dtype), w_gate))
    return h + gate * mm(p_l, w_proj)


def kernel(x_prompt, x_sample, p_prompt, p_sample, state_wkv, state_shift, cache_k, cache_v,
           cache_kidx, page_table, cache_win_k0, cache_win_v0, cache_win_k1, cache_win_v1,
           cache_win_k2, cache_win_v2, ln_gain, e_w_in, e_shift_mu, e_w0, e_w_lora, e_a0,
           e_a_lora, e_k_k, e_k_a, e_r_k, e_gn_gain, e_gn_bias, e_q_norm, e_k_norm, e_qi_norm,
           e_w_qi, e_kidx_gain, e_kidx_bias, e_w_out, o_w_in, o_q_norm, o_k_norm, o_w_out,
           ple_w_proj, ple_w_gate):
    depth = ln_gain.shape[0]
    bp, s_len, _ = x_prompt.shape
    t_len = x_sample.shape[1]
    past = page_table.shape[1] * PAGE_SIZE
    pos_p = jnp.arange(s_len, dtype=jnp.int32)
    pos_s = past + jnp.arange(t_len, dtype=jnp.int32)
    bufs_k = (cache_win_k0, cache_win_k1, cache_win_k2)
    bufs_v = (cache_win_v0, cache_win_v1, cache_win_v2)
    hp, hs = x_prompt, x_sample
    ev_p, ev_s, od_p, od_s = [], [], [], []
    for i in range(depth):
        l = i // 2
        xp = rms_norm(hp, ln_gain[i])
        xs = rms_norm(hs, ln_gain[i])
        if i % 2 == 0:
            ep = {'w_in': e_w_in[l], 'shift_mu': e_shift_mu[l], 'w0': e_w0[l], 'w_lora': e_w_lora[l],
                  'a0': e_a0[l], 'a_lora': e_a_lora[l], 'k_k': e_k_k[l], 'k_a': e_k_a[l],
                  'r_k': e_r_k[l], 'gn_gain': e_gn_gain[l], 'gn_bias': e_gn_bias[l],
                  'q_norm': e_q_norm[l], 'k_norm': e_k_norm[l], 'qi_norm': e_qi_norm[l],
                  'w_qi': e_w_qi[l], 'kidx_gain': e_kidx_gain[l], 'kidx_bias': e_kidx_bias[l],
                  'w_out': e_w_out[l]}
            row0 = jnp.zeros((bp, SHIFT_W), hp.dtype)
            st0 = jnp.zeros((bp, A_HEADS, A_HEAD, A_HEAD), hp.dtype)
            mp, stp = even_mixer(xp, pos_p, row0, st0, dsa_prompt_attend, ep)
            att_s = functools.partial(dsa_sample, cache_k=cache_k, cache_v=cache_v,
                                      cache_kidx=cache_kidx, page_table=page_table, layer=l)
            ms, sts = even_mixer(xs, pos_s, state_shift[l], state_wkv[l], att_s, ep)
            ev_p.append(stp)
            ev_s.append(sts)
        else:
            op = {'w_in': o_w_in[l], 'q_norm': o_q_norm[l], 'k_norm': o_k_norm[l], 'w_out': o_w_out[l]}
            mp, stp = odd_mixer(xp, pos_p, window_attend_prompt, op)
            att_s = functools.partial(window_attend_sample, bufs_k=bufs_k, bufs_v=bufs_v, layer=l)
            ms, sts = odd_mixer(xs, pos_s, att_s, op)
            od_p.append(stp)
            od_s.append(sts)
        hp = ple_add(hp + mp, p_prompt[i], ple_w_proj[i], ple_w_gate[i])
        hs = ple_add(hs + ms, p_sample[i], ple_w_proj[i], ple_w_gate[i])

    def st(lst, j):
        return jnp.stack([e[j] for e in lst], axis=0)

    outs = [hp, hs, st(ev_p, 0), st(ev_s, 0), st(ev_p, 1), st(ev_s, 1)]
    outs += [st(ev_p, j) for j in (2, 3, 4)] + [st(ev_s, j) for j in (2, 3, 4)]
    outs += [st(od_p, j) for j in range(6)] + [st(od_s, j) for j in range(6)]
    return tuple(outs)
```

```python
import functools

import jax
import jax.numpy as jnp
from jax import lax
from jax.experimental import pallas as pl
from jax.experimental.pallas import tpu as pltpu

F32 = jnp.float32
BF16 = jnp.bfloat16
I32 = jnp.int32

D_MODEL = 4096
PAGE_SIZE = 128
HEAD_DIM = 128
ROT_DIM = HEAD_DIM // 4
ROPE_THETA = 500000.0
NORM_EPS = 1e-6

A_WIDTH = D_MODEL // 2
A_HEAD = 64
A_HEADS = A_WIDTH // A_HEAD
DECAY_LORA = 96
AAA_LORA = 96
GN_EPS = 64e-5
SHIFT_W = 3 * A_WIDTH + DECAY_LORA + AAA_LORA

B_WIDTH = D_MODEL // 2
B_HEADS = B_WIDTH // HEAD_DIM
B_KV_HEADS = 4
IDX_HEADS = 16
IDX_DIM = 128
IDX_Q_RANK = 512
IDX_TOPK_MAX = 256
QBLOCK = 128

C_GROUPS = ((128, 1), (512, 4), (2048, 16))
C_HEADS = 16
C_WIDTH = C_HEADS * HEAD_DIM

IN_A = SHIFT_W + A_WIDTH
IN_B = B_WIDTH + 2 * B_KV_HEADS * HEAD_DIM + IDX_Q_RANK + IDX_DIM + IDX_HEADS + B_WIDTH

V7X_LANES = 128
V7X_VMEM_LIMIT_BYTES = 58 * 1024 * 1024
INT_MIN = -2 ** 31

_NT = (((1,), (1,)), ((), ()))


def _matmul_kernel(a_ref, b_ref, o_ref, bq_ref):
    @pl.when(pl.program_id(1) == 0)
    def _():
        bq_ref[...] = b_ref[...].astype(BF16)

    o_ref[...] = jnp.dot(a_ref[...].astype(BF16), bq_ref[...], preferred_element_type=F32)


def _pick_tile(n, cands):
    for c in cands:
        if n % c == 0:
            return c
    return n


def matmul(a, b):
    m, k = a.shape
    _, n = b.shape
    n_pad = -(-n // V7X_LANES) * V7X_LANES
    if n_pad != n:
        b = jnp.pad(b, ((0, 0), (0, n_pad - n)))
    tn = _pick_tile(n_pad, (1024, 512, 256, 128))
    tm = _pick_tile(m, (512, 256, 128, 64, 32, 16, 8))
    out = pl.pallas_call(
        _matmul_kernel,
        grid=(n_pad // tn, m // tm),
        in_specs=[pl.BlockSpec((tm, k), lambda j, i: (i, 0)),
                  pl.BlockSpec((k, tn), lambda j, i: (0, j))],
        out_specs=pl.BlockSpec((tm, tn), lambda j, i: (i, j)),
        out_shape=jax.ShapeDtypeStruct((m, n_pad), F32),
        scratch_shapes=[pltpu.VMEM((k, tn), BF16)],
        compiler_params=pltpu.CompilerParams(
            dimension_semantics=("arbitrary", "arbitrary"),
            vmem_limit_bytes=V7X_VMEM_LIMIT_BYTES),
        name="matmul",
    )(a, b)
    return out[:, :n] if n_pad != n else out


def mm(x, w):
    lead = x.shape[:-1]
    return matmul(x.reshape(-1, x.shape[-1]).astype(BF16), w).reshape(lead + (w.shape[-1],))


def _split_bf16(x):
    hi = x.astype(BF16)
    lo = (x - hi.astype(F32)).astype(BF16)
    return jnp.concatenate([hi, lo], axis=1)


def _rwkv_scan_kernel(nkk_ref, w_ref, kka_ref, k_ref, v_ref, q_ref, vc_ref, vth_ref, vtl_ref, s0_ref,
                      yt_ref, sout_ref, s_scr, *, pairs, steps, use_mxu):
    tchunk = pl.program_id(2)

    @pl.when(tchunk == 0)
    def _():
        for p in range(pairs):
            s_scr[p] = jnp.concatenate([s0_ref[0, 2 * p], s0_ref[0, 2 * p + 1]], axis=1)

    lane = lax.broadcasted_iota(I32, (A_HEAD, V7X_LANES), 1)
    row = lax.broadcasted_iota(I32, (A_HEAD, V7X_LANES), 0)
    lo = lane < A_HEAD
    eye_lo = lane == row
    eye_hi = lane == row + A_HEAD
    eye = jnp.logical_or(eye_lo, eye_hi)
    lane_t = lax.broadcasted_iota(I32, (A_HEAD, steps), 1)
    yt_ref[...] = jnp.zeros(yt_ref.shape, F32)
    if use_mxu:
        kk_i = lax.broadcasted_iota(I32, (2 * V7X_LANES, 2 * V7X_LANES), 0)
        nn_i = lax.broadcasted_iota(I32, (2 * V7X_LANES, 2 * V7X_LANES), 1)
        seg_mat = jnp.where(((kk_i % V7X_LANES) >= A_HEAD) == (nn_i >= V7X_LANES), 1.0, 0.0).astype(BF16)
        oh_k = lax.broadcasted_iota(I32, (2 * V7X_LANES, V7X_LANES), 0) % V7X_LANES

    def seg_sum(x):
        s_lo = jnp.sum(jnp.where(lo, x, 0.0), axis=1, keepdims=True)
        s_hi = jnp.sum(jnp.where(lo, 0.0, x), axis=1, keepdims=True)
        return s_lo, s_hi

    def step(t, carry):
        if use_mxu:
            onehot = jnp.where(oh_k == t, 1.0, 0.0).astype(BF16)
        for p in range(pairs):
            def rowvec(ref):
                return jnp.broadcast_to(ref[p, pl.ds(t, 1), :], (A_HEAD, V7X_LANES))

            s = s_scr[p]
            sa_lo, sa_hi = seg_sum(s * rowvec(nkk_ref))
            py = s * rowvec(q_ref) + jnp.where(eye, rowvec(vc_ref), 0.0)
            r0 = p * V7X_LANES
            if use_mxu:
                yy = jnp.dot(_split_bf16(py), seg_mat, preferred_element_type=F32)
                y_lo, y_hi = yy[:, :V7X_LANES], yy[:, V7X_LANES:]
                vt = jnp.concatenate([vth_ref[r0:r0 + V7X_LANES, :], vtl_ref[r0:r0 + V7X_LANES, :]], axis=1)
                vv = jnp.dot(vt, onehot, preferred_element_type=F32)
                v_b = jnp.where(lo, vv[:A_HEAD], vv[A_HEAD:])
            else:
                y_lo, y_hi = seg_sum(py)
                vrow = rowvec(v_ref)
                v_lo = jnp.sum(jnp.where(eye_lo, vrow, 0.0), axis=1, keepdims=True)
                v_hi = jnp.sum(jnp.where(eye_hi, vrow, 0.0), axis=1, keepdims=True)
                v_b = jnp.where(lo, v_lo, v_hi)
            sa_b = jnp.where(lo, sa_lo, sa_hi)
            s_scr[p] = s * rowvec(w_ref) + sa_b * rowvec(kka_ref) + v_b * rowvec(k_ref)
            yt_ref[0, r0:r0 + A_HEAD, :] = jnp.where(lane_t == t, y_lo, yt_ref[0, r0:r0 + A_HEAD, :])
            yt_ref[0, r0 + A_HEAD:r0 + V7X_LANES, :] = jnp.where(
                lane_t == t, y_hi, yt_ref[0, r0 + A_HEAD:r0 + V7X_LANES, :])
        return carry

    lax.fori_loop(0, steps, step, 0, unroll=RWKV_UNROLL if steps % RWKV_UNROLL == 0 else 1)

    @pl.when(tchunk == pl.num_programs(2) - 1)
    def _():
        for p in range(pairs):
            s = s_scr[p]
            sout_ref[0, 2 * p] = s[:, :A_HEAD]
            sout_ref[0, 2 * p + 1] = s[:, A_HEAD:]


RWKV_PAIRS_PER_STEP = 8
RWKV_UNROLL = 4


def rwkv_scan(nkk, w, kka, k, v, q, vc, vt_hi, vt_lo, s0):
    b, npairs, t, _ = nkk.shape
    pairs = RWKV_PAIRS_PER_STEP
    c = npairs * V7X_LANES
    tc = min(t, V7X_LANES)
    assert t % tc == 0 and npairs % pairs == 0
    bw = pairs * V7X_LANES
    row_spec = pl.BlockSpec((None, pairs, tc, V7X_LANES), lambda bi, hi, ti: (bi, hi, ti, 0))
    vt_spec = pl.BlockSpec((None, bw, tc), lambda bi, hi, ti: (bi, hi, ti))
    st_spec = pl.BlockSpec((1, 2 * pairs, A_HEAD, A_HEAD), lambda bi, hi, ti: (bi, hi, 0, 0))
    return pl.pallas_call(
        functools.partial(_rwkv_scan_kernel, pairs=pairs, steps=tc, use_mxu=(tc == V7X_LANES)),
        grid=(b, npairs // pairs, t // tc),
        in_specs=[row_spec] * 7 + [vt_spec] * 2 + [st_spec],
        out_specs=[pl.BlockSpec((1, bw, tc), lambda bi, hi, ti: (bi, hi, ti)), st_spec],
        out_shape=[jax.ShapeDtypeStruct((b, c, t), F32), jax.ShapeDtypeStruct(s0.shape, F32)],
        scratch_shapes=[pltpu.VMEM((pairs, A_HEAD, V7X_LANES), F32)],
        compiler_params=pltpu.CompilerParams(dimension_semantics=("arbitrary", "arbitrary", "arbitrary")),
        name="rwkv_scan",
    )(nkk, w, kka, k, v, q, vc, vt_hi, vt_lo, s0)


def _select_topk_mask(score, allowed, topk):
    r, l = score.shape
    score = jnp.where(score == 0.0, 0.0, score)
    bits = pltpu.bitcast(score, I32)
    key = jnp.where(bits < 0, bits ^ jnp.int32(0x7FFFFFFF), bits)
    key = jnp.where(allowed, key, jnp.int32(INT_MIN))
    kf = jnp.float32(topk)

    def count(pred):
        return jnp.sum(jnp.where(pred, 1.0, 0.0), axis=1, keepdims=True)

    def bit_step(i, prefix):
        cand = prefix | lax.shift_left(jnp.int32(1), jnp.int32(31) - i)
        ok = count(key >= (cand ^ jnp.int32(INT_MIN))) >= kf
        return jnp.where(ok, cand, prefix)

    prefix = lax.fori_loop(0, 32, bit_step, jnp.zeros((r, 1), I32))
    thr = prefix ^ jnp.int32(INT_MIN)
    gt = key > thr
    eq = jnp.logical_and(key == thr, allowed)
    need = kf - count(gt)
    li = lax.broadcasted_iota(I32, (V7X_LANES, 2 * V7X_LANES), 0)
    lj = lax.broadcasted_iota(I32, (V7X_LANES, 2 * V7X_LANES), 1)
    tri_ones = jnp.where(jnp.logical_or(lj >= V7X_LANES, li < lj), 1.0, 0.0).astype(BF16)
    running = jnp.zeros((r, V7X_LANES), F32)
    sel = []
    for c in range(l // V7X_LANES):
        sl = slice(c * V7X_LANES, (c + 1) * V7X_LANES)
        eq_c = eq[:, sl]
        res = jnp.dot(jnp.where(eq_c, 1.0, 0.0).astype(BF16), tri_ones, preferred_element_type=F32)
        before = res[:, :V7X_LANES] + running
        running = running + res[:, V7X_LANES:]
        sel.append(jnp.logical_or(gt[:, sl], jnp.logical_and(eq_c, before < need)))
    return jnp.concatenate(sel, axis=1)


def _dsa_prompt_kernel(qi_ref, qil_ref, wi_ref, kidx_ref, kidxl_ref, q_ref, k_ref, v_ref, o_ref, *, topk):
    qb = pl.program_id(1)
    l = kidx_ref.shape[0]
    kidx = kidx_ref[...]
    kidx_lo = kidxl_ref[...]
    score = jnp.zeros((QBLOCK, l), F32)
    for h in range(IDX_HEADS):
        hs = slice(h * IDX_DIM, (h + 1) * IDX_DIM)
        s = (lax.dot_general(qi_ref[:, hs], kidx, _NT, preferred_element_type=F32)
             + lax.dot_general(qi_ref[:, hs], kidx_lo, _NT, preferred_element_type=F32)
             + lax.dot_general(qil_ref[:, hs], kidx, _NT, preferred_element_type=F32))
        s = jnp.maximum(s * IDX_DIM ** -0.5, 0.0)
        score = score + s * wi_ref[:, h:h + 1]
    qpos = qb * QBLOCK + lax.broadcasted_iota(I32, (QBLOCK, l), 0)
    kpos = lax.broadcasted_iota(I32, (QBLOCK, l), 1)
    sel = _select_topk_mask(score, kpos <= qpos, topk)
    bias = jnp.where(sel, 0.0, -jnp.inf)
    group = B_HEADS // B_KV_HEADS
    for g in range(B_KV_HEADS):
        kg = k_ref[:, g * HEAD_DIM:(g + 1) * HEAD_DIM]
        vg = v_ref[:, g * HEAD_DIM:(g + 1) * HEAD_DIM]
        for j in range(group):
            h = g * group + j
            s = lax.dot_general(q_ref[:, h * HEAD_DIM:(h + 1) * HEAD_DIM], kg, _NT, preferred_element_type=F32)
            s = s * HEAD_DIM ** -0.5 + bias
            m = jnp.max(s, axis=1, keepdims=True)
            p = jnp.exp(s - m)
            denom = jnp.sum(p, axis=1, keepdims=True)
            o = jnp.dot(p.astype(BF16), vg, preferred_element_type=F32)
            o_ref[:, h * HEAD_DIM:(h + 1) * HEAD_DIM] = o / denom


def dsa_prompt_attend(q, k, v, qi, wi, kidx):
    b, s_len = q.shape[:2]
    topk = min(IDX_TOPK_MAX, s_len // 4)

    def flat16(z):
        return z.reshape(b, s_len, -1).astype(BF16)

    def flat16_lo(z):
        z = z.reshape(b, s_len, -1)
        return (z - z.astype(BF16).astype(F32)).astype(BF16)

    def qspec(w):
        return pl.BlockSpec((None, QBLOCK, w), lambda bi, qb: (bi, qb, 0))

    def kspec(w):
        return pl.BlockSpec((None, s_len, w), lambda bi, qb: (bi, 0, 0))

    return pl.pallas_call(
        functools.partial(_dsa_prompt_kernel, topk=topk),
        grid=(b, s_len // QBLOCK),
        in_specs=[qspec(B_WIDTH), qspec(B_WIDTH), qspec(IDX_HEADS), kspec(IDX_DIM), kspec(IDX_DIM),
                  qspec(B_WIDTH), kspec(B_KV_HEADS * HEAD_DIM), kspec(B_KV_HEADS * HEAD_DIM)],
        out_specs=qspec(B_WIDTH),
        out_shape=jax.ShapeDtypeStruct((b, s_len, B_WIDTH), F32),
        compiler_params=pltpu.CompilerParams(dimension_semantics=("arbitrary", "arbitrary"),
                                             vmem_limit_bytes=V7X_VMEM_LIMIT_BYTES),
        name="dsa_prompt",
    )(flat16(qi), flat16_lo(qi), wi, flat16(kidx), flat16_lo(kidx), flat16(q), flat16(k), flat16(v))


DIL_WM = 128


def _dilated_prompt_kernel(q_ref, kp_ref, kc_ref, vp_ref, vc_ref, o_ref, lse_ref, *, dil, heads):
    n = pl.program_id(1)
    iq = lax.broadcasted_iota(I32, (DIL_WM, 2 * DIL_WM), 0)
    ik = lax.broadcasted_iota(I32, (DIL_WM, 2 * DIL_WM), 1)
    dist = iq + DIL_WM - ik
    ok = jnp.logical_and(dist >= 0, dist <= DIL_WM)
    ok = jnp.logical_and(ok, jnp.logical_or(ik >= DIL_WM, n > 0))
    bias = jnp.where(ok, 0.0, -jnp.inf)
    for r in range(dil):
        def rows(ref, hs):
            if dil == 1:
                return ref[:, hs]
            return ref[pl.ds(r, DIL_WM, stride=dil), :]

        for h in range(heads):
            hs = slice(h * HEAD_DIM, (h + 1) * HEAD_DIM)
            q = rows(q_ref, hs).astype(BF16)
            k = jnp.concatenate([rows(kp_ref, hs), rows(kc_ref, hs)], axis=0).astype(BF16)
            v = jnp.concatenate([rows(vp_ref, hs), rows(vc_ref, hs)], axis=0).astype(BF16)
            s = lax.dot_general(q, k, _NT, preferred_element_type=F32) * HEAD_DIM ** -0.5 + bias
            m = jnp.max(s, axis=1, keepdims=True)
            p = jnp.exp(s - m)
            denom = jnp.sum(p, axis=1, keepdims=True)
            o = jnp.dot(p.astype(BF16), v, preferred_element_type=F32) / denom
            lse = jnp.broadcast_to(m + jnp.log(denom), (DIL_WM, HEAD_DIM))
            if dil == 1:
                o_ref[:, hs] = o
                lse_ref[:, hs] = lse
            else:
                o_ref[pl.ds(r, DIL_WM, stride=dil), :] = o
                lse_ref[pl.ds(r, DIL_WM, stride=dil), :] = lse


def dilated_prompt_attend(q, k, v, window, dil):
    b, s_len, h, d = q.shape
    assert window // dil == DIL_WM and s_len % (DIL_WM * dil) == 0
    rows = DIL_WM * dil
    heads = h if dil == 1 else 1
    bw = heads * d
    cur = pl.BlockSpec((None, rows, bw), lambda bi, n, hi: (bi, n, hi))
    prev = pl.BlockSpec((None, rows, bw), lambda bi, n, hi: (bi, jnp.maximum(n - 1, 0), hi))
    q2, k2, v2 = (z.reshape(b, s_len, h * d) for z in (q, k, v))
    o, lse = pl.pallas_call(
        functools.partial(_dilated_prompt_kernel, dil=dil, heads=heads),
        grid=(b, s_len // rows, (h * d) // bw),
        in_specs=[cur, prev, cur, prev, cur],
        out_specs=[cur, cur],
        out_shape=[jax.ShapeDtypeStruct(q2.shape, F32)] * 2,
        compiler_params=pltpu.CompilerParams(dimension_semantics=("arbitrary",) * 3),
        name="dilated_prompt",
    )(q2, k2, k2, v2, v2)
    return o.reshape(q.shape), lse.reshape(q.shape)


def _offsets(sizes):
    out, acc = [], 0
    for s in sizes[:-1]:
        acc += s
        out.append(acc)
    return out


def _rms(x, eps=NORM_EPS):
    xf = x.astype(F32)
    return xf * lax.rsqrt(jnp.mean(xf * xf, axis=-1, keepdims=True) + eps)


def rms_norm(x, g):
    return (_rms(x) * g.astype(F32)).astype(x.dtype)


def layer_norm(x, g, b, eps=NORM_EPS):
    xf = x.astype(F32)
    mu = jnp.mean(xf, axis=-1, keepdims=True)
    xc = xf - mu
    y = xc * lax.rsqrt(jnp.mean(xc * xc, axis=-1, keepdims=True) + eps)
    return (y * g.astype(F32) + b.astype(F32)).astype(x.dtype)


def partial_rope(x, pos):
    half = ROT_DIM // 2
    freqs = ROPE_THETA ** (-jnp.arange(half, dtype=F32) / half)
    ang = pos.astype(F32)[:, None] * freqs[None, :]
    ang = ang.reshape((1, ang.shape[0]) + (1,) * (x.ndim - 3) + (half,))
    cos, sin = jnp.cos(ang), jnp.sin(ang)
    xf = x.astype(F32)
    x1, x2 = xf[..., :half], xf[..., half:ROT_DIM]
    out = jnp.concatenate([x1 * cos - x2 * sin, x2 * cos + x1 * sin, xf[..., ROT_DIM:]], axis=-1)
    return out.astype(x.dtype)


def rwkv7_mix(sh, prev_row, s0, mu, w0, w_lora, a0, a_lora, k_k, k_a, r_k, gn_g, gn_b):
    bn, t, _ = sh.shape
    prev = jnp.concatenate([prev_row[:, None, :].astype(sh.dtype), sh[:, :-1]], axis=1)
    xm = sh + (prev - sh) * mu
    r, k, v, xw, xa = jnp.split(xm, [A_WIDTH, 2 * A_WIDTH, 3 * A_WIDTH, 3 * A_WIDTH + DECAY_LORA], axis=-1)
    wlog = -jax.nn.softplus(-(w0 + mm(jnp.tanh(xw), w_lora))) - 0.5
    decay = jnp.exp(-jnp.exp(wlog.astype(F32)))
    a = jax.nn.sigmoid((a0 + mm(xa, a_lora)).astype(F32))

    def heads(z):
        return z.astype(F32).reshape(bn, t, A_HEADS, A_HEAD)

    def head_sum(z):
        return jnp.broadcast_to(jnp.sum(heads(z), axis=-1, keepdims=True),
                                (bn, t, A_HEADS, A_HEAD)).reshape(bn, t, A_WIDTH)

    kk = heads(k * k_k)
    kk = kk / jnp.maximum(jnp.sqrt(jnp.sum(kk * kk, axis=-1, keepdims=True)), 1e-12)
    kk = kk.reshape(bn, t, A_WIDTH)
    k2 = k * (1.0 + (a - 1.0) * k_a.astype(F32))
    kka = kk * a
    q = decay * r - kk * head_sum(kka * r)
    vc = v * head_sum(k2 * r)

    def pair_major(z):
        return jnp.swapaxes(z.reshape(bn, t, A_WIDTH // V7X_LANES, V7X_LANES), 1, 2)

    vt = jnp.swapaxes(v, 1, 2)
    vt_hi = vt.astype(BF16)
    vt_lo = (vt - vt_hi.astype(F32)).astype(BF16)
    yt, s_fin = rwkv_scan(*[pair_major(z) for z in (-kk, decay, kka, k2, v, q, vc)], vt_hi, vt_lo,
                          s0.astype(F32))
    y = heads(jnp.swapaxes(yt, 1, 2))
    ym = jnp.mean(y, axis=-1, keepdims=True)
    yc = y - ym
    yn = yc * lax.rsqrt(jnp.mean(yc * yc, axis=-1, keepdims=True) + GN_EPS)
    yn = yn.reshape(bn, t, A_WIDTH) * gn_g.astype(F32) + gn_b.astype(F32)
    bonus = (jnp.sum(heads(r * k2) * r_k.astype(F32), axis=-1, keepdims=True) * heads(v)).reshape(bn, t, A_WIDTH)
    return (yn + bonus).astype(sh.dtype), s_fin.astype(s0.dtype), sh[:, -1]


def dsa_project(u, pos, q_norm, k_norm, qi_norm, w_qi, kidx_g, kidx_b):
    bn, t, _ = u.shape
    sizes = (B_WIDTH, B_KV_HEADS * HEAD_DIM, B_KV_HEADS * HEAD_DIM, IDX_Q_RANK, IDX_DIM, IDX_HEADS)
    q, k, v, cqi, kidx, wi = jnp.split(u, _offsets(sizes), axis=-1)
    q = partial_rope(rms_norm(q.reshape(bn, t, B_HEADS, HEAD_DIM), q_norm), pos)
    k = partial_rope(rms_norm(k.reshape(bn, t, B_KV_HEADS, HEAD_DIM), k_norm), pos)
    v = v.reshape(bn, t, B_KV_HEADS, HEAD_DIM)
    qi = mm(rms_norm(cqi, qi_norm), w_qi).reshape(bn, t, IDX_HEADS, IDX_DIM)
    qi = partial_rope(qi, pos)
    kidx = partial_rope(layer_norm(kidx, kidx_g, kidx_b), pos)
    wi = wi * IDX_HEADS ** -0.5
    return q, k, v, qi, kidx, wi


def index_topk(qi, wi, kidx, qpos, topk):
    s = jnp.einsum('bqhd,bkd->bqhk', qi.astype(F32), kidx.astype(F32)) * IDX_DIM ** -0.5
    score = jnp.einsum('bqhk,bqh->bqk', jax.nn.relu(s), wi.astype(F32))
    allowed = jnp.arange(kidx.shape[1])[None, :] <= qpos[:, None]
    score = jnp.where(allowed[None], score, -jnp.inf)
    _, idx = lax.top_k(score, topk)
    return idx, idx <= qpos[None, :, None]


def sparse_attn(q, kg, vg, valid):
    bn, tq, h, d = q.shape
    qg = q.reshape(bn, tq, B_KV_HEADS, h // B_KV_HEADS, d)
    s = jnp.einsum('bqhgd,bqkhd->bqhgk', qg, kg).astype(F32) * d ** -0.5
    s = jnp.where(valid[:, :, None, None, :], s, -jnp.inf)
    p = jax.nn.softmax(s, axis=-1)
    o = jnp.einsum('bqhgk,bqkhd->bqhgd', p.astype(vg.dtype), vg)
    return o.reshape(bn, tq, h * d)


def dsa_sample(q, k, v, qi, wi, kidx, cache_k, cache_v, cache_kidx, page_table, layer):
    bn, t = q.shape[:2]
    past = page_table.shape[1] * PAGE_SIZE
    topk = min(IDX_TOPK_MAX, (past + t) // 4)
    kidx_past = cache_kidx[layer, page_table].reshape(bn, past, IDX_DIM)
    kidx_all = jnp.concatenate([kidx_past, kidx.astype(kidx_past.dtype)], axis=1)
    qpos = past + jnp.arange(t)
    idx, valid = index_topk(qi, wi, kidx_all, qpos, topk)
    bidx = jnp.arange(bn)[:, None, None]
    pidx = jnp.minimum(idx, past - 1)
    phys = page_table[bidx, pidx // PAGE_SIZE]
    off = pidx % PAGE_SIZE
    nidx = jnp.clip(idx - past, 0, t - 1)
    is_new = (idx >= past)[..., None, None]
    kg = jnp.where(is_new, k[bidx, nidx], cache_k[layer, phys, off].astype(k.dtype))
    vg = jnp.where(is_new, v[bidx, nidx], cache_v[layer, phys, off].astype(v.dtype))
    return sparse_attn(q, kg, vg, valid)


def even_mixer(xn, pos, prev_row, s0, attend, ep):
    u = mm(xn, ep['w_in'])
    u_a, g_a, u_b, g_b = jnp.split(u, [SHIFT_W, IN_A, IN_A + IN_B - B_WIDTH], axis=-1)
    y_a, s_fin, last_row = rwkv7_mix(u_a, prev_row, s0, ep['shift_mu'], ep['w0'], ep['w_lora'],
                                     ep['a0'], ep['a_lora'], ep['k_k'], ep['k_a'], ep['r_k'],
                                     ep['gn_gain'], ep['gn_bias'])
    q, k, v, qi, kidx, wi = dsa_project(u_b, pos, ep['q_norm'], ep['k_norm'], ep['qi_norm'],
                                        ep['w_qi'], ep['kidx_gain'], ep['kidx_bias'])
    y_b = attend(q, k, v, qi, wi, kidx)
    y = jnp.concatenate([y_a * jax.nn.silu(g_a), y_b * jax.nn.silu(g_b)], axis=-1)
    return mm(y, ep['w_out']), (s_fin, last_row, k, v, kidx)


def dilated_sample(q, k, v, buf_k, buf_v, window, dil):
    t = q.shape[1]
    d = q.shape[-1]
    wb = buf_k.shape[1]
    wm = window // dil
    kc = jnp.concatenate([buf_k.astype(k.dtype), k], axis=1)
    vc = jnp.concatenate([buf_v.astype(v.dtype), v], axis=1)
    idx = wb + jnp.arange(t)[:, None] - jnp.arange(wm + 1)[None, :] * dil
    valid = idx >= 0
    idxc = jnp.maximum(idx, 0)
    kg, vg = kc[:, idxc], vc[:, idxc]
    s = jnp.einsum('bthd,btjhd->bthj', q, kg).astype(F32) * d ** -0.5
    s = jnp.where(valid[None, :, None, :], s, -jnp.inf)
    lse = jax.nn.logsumexp(s, axis=-1)
    p = jnp.exp(s - lse[..., None])
    o = jnp.einsum('bthj,btjhd->bthd', p.astype(vg.dtype), vg)
    return o, lse, kc[:, -wb:], vc[:, -wb:]


def window_attend_prompt(g, q, k, v, win, dil):
    o, lse = dilated_prompt_attend(q, k, v, win, dil)
    keep = min(win, q.shape[1])
    return o, lse, k[:, -keep:], v[:, -keep:]


def window_attend_sample(g, q, k, v, win, dil, bufs_k, bufs_v, layer):
    return dilated_sample(q, k, v, bufs_k[g][layer], bufs_v[g][layer], win, dil)


def odd_mixer(xn, pos, attend, op):
    u = mm(xn, op['w_in'])
    bn, t, _ = u.shape
    n_g = len(C_GROUPS)
    qkv, gate = jnp.split(u, [3 * n_g * C_WIDTH], axis=-1)
    qkv = qkv.reshape(bn, t, n_g, 3, C_HEADS, HEAD_DIM)
    outs, lses, bufs = [], [], []
    for g, (win, dil) in enumerate(C_GROUPS):
        q = partial_rope(rms_norm(qkv[:, :, g, 0], op['q_norm'][g]), pos)
        k = partial_rope(rms_norm(qkv[:, :, g, 1], op['k_norm'][g]), pos)
        v = qkv[:, :, g, 2]
        o, lse, kb, vb = attend(g, q, k, v, win, dil)
        outs.append(o.astype(F32))
        lses.append(lse if lse.ndim == o.ndim else lse[..., None])
        bufs += [kb, vb]
    alpha = jax.nn.softmax(jnp.stack(lses, axis=0), axis=0)
    o = jnp.sum(alpha * jnp.stack(outs, axis=0), axis=0)
    y = o.reshape(bn, t, C_WIDTH).astype(xn.dtype) * jax.nn.silu(gate)
    return mm(y, op['w_out']), tuple(bufs)


def ple_add(h, p_l, w_proj, w_gate):
    gate = jax.nn.sigmoid(mm(_rms(h).astype(h.dtype), w_gate))
    return h + gate * mm(p_l, w_proj)


def kernel(x_prompt, x_sample, p_prompt, p_sample, state_wkv, state_shift, cache_k, cache_v,
           cache_kidx, page_table, cache_win_k0, cache_win_v0, cache_win_k1, cache_win_v1,
           cache_win_k2, cache_win_v2, ln_gain, e_w_in, e_shift_mu, e_w0, e_w_lora, e_a0,
           e_a_lora, e_k_k, e_k_a, e_r_k, e_gn_gain, e_gn_bias, e_q_norm, e_k_norm, e_qi_norm,
           e_w_qi, e_kidx_gain, e_kidx_bias, e_w_out, o_w_in, o_q_norm, o_k_norm, o_w_out,
           ple_w_proj, ple_w_gate):
    depth = ln_gain.shape[0]
    bp, s_len, _ = x_prompt.shape
    t_len = x_sample.shape[1]
    past = page_table.shape[1] * PAGE_SIZE
    pos_p = jnp.arange(s_len, dtype=jnp.int32)
    pos_s = past + jnp.arange(t_len, dtype=jnp.int32)
    bufs_k = (cache_win_k0, cache_win_k1, cache_win_k2)
    bufs_v = (cache_win_v0, cache_win_v1, cache_win_v2)
    hp, hs = x_prompt, x_sample
    ev_p, ev_s, od_p, od_s = [], [], [], []
    for i in range(depth):
        l = i // 2
        xp = rms_norm(hp, ln_gain[i])
        xs = rms_norm(hs, ln_gain[i])
        if i % 2 == 0:
            ep = {'w_in': e_w_in[l], 'shift_mu': e_shift_mu[l], 'w0': e_w0[l], 'w_lora': e_w_lora[l],
                  'a0': e_a0[l], 'a_lora': e_a_lora[l], 'k_k': e_k_k[l], 'k_a': e_k_a[l],
                  'r_k': e_r_k[l], 'gn_gain': e_gn_gain[l], 'gn_bias': e_gn_bias[l],
                  'q_norm': e_q_norm[l], 'k_norm': e_k_norm[l], 'qi_norm': e_qi_norm[l],
                  'w_qi': e_w_qi[l], 'kidx_gain': e_kidx_gain[l], 'kidx_bias': e_kidx_bias[l],
                  'w_out': e_w_out[l]}
            row0 = jnp.zeros((bp, SHIFT_W), hp.dtype)
            st0 = jnp.zeros((bp, A_HEADS, A_HEAD, A_HEAD), hp.dtype)
            mp, stp = even_mixer(xp, pos_p, row0, st0, dsa_prompt_attend, ep)
            att_s = functools.partial(dsa_sample, cache_k=cache_k, cache_v=cache_v,
                                      cache_kidx=cache_kidx, page_table=page_table, layer=l)
            ms, sts = even_mixer(xs, pos_s, state_shift[l], state_wkv[l], att_s, ep)
            ev_p.append(stp)
            ev_s.append(sts)
        else:
            op = {'w_in': o_w_in[l], 'q_norm': o_q_norm[l], 'k_norm': o_k_norm[l], 'w_out': o_w_out[l]}
            mp, stp = odd_mixer(xp, pos_p, window_attend_prompt, op)
            att_s = functools.partial(window_attend_sample, bufs_k=bufs_k, bufs_v=bufs_v, layer=l)
            ms, sts = odd_mixer(xs, pos_s, att_s, op)
            od_p.append(stp)
            od_s.append(sts)
        hp = ple_add(hp + mp, p_prompt[i], ple_w_proj[i], ple_w_gate[i])
        hs = ple_add(hs + ms, p_sample[i], ple_w_proj[i], ple_w_gate[i])

    def st(lst, j):
        return jnp.stack([e[j] for e in lst], axis=0)

    outs = [hp, hs, st(ev_p, 0), st(ev_s, 0), st(ev_p, 1), st(ev_s, 1)]
    outs += [st(ev_p, j) for j in (2, 3, 4)] + [st(ev_s, j) for j in (2, 3, 4)]
    outs += [st(od_p, j) for j in range(6)] + [st(od_s, j) for j in range(6)]
    return tuple(outs)
```

```python
import functools

import jax
import jax.numpy as jnp
from jax import lax
from jax.experimental import pallas as pl
from jax.experimental.pallas import tpu as pltpu

F32 = jnp.float32
BF16 = jnp.bfloat16
I32 = jnp.int32

D_MODEL = 4096
PAGE_SIZE = 128
HEAD_DIM = 128
ROT_DIM = HEAD_DIM // 4
ROPE_THETA = 500000.0
NORM_EPS = 1e-6

A_WIDTH = D_MODEL // 2
A_HEAD = 64
A_HEADS = A_WIDTH // A_HEAD
DECAY_LORA = 96
AAA_LORA = 96
GN_EPS = 64e-5
SHIFT_W = 3 * A_WIDTH + DECAY_LORA + AAA_LORA

B_WIDTH = D_MODEL // 2
B_HEADS = B_WIDTH // HEAD_DIM
B_KV_HEADS = 4
IDX_HEADS = 16
IDX_DIM = 128
IDX_Q_RANK = 512
IDX_TOPK_MAX = 256
QBLOCK = 128

C_GROUPS = ((128, 1), (512, 4), (2048, 16))
C_HEADS = 16
C_WIDTH = C_HEADS * HEAD_DIM

IN_A = SHIFT_W + A_WIDTH
IN_B = B_WIDTH + 2 * B_KV_HEADS * HEAD_DIM + IDX_Q_RANK + IDX_DIM + IDX_HEADS + B_WIDTH

V7X_LANES = 128
V7X_VMEM_LIMIT_BYTES = 58 * 1024 * 1024
INT_MIN = -2 ** 31

_NT = (((1,), (1,)), ((), ()))


def _matmul_kernel(a_ref, b_ref, o_ref, bq_ref):
    @pl.when(pl.program_id(1) == 0)
    def _():
        bq_ref[...] = b_ref[...].astype(BF16)

    o_ref[...] = jnp.dot(a_ref[...].astype(BF16), bq_ref[...], preferred_element_type=F32)


def _pick_tile(n, cands):
    for c in cands:
        if n % c == 0:
            return c
    return n


def matmul(a, b):
    m, k = a.shape
    _, n = b.shape
    n_pad = -(-n // V7X_LANES) * V7X_LANES
    if n_pad != n:
        b = jnp.pad(b, ((0, 0), (0, n_pad - n)))
    tn = _pick_tile(n_pad, (1024, 512, 256, 128))
    tm = _pick_tile(m, (512, 256, 128, 64, 32, 16, 8))
    out = pl.pallas_call(
        _matmul_kernel,
        grid=(n_pad // tn, m // tm),
        in_specs=[pl.BlockSpec((tm, k), lambda j, i: (i, 0)),
                  pl.BlockSpec((k, tn), lambda j, i: (0, j))],
        out_specs=pl.BlockSpec((tm, tn), lambda j, i: (i, j)),
        out_shape=jax.ShapeDtypeStruct((m, n_pad), F32),
        scratch_shapes=[pltpu.VMEM((k, tn), BF16)],
        compiler_params=pltpu.CompilerParams(
            dimension_semantics=("arbitrary", "arbitrary"),
            vmem_limit_bytes=V7X_VMEM_LIMIT_BYTES),
        name="matmul",
    )(a, b)
    return out[:, :n] if n_pad != n else out


def mm(x, w):
    lead = x.shape[:-1]
    return matmul(x.reshape(-1, x.shape[-1]).astype(BF16), w).reshape(lead + (w.shape[-1],))


def _split_bf16(x):
    hi = x.astype(BF16)
    lo = (x - hi.astype(F32)).astype(BF16)
    return jnp.concatenate([hi, lo], axis=1)


def _rwkv_scan_kernel(nkk_ref, w_ref, kka_ref, k_ref, v_ref, q_ref, vc_ref, vth_ref, vtl_ref, s0_ref,
                      yt_ref, sout_ref, s_scr, *, pairs, steps, use_mxu):
    tchunk = pl.program_id(2)

    @pl.when(tchunk == 0)
    def _():
        for p in range(pairs):
            s_scr[p] = jnp.concatenate([s0_ref[0, 2 * p], s0_ref[0, 2 * p + 1]], axis=1)

    lane = lax.broadcasted_iota(I32, (A_HEAD, V7X_LANES), 1)
    row = lax.broadcasted_iota(I32, (A_HEAD, V7X_LANES), 0)
    lo = lane < A_HEAD
    eye_lo = lane == row
    eye_hi = lane == row + A_HEAD
    eye = jnp.logical_or(eye_lo, eye_hi)
    lane_t = lax.broadcasted_iota(I32, (A_HEAD, steps), 1)
    yt_ref[...] = jnp.zeros(yt_ref.shape, F32)
    if use_mxu:
        kk_i = lax.broadcasted_iota(I32, (2 * V7X_LANES, 2 * V7X_LANES), 0)
        nn_i = lax.broadcasted_iota(I32, (2 * V7X_LANES, 2 * V7X_LANES), 1)
        seg_mat = jnp.where(((kk_i % V7X_LANES) >= A_HEAD) == (nn_i >= V7X_LANES), 1.0, 0.0).astype(BF16)
        oh_k = lax.broadcasted_iota(I32, (2 * V7X_LANES, V7X_LANES), 0) % V7X_LANES

    def seg_sum(x):
        s_lo = jnp.sum(jnp.where(lo, x, 0.0), axis=1, keepdims=True)
        s_hi = jnp.sum(jnp.where(lo, 0.0, x), axis=1, keepdims=True)
        return s_lo, s_hi

    def step(t, carry):
        if use_mxu:
            onehot = jnp.where(oh_k == t, 1.0, 0.0).astype(BF16)
        for p in range(pairs):
            def rowvec(ref):
                return jnp.broadcast_to(ref[p, pl.ds(t, 1), :], (A_HEAD, V7X_LANES))

            s = s_scr[p]
            sa_lo, sa_hi = seg_sum(s * rowvec(nkk_ref))
            py = s * rowvec(q_ref) + jnp.where(eye, rowvec(vc_ref), 0.0)
            r0 = p * V7X_LANES
            if use_mxu:
                yy = jnp.dot(_split_bf16(py), seg_mat, preferred_element_type=F32)
                y_lo, y_hi = yy[:, :V7X_LANES], yy[:, V7X_LANES:]
                vt = jnp.concatenate([vth_ref[r0:r0 + V7X_LANES, :], vtl_ref[r0:r0 + V7X_LANES, :]], axis=1)
                vv = jnp.dot(vt, onehot, preferred_element_type=F32)
                v_b = jnp.where(lo, vv[:A_HEAD], vv[A_HEAD:])
            else:
                y_lo, y_hi = seg_sum(py)
                vrow = rowvec(v_ref)
                v_lo = jnp.sum(jnp.where(eye_lo, vrow, 0.0), axis=1, keepdims=True)
                v_hi = jnp.sum(jnp.where(eye_hi, vrow, 0.0), axis=1, keepdims=True)
                v_b = jnp.where(lo, v_lo, v_hi)
            sa_b = jnp.where(lo, sa_lo, sa_hi)
            s_scr[p] = s * rowvec(w_ref) + sa_b * rowvec(kka_ref) + v_b * rowvec(k_ref)
            yt_ref[0, r0:r0 + A_HEAD, :] = jnp.where(lane_t == t, y_lo, yt_ref[0, r0:r0 + A_HEAD, :])
            yt_ref[0, r0 + A_HEAD:r0 + V7X_LANES, :] = jnp.where(
                lane_t == t, y_hi, yt_ref[0, r0 + A_HEAD:r0 + V7X_LANES, :])
        return carry

    lax.fori_loop(0, steps, step, 0, unroll=RWKV_UNROLL if steps % RWKV_UNROLL == 0 else 1)

    @pl.when(tchunk == pl.num_programs(2) - 1)
    def _():
        for p in range(pairs):
            s = s_scr[p]
            sout_ref[0, 2 * p] = s[:, :A_HEAD]
            sout_ref[0, 2 * p + 1] = s[:, A_HEAD:]


RWKV_PAIRS_PER_STEP = 8
RWKV_UNROLL = 4


def rwkv_scan(nkk, w, kka, k, v, q, vc, vt_hi, vt_lo, s0):
    b, npairs, t, _ = nkk.shape
    pairs = RWKV_PAIRS_PER_STEP
    c = npairs * V7X_LANES
    tc = min(t, V7X_LANES)
    assert t % tc == 0 and npairs % pairs == 0
    bw = pairs * V7X_LANES
    row_spec = pl.BlockSpec((None, pairs, tc, V7X_LANES), lambda bi, hi, ti: (bi, hi, ti, 0))
    vt_spec = pl.BlockSpec((None, bw, tc), lambda bi, hi, ti: (bi, hi, ti))
    st_spec = pl.BlockSpec((1, 2 * pairs, A_HEAD, A_HEAD), lambda bi, hi, ti: (bi, hi, 0, 0))
    return pl.pallas_call(
        functools.partial(_rwkv_scan_kernel, pairs=pairs, steps=tc, use_mxu=(tc == V7X_LANES)),
        grid=(b, npairs // pairs, t // tc),
        in_specs=[row_spec] * 7 + [vt_spec] * 2 + [st_spec],
        out_specs=[pl.BlockSpec((1, bw, tc), lambda bi, hi, ti: (bi, hi, ti)), st_spec],
        out_shape=[jax.ShapeDtypeStruct((b, c, t), F32), jax.ShapeDtypeStruct(s0.shape, F32)],
        scratch_shapes=[pltpu.VMEM((pairs, A_HEAD, V7X_LANES), F32)],
        compiler_params=pltpu.CompilerParams(dimension_semantics=("arbitrary", "arbitrary", "arbitrary")),
        name="rwkv_scan",
    )(nkk, w, kka, k, v, q, vc, vt_hi, vt_lo, s0)


def _select_topk_mask(score, allowed, topk):
    r, l = score.shape
    score = jnp.where(score == 0.0, 0.0, score)
    bits = pltpu.bitcast(score, I32)
    key = jnp.where(bits < 0, bits ^ jnp.int32(0x7FFFFFFF), bits)
    key = jnp.where(allowed, key, jnp.int32(INT_MIN))
    kf = jnp.float32(topk)

    def count(pred):
        return jnp.sum(jnp.where(pred, 1.0, 0.0), axis=1, keepdims=True)

    def bit_step(i, prefix):
        cand = prefix | lax.shift_left(jnp.int32(1), jnp.int32(31) - i)
        ok = count(key >= (cand ^ jnp.int32(INT_MIN))) >= kf
        return jnp.where(ok, cand, prefix)

    prefix = lax.fori_loop(0, 32, bit_step, jnp.zeros((r, 1), I32))
    thr = prefix ^ jnp.int32(INT_MIN)
    gt = key > thr
    eq = jnp.logical_and(key == thr, allowed)
    need = kf - count(gt)
    li = lax.broadcasted_iota(I32, (V7X_LANES, 2 * V7X_LANES), 0)
    lj = lax.broadcasted_iota(I32, (V7X_LANES, 2 * V7X_LANES), 1)
    tri_ones = jnp.where(jnp.logical_or(lj >= V7X_LANES, li < lj), 1.0, 0.0).astype(BF16)
    running = jnp.zeros((r, V7X_LANES), F32)
    sel = []
    for c in range(l // V7X_LANES):
        sl = slice(c * V7X_LANES, (c + 1) * V7X_LANES)
        eq_c = eq[:, sl]
        res = jnp.dot(jnp.where(eq_c, 1.0, 0.0).astype(BF16), tri_ones, preferred_element_type=F32)
        before = res[:, :V7X_LANES] + running
        running = running + res[:, V7X_LANES:]
        sel.append(jnp.logical_or(gt[:, sl], jnp.logical_and(eq_c, before < need)))
    return jnp.concatenate(sel, axis=1)


def _dsa_prompt_kernel(qi_ref, qil_ref, wi_ref, kidx_ref, kidxl_ref, q_ref, k_ref, v_ref, o_ref, *, topk):
    qb = pl.program_id(1)
    l = kidx_ref.shape[0]
    kidx = kidx_ref[...]
    kidx_lo = kidxl_ref[...]
    score = jnp.zeros((QBLOCK, l), F32)
    for h in range(IDX_HEADS):
        hs = slice(h * IDX_DIM, (h + 1) * IDX_DIM)
        s = (lax.dot_general(qi_ref[:, hs], kidx, _NT, preferred_element_type=F32)
             + lax.dot_general(qi_ref[:, hs], kidx_lo, _NT, preferred_element_type=F32)
             + lax.dot_general(qil_ref[:, hs], kidx, _NT, preferred_element_type=F32))
        s = jnp.maximum(s * IDX_DIM ** -0.5, 0.0)
        score = score + s * wi_ref[:, h:h + 1]
    qpos = qb * QBLOCK + lax.broadcasted_iota(I32, (QBLOCK, l), 0)
    kpos = lax.broadcasted_iota(I32, (QBLOCK, l), 1)
    sel = _select_topk_mask(score, kpos <= qpos, topk)
    bias = jnp.where(sel, 0.0, -jnp.inf)
    group = B_HEADS // B_KV_HEADS
    for g in range(B_KV_HEADS):
        kg = k_ref[:, g * HEAD_DIM:(g + 1) * HEAD_DIM]
        vg = v_ref[:, g * HEAD_DIM:(g + 1) * HEAD_DIM]
        for j in range(group):
            h = g * group + j
            s = lax.dot_general(q_ref[:, h * HEAD_DIM:(h + 1) * HEAD_DIM], kg, _NT, preferred_element_type=F32)
            s = s * HEAD_DIM ** -0.5 + bias
            m = jnp.max(s, axis=1, keepdims=True)
            p = jnp.exp(s - m)
            denom = jnp.sum(p, axis=1, keepdims=True)
            o = jnp.dot(p.astype(BF16), vg, preferred_element_type=F32)
            o_ref[:, h * HEAD_DIM:(h + 1) * HEAD_DIM] = o / denom


def dsa_prompt_attend(q, k, v, qi, wi, kidx):
    b, s_len = q.shape[:2]
    topk = min(IDX_TOPK_MAX, s_len // 4)

    def flat16(z):
        return z.reshape(b, s_len, -1).astype(BF16)

    def flat16_lo(z):
        z = z.reshape(b, s_len, -1)
        return (z - z.astype(BF16).astype(F32)).astype(BF16)

    def qspec(w):
        return pl.BlockSpec((None, QBLOCK, w), lambda bi, qb: (bi, qb, 0))

    def kspec(w):
        return pl.BlockSpec((None, s_len, w), lambda bi, qb: (bi, 0, 0))

    return pl.pallas_call(
        functools.partial(_dsa_prompt_kernel, topk=topk),
        grid=(b, s_len // QBLOCK),
        in_specs=[qspec(B_WIDTH), qspec(B_WIDTH), qspec(IDX_HEADS), kspec(IDX_DIM), kspec(IDX_DIM),
                  qspec(B_WIDTH), kspec(B_KV_HEADS * HEAD_DIM), kspec(B_KV_HEADS * HEAD_DIM)],
        out_specs=qspec(B_WIDTH),
        out_shape=jax.ShapeDtypeStruct((b, s_len, B_WIDTH), F32),
        compiler_params=pltpu.CompilerParams(dimension_semantics=("arbitrary", "arbitrary"),
                                             vmem_limit_bytes=V7X_VMEM_LIMIT_BYTES),
        name="dsa_prompt",
    )(flat16(qi), flat16_lo(qi), wi, flat16(kidx), flat16_lo(kidx), flat16(q), flat16(k), flat16(v))


SAMPLE_PAGES_PER_STEP = 8
SAMPLE_T_PAD = 8


def _hi_lo(x):
    hi = x.astype(BF16)
    return hi, (x - hi.astype(F32)).astype(BF16)


def _dsa_sample_score_kernel(pt_ref, qi_ref, qil_ref, wi_ref, *refs, n_steps):
    j = pl.program_id(1)
    pages, new_ref, o_ref = refs[:-2], refs[-2], refs[-1]
    qh, ql, wcol = qi_ref[...], qil_ref[...], wi_ref[...]
    for i, pref in enumerate(pages):
        kidx = pref[...]
        if i == 0:
            kidx = jnp.where(j == n_steps - 1, new_ref[...], kidx)
        kh, kl = _hi_lo(kidx)
        s = (lax.dot_general(qh, kh, _NT, preferred_element_type=F32)
             + lax.dot_general(qh, kl, _NT, preferred_element_type=F32)
             + lax.dot_general(ql, kh, _NT, preferred_element_type=F32))
        s = jnp.maximum(s * IDX_DIM ** -0.5, 0.0) * wcol
        tot = s[0:SAMPLE_T_PAD]
        for h in range(1, IDX_HEADS):
            tot = tot + s[h * SAMPLE_T_PAD:(h + 1) * SAMPLE_T_PAD]
        o_ref[:, i * PAGE_SIZE:(i + 1) * PAGE_SIZE] = tot


def _dsa_sample_attn_kernel(pt_ref, score_ref, q_ref, *refs, n_steps, topk, past, t_len):
    pps = SAMPLE_PAGES_PER_STEP
    k_pages, v_pages = refs[:pps], refs[pps:2 * pps]
    kn_ref, vn_ref, o_ref, bias_scr, m_scr, l_scr, acc_scr = refs[2 * pps:]
    j = pl.program_id(1)
    width = score_ref.shape[1]
    group = B_HEADS // B_KV_HEADS

    @pl.when(j == 0)
    def _():
        qpos = past + lax.broadcasted_iota(I32, (SAMPLE_T_PAD, width), 0)
        kpos = lax.broadcasted_iota(I32, (SAMPLE_T_PAD, width), 1)
        sel = _select_topk_mask(score_ref[...], kpos <= qpos, topk)
        bias_scr[...] = jnp.where(sel, 0.0, -jnp.inf)
        m_scr[...] = jnp.full(m_scr.shape, -jnp.inf, F32)
        l_scr[...] = jnp.zeros(l_scr.shape, F32)
        acc_scr[...] = jnp.zeros(acc_scr.shape, F32)

    last = j == n_steps - 1
    col = pl.multiple_of(j * (pps * PAGE_SIZE), pps * PAGE_SIZE)
    b4 = bias_scr[0:t_len, pl.ds(col, pps * PAGE_SIZE)]
    bias = jnp.concatenate([b4] * group, axis=0)
    for g in range(B_KV_HEADS):
        def head_rows(pages, new_ref):
            first = jnp.where(last, new_ref[:, g, :], pages[0][:, g, :])
            return jnp.concatenate([first] + [r[:, g, :] for r in pages[1:]], axis=0).astype(BF16)

        kg = head_rows(k_pages, kn_ref)
        vg = head_rows(v_pages, vn_ref)
        s = lax.dot_general(q_ref[g], kg, _NT, preferred_element_type=F32) * HEAD_DIM ** -0.5 + bias
        m_old = m_scr[g]
        m_new = jnp.maximum(m_old, jnp.max(s, axis=1, keepdims=True))
        m_safe = jnp.where(m_new == -jnp.inf, 0.0, m_new)
        alpha = jnp.exp(m_old - m_safe)
        p = jnp.exp(s - m_safe)
        l_scr[g] = alpha * l_scr[g] + jnp.sum(p, axis=1, keepdims=True)
        acc_scr[g] = alpha * acc_scr[g] + jnp.dot(p.astype(BF16), vg, preferred_element_type=F32)
        m_scr[g] = m_new

    @pl.when(last)
    def _():
        for g in range(B_KV_HEADS):
            o = acc_scr[g] / l_scr[g]
            for hq in range(group):
                h = g * group + hq
                o_ref[:, h * HEAD_DIM:(h + 1) * HEAD_DIM] = o[hq * t_len:(hq + 1) * t_len]


def dsa_sample_attend(q, k, v, qi, wi, kidx, cache_k, cache_v, cache_kidx, page_table, layer):
    b, t = q.shape[:2]
    n_pages = page_table.shape[1]
    past = n_pages * PAGE_SIZE
    topk = min(IDX_TOPK_MAX, (past + t) // 4)
    pps = SAMPLE_PAGES_PER_STEP
    assert n_pages % pps == 0 and t <= SAMPLE_T_PAD
    n_steps = n_pages // pps + 1
    width = n_steps * pps * PAGE_SIZE

    qi_r = jnp.pad(jnp.swapaxes(qi, 1, 2), ((0, 0), (0, 0), (0, SAMPLE_T_PAD - t), (0, 0)))
    qi_hi, qi_lo = _hi_lo(qi_r.reshape(b, IDX_HEADS * SAMPLE_T_PAD, IDX_DIM))
    wi_r = jnp.pad(jnp.swapaxes(wi, 1, 2), ((0, 0), (0, 0), (0, SAMPLE_T_PAD - t)))
    wi_r = wi_r.reshape(b, IDX_HEADS * SAMPLE_T_PAD, 1)
    kidx_new = jnp.pad(kidx, ((0, 0), (0, PAGE_SIZE - t), (0, 0)))

    def page_spec(i, *tail):
        def imap(bi, j, pt):
            return (layer, pt[bi, jnp.minimum(j * pps + i, n_pages - 1)]) + (0,) * (1 + len(tail))
        return pl.BlockSpec((None, None, PAGE_SIZE) + tail, imap)

    def per_b(shape):
        return pl.BlockSpec((None,) + shape, lambda bi, j, pt: (bi,) + (0,) * len(shape))

    score = pl.pallas_call(
        functools.partial(_dsa_sample_score_kernel, n_steps=n_steps),
        grid_spec=pltpu.PrefetchScalarGridSpec(
            num_scalar_prefetch=1, grid=(b, n_steps),
            in_specs=[per_b((IDX_HEADS * SAMPLE_T_PAD, IDX_DIM)), per_b((IDX_HEADS * SAMPLE_T_PAD, IDX_DIM)),
                      per_b((IDX_HEADS * SAMPLE_T_PAD, 1))]
                     + [page_spec(i, IDX_DIM) for i in range(pps)] + [per_b((PAGE_SIZE, IDX_DIM))],
            out_specs=pl.BlockSpec((None, SAMPLE_T_PAD, pps * PAGE_SIZE), lambda bi, j, pt: (bi, 0, j))),
        out_shape=jax.ShapeDtypeStruct((b, SAMPLE_T_PAD, width), F32),
        compiler_params=pltpu.CompilerParams(dimension_semantics=("arbitrary", "arbitrary")),
        name="dsa_sample_score",
    )(page_table, qi_hi, qi_lo, wi_r, *([cache_kidx] * pps), kidx_new)

    group = B_HEADS // B_KV_HEADS
    q_r = q.reshape(b, t, B_KV_HEADS, group, HEAD_DIM)
    q_r = jnp.transpose(q_r, (0, 2, 3, 1, 4)).reshape(b, B_KV_HEADS, group * t, HEAD_DIM).astype(BF16)
    k_new = jnp.pad(k, ((0, 0), (0, PAGE_SIZE - t), (0, 0), (0, 0)))
    v_new = jnp.pad(v, ((0, 0), (0, PAGE_SIZE - t), (0, 0), (0, 0)))
    return pl.pallas_call(
        functools.partial(_dsa_sample_attn_kernel, n_steps=n_steps, topk=topk, past=past, t_len=t),
        grid_spec=pltpu.PrefetchScalarGridSpec(
            num_scalar_prefetch=1, grid=(b, n_steps),
            in_specs=[per_b((SAMPLE_T_PAD, width)), per_b((B_KV_HEADS, group * t, HEAD_DIM))]
                     + [page_spec(i, B_KV_HEADS, HEAD_DIM) for i in range(pps)] * 2
                     + [per_b((PAGE_SIZE, B_KV_HEADS, HEAD_DIM))] * 2,
            out_specs=pl.BlockSpec((None, t, B_HEADS * HEAD_DIM), lambda bi, j, pt: (bi, 0, 0)),
            scratch_shapes=[pltpu.VMEM((SAMPLE_T_PAD, width), F32),
                            pltpu.VMEM((B_KV_HEADS, group * t, 1), F32),
                            pltpu.VMEM((B_KV_HEADS, group * t, 1), F32),
                            pltpu.VMEM((B_KV_HEADS, group * t, HEAD_DIM), F32)]),
        out_shape=jax.ShapeDtypeStruct((b, t, B_HEADS * HEAD_DIM), F32),
        compiler_params=pltpu.CompilerParams(dimension_semantics=("arbitrary", "arbitrary")),
        name="dsa_sample_attn",
    )(page_table, score, q_r, *([cache_k] * pps), *([cache_v] * pps), k_new, v_new)


DIL_WM = 128


def _dilated_prompt_kernel(q_ref, kp_ref, kc_ref, vp_ref, vc_ref, o_ref, lse_ref, *, dil, heads):
    n = pl.program_id(1)
    iq = lax.broadcasted_iota(I32, (DIL_WM, 2 * DIL_WM), 0)
    ik = lax.broadcasted_iota(I32, (DIL_WM, 2 * DIL_WM), 1)
    dist = iq + DIL_WM - ik
    ok = jnp.logical_and(dist >= 0, dist <= DIL_WM)
    ok = jnp.logical_and(ok, jnp.logical_or(ik >= DIL_WM, n > 0))
    bias = jnp.where(ok, 0.0, -jnp.inf)
    for r in range(dil):
        def rows(ref, hs):
            if dil == 1:
                return ref[:, hs]
            return ref[pl.ds(r, DIL_WM, stride=dil), :]

        for h in range(heads):
            hs = slice(h * HEAD_DIM, (h + 1) * HEAD_DIM)
            q = rows(q_ref, hs).astype(BF16)
            k = jnp.concatenate([rows(kp_ref, hs), rows(kc_ref, hs)], axis=0).astype(BF16)
            v = jnp.concatenate([rows(vp_ref, hs), rows(vc_ref, hs)], axis=0).astype(BF16)
            s = lax.dot_general(q, k, _NT, preferred_element_type=F32) * HEAD_DIM ** -0.5 + bias
            m = jnp.max(s, axis=1, keepdims=True)
            p = jnp.exp(s - m)
            denom = jnp.sum(p, axis=1, keepdims=True)
            o = jnp.dot(p.astype(BF16), v, preferred_element_type=F32) / denom
            lse = jnp.broadcast_to(m + jnp.log(denom), (DIL_WM, HEAD_DIM))
            if dil == 1:
                o_ref[:, hs] = o
                lse_ref[:, hs] = lse
            else:
                o_ref[pl.ds(r, DIL_WM, stride=dil), :] = o
                lse_ref[pl.ds(r, DIL_WM, stride=dil), :] = lse


def dilated_prompt_attend(q, k, v, window, dil):
    b, s_len, h, d = q.shape
    assert window // dil == DIL_WM and s_len % (DIL_WM * dil) == 0
    rows = DIL_WM * dil
    heads = h if dil == 1 else 1
    bw = heads * d
    cur = pl.BlockSpec((None, rows, bw), lambda bi, n, hi: (bi, n, hi))
    prev = pl.BlockSpec((None, rows, bw), lambda bi, n, hi: (bi, jnp.maximum(n - 1, 0), hi))
    q2, k2, v2 = (z.reshape(b, s_len, h * d) for z in (q, k, v))
    o, lse = pl.pallas_call(
        functools.partial(_dilated_prompt_kernel, dil=dil, heads=heads),
        grid=(b, s_len // rows, (h * d) // bw),
        in_specs=[cur, prev, cur, prev, cur],
        out_specs=[cur, cur],
        out_shape=[jax.ShapeDtypeStruct(q2.shape, F32)] * 2,
        compiler_params=pltpu.CompilerParams(dimension_semantics=("arbitrary",) * 3),
        name="dilated_prompt",
    )(q2, k2, k2, v2, v2)
    return o.reshape(q.shape), lse.reshape(q.shape)


def _offsets(sizes):
    out, acc = [], 0
    for s in sizes[:-1]:
        acc += s
        out.append(acc)
    return out


def _rms(x, eps=NORM_EPS):
    xf = x.astype(F32)
    return xf * lax.rsqrt(jnp.mean(xf * xf, axis=-1, keepdims=True) + eps)


def rms_norm(x, g):
    return (_rms(x) * g.astype(F32)).astype(x.dtype)


def layer_norm(x, g, b, eps=NORM_EPS):
    xf = x.astype(F32)
    mu = jnp.mean(xf, axis=-1, keepdims=True)
    xc = xf - mu
    y = xc * lax.rsqrt(jnp.mean(xc * xc, axis=-1, keepdims=True) + eps)
    return (y * g.astype(F32) + b.astype(F32)).astype(x.dtype)


def partial_rope(x, pos):
    half = ROT_DIM // 2
    freqs = ROPE_THETA ** (-jnp.arange(half, dtype=F32) / half)
    ang = pos.astype(F32)[:, None] * freqs[None, :]
    ang = ang.reshape((1, ang.shape[0]) + (1,) * (x.ndim - 3) + (half,))
    cos, sin = jnp.cos(ang), jnp.sin(ang)
    xf = x.astype(F32)
    x1, x2 = xf[..., :half], xf[..., half:ROT_DIM]
    out = jnp.concatenate([x1 * cos - x2 * sin, x2 * cos + x1 * sin, xf[..., ROT_DIM:]], axis=-1)
    return out.astype(x.dtype)


def rwkv7_mix(sh, prev_row, s0, mu, w0, w_lora, a0, a_lora, k_k, k_a, r_k, gn_g, gn_b):
    bn, t, _ = sh.shape
    prev = jnp.concatenate([prev_row[:, None, :].astype(sh.dtype), sh[:, :-1]], axis=1)
    xm = sh + (prev - sh) * mu
    r, k, v, xw, xa = jnp.split(xm, [A_WIDTH, 2 * A_WIDTH, 3 * A_WIDTH, 3 * A_WIDTH + DECAY_LORA], axis=-1)
    wlog = -jax.nn.softplus(-(w0 + mm(jnp.tanh(xw), w_lora))) - 0.5
    decay = jnp.exp(-jnp.exp(wlog.astype(F32)))
    a = jax.nn.sigmoid((a0 + mm(xa, a_lora)).astype(F32))

    def heads(z):
        return z.astype(F32).reshape(bn, t, A_HEADS, A_HEAD)

    def head_sum(z):
        return jnp.broadcast_to(jnp.sum(heads(z), axis=-1, keepdims=True),
                                (bn, t, A_HEADS, A_HEAD)).reshape(bn, t, A_WIDTH)

    kk = heads(k * k_k)
    kk = kk / jnp.maximum(jnp.sqrt(jnp.sum(kk * kk, axis=-1, keepdims=True)), 1e-12)
    kk = kk.reshape(bn, t, A_WIDTH)
    k2 = k * (1.0 + (a - 1.0) * k_a.astype(F32))
    kka = kk * a
    q = decay * r - kk * head_sum(kka * r)
    vc = v * head_sum(k2 * r)

    def pair_major(z):
        return jnp.swapaxes(z.reshape(bn, t, A_WIDTH // V7X_LANES, V7X_LANES), 1, 2)

    vt = jnp.swapaxes(v, 1, 2)
    vt_hi = vt.astype(BF16)
    vt_lo = (vt - vt_hi.astype(F32)).astype(BF16)
    yt, s_fin = rwkv_scan(*[pair_major(z) for z in (-kk, decay, kka, k2, v, q, vc)], vt_hi, vt_lo,
                          s0.astype(F32))
    y = heads(jnp.swapaxes(yt, 1, 2))
    ym = jnp.mean(y, axis=-1, keepdims=True)
    yc = y - ym
    yn = yc * lax.rsqrt(jnp.mean(yc * yc, axis=-1, keepdims=True) + GN_EPS)
    yn = yn.reshape(bn, t, A_WIDTH) * gn_g.astype(F32) + gn_b.astype(F32)
    bonus = (jnp.sum(heads(r * k2) * r_k.astype(F32), axis=-1, keepdims=True) * heads(v)).reshape(bn, t, A_WIDTH)
    return (yn + bonus).astype(sh.dtype), s_fin.astype(s0.dtype), sh[:, -1]


def dsa_project(u, pos, q_norm, k_norm, qi_norm, w_qi, kidx_g, kidx_b):
    bn, t, _ = u.shape
    sizes = (B_WIDTH, B_KV_HEADS * HEAD_DIM, B_KV_HEADS * HEAD_DIM, IDX_Q_RANK, IDX_DIM, IDX_HEADS)
    q, k, v, cqi, kidx, wi = jnp.split(u, _offsets(sizes), axis=-1)
    q = partial_rope(rms_norm(q.reshape(bn, t, B_HEADS, HEAD_DIM), q_norm), pos)
    k = partial_rope(rms_norm(k.reshape(bn, t, B_KV_HEADS, HEAD_DIM), k_norm), pos)
    v = v.reshape(bn, t, B_KV_HEADS, HEAD_DIM)
    qi = mm(rms_norm(cqi, qi_norm), w_qi).reshape(bn, t, IDX_HEADS, IDX_DIM)
    qi = partial_rope(qi, pos)
    kidx = partial_rope(layer_norm(kidx, kidx_g, kidx_b), pos)
    wi = wi * IDX_HEADS ** -0.5
    return q, k, v, qi, kidx, wi


def even_mixer(xn, pos, prev_row, s0, attend, ep):
    u = mm(xn, ep['w_in'])
    u_a, g_a, u_b, g_b = jnp.split(u, [SHIFT_W, IN_A, IN_A + IN_B - B_WIDTH], axis=-1)
    y_a, s_fin, last_row = rwkv7_mix(u_a, prev_row, s0, ep['shift_mu'], ep['w0'], ep['w_lora'],
                                     ep['a0'], ep['a_lora'], ep['k_k'], ep['k_a'], ep['r_k'],
                                     ep['gn_gain'], ep['gn_bias'])
    q, k, v, qi, kidx, wi = dsa_project(u_b, pos, ep['q_norm'], ep['k_norm'], ep['qi_norm'],
                                        ep['w_qi'], ep['kidx_gain'], ep['kidx_bias'])
    y_b = attend(q, k, v, qi, wi, kidx)
    y = jnp.concatenate([y_a * jax.nn.silu(g_a), y_b * jax.nn.silu(g_b)], axis=-1)
    return mm(y, ep['w_out']), (s_fin, last_row, k, v, kidx)


def dilated_sample(q, k, v, buf_k, buf_v, window, dil):
    t = q.shape[1]
    d = q.shape[-1]
    wb = buf_k.shape[1]
    wm = window // dil
    kc = jnp.concatenate([buf_k.astype(k.dtype), k], axis=1)
    vc = jnp.concatenate([buf_v.astype(v.dtype), v], axis=1)
    idx = wb + jnp.arange(t)[:, None] - jnp.arange(wm + 1)[None, :] * dil
    valid = idx >= 0
    idxc = jnp.maximum(idx, 0)
    kg, vg = kc[:, idxc], vc[:, idxc]
    s = jnp.einsum('bthd,btjhd->bthj', q, kg).astype(F32) * d ** -0.5
    s = jnp.where(valid[None, :, None, :], s, -jnp.inf)
    lse = jax.nn.logsumexp(s, axis=-1)
    p = jnp.exp(s - lse[..., None])
    o = jnp.einsum('bthj,btjhd->bthd', p.astype(vg.dtype), vg)
    return o, lse, kc[:, -wb:], vc[:, -wb:]


def window_attend_prompt(g, q, k, v, win, dil):
    o, lse = dilated_prompt_attend(q, k, v, win, dil)
    keep = min(win, q.shape[1])
    return o, lse, k[:, -keep:], v[:, -keep:]


def window_attend_sample(g, q, k, v, win, dil, bufs_k, bufs_v, layer):
    return dilated_sample(q, k, v, bufs_k[g][layer], bufs_v[g][layer], win, dil)


def odd_mixer(xn, pos, attend, op):
    u = mm(xn, op['w_in'])
    bn, t, _ = u.shape
    n_g = len(C_GROUPS)
    qkv, gate = jnp.split(u, [3 * n_g * C_WIDTH], axis=-1)
    qkv = qkv.reshape(bn, t, n_g, 3, C_HEADS, HEAD_DIM)
    outs, lses, bufs = [], [], []
    for g, (win, dil) in enumerate(C_GROUPS):
        q = partial_rope(rms_norm(qkv[:, :, g, 0], op['q_norm'][g]), pos)
        k = partial_rope(rms_norm(qkv[:, :, g, 1], op['k_norm'][g]), pos)
        v = qkv[:, :, g, 2]
        o, lse, kb, vb = attend(g, q, k, v, win, dil)
        outs.append(o.astype(F32))
        lses.append(lse if lse.ndim == o.ndim else lse[..., None])
        bufs += [kb, vb]
    alpha = jax.nn.softmax(jnp.stack(lses, axis=0), axis=0)
    o = jnp.sum(alpha * jnp.stack(outs, axis=0), axis=0)
    y = o.reshape(bn, t, C_WIDTH).astype(xn.dtype) * jax.nn.silu(gate)
    return mm(y, op['w_out']), tuple(bufs)


def ple_add(h, p_l, w_proj, w_gate):
    gate = jax.nn.sigmoid(mm(_rms(h).astype(h.dtype), w_gate))
    return h + gate * mm(p_l, w_proj)


def kernel(x_prompt, x_sample, p_prompt, p_sample, state_wkv, state_shift, cache_k, cache_v,
           cache_kidx, page_table, cache_win_k0, cache_win_v0, cache_win_k1, cache_win_v1,
           cache_win_k2, cache_win_v2, ln_gain, e_w_in, e_shift_mu, e_w0, e_w_lora, e_a0,
           e_a_lora, e_k_k, e_k_a, e_r_k, e_gn_gain, e_gn_bias, e_q_norm, e_k_norm, e_qi_norm,
           e_w_qi, e_kidx_gain, e_kidx_bias, e_w_out, o_w_in, o_q_norm, o_k_norm, o_w_out,
           ple_w_proj, ple_w_gate):
    depth = ln_gain.shape[0]
    bp, s_len, _ = x_prompt.shape
    t_len = x_sample.shape[1]
    past = page_table.shape[1] * PAGE_SIZE
    pos_p = jnp.arange(s_len, dtype=jnp.int32)
    pos_s = past + jnp.arange(t_len, dtype=jnp.int32)
    bufs_k = (cache_win_k0, cache_win_k1, cache_win_k2)
    bufs_v = (cache_win_v0, cache_win_v1, cache_win_v2)
    hp, hs = x_prompt, x_sample
    ev_p, ev_s, od_p, od_s = [], [], [], []
    for i in range(depth):
        l = i // 2
        xp = rms_norm(hp, ln_gain[i])
        xs = rms_norm(hs, ln_gain[i])
        if i % 2 == 0:
            ep = {'w_in': e_w_in[l], 'shift_mu': e_shift_mu[l], 'w0': e_w0[l], 'w_lora': e_w_lora[l],
                  'a0': e_a0[l], 'a_lora': e_a_lora[l], 'k_k': e_k_k[l], 'k_a': e_k_a[l],
                  'r_k': e_r_k[l], 'gn_gain': e_gn_gain[l], 'gn_bias': e_gn_bias[l],
                  'q_norm': e_q_norm[l], 'k_norm': e_k_norm[l], 'qi_norm': e_qi_norm[l],
                  'w_qi': e_w_qi[l], 'kidx_gain': e_kidx_gain[l], 'kidx_bias': e_kidx_bias[l],
                  'w_out': e_w_out[l]}
            row0 = jnp.zeros((bp, SHIFT_W), hp.dtype)
            st0 = jnp.zeros((bp, A_HEADS, A_HEAD, A_HEAD), hp.dtype)
            mp, stp = even_mixer(xp, pos_p, row0, st0, dsa_prompt_attend, ep)
            att_s = functools.partial(dsa_sample_attend, cache_k=cache_k, cache_v=cache_v,
                                      cache_kidx=cache_kidx, page_table=page_table, layer=l)
            ms, sts = even_mixer(xs, pos_s, state_shift[l], state_wkv[l], att_s, ep)
            ev_p.append(stp)
            ev_s.append(sts)
        else:
            op = {'w_in': o_w_in[l], 'q_norm': o_q_norm[l], 'k_norm': o_k_norm[l], 'w_out': o_w_out[l]}
            mp, stp = odd_mixer(xp, pos_p, window_attend_prompt, op)
            att_s = functools.partial(window_attend_sample, bufs_k=bufs_k, bufs_v=bufs_v, layer=l)
            ms, sts = odd_mixer(xs, pos_s, att_s, op)
            od_p.append(stp)
            od_s.append(sts)
        hp = ple_add(hp + mp, p_prompt[i], ple_w_proj[i], ple_w_gate[i])
        hs = ple_add(hs + ms, p_sample[i], ple_w_proj[i], ple_w_gate[i])

    def st(lst, j):
        return jnp.stack([e[j] for e in lst], axis=0)

    outs = [hp, hs, st(ev_p, 0), st(ev_s, 0), st(ev_p, 1), st(ev_s, 1)]
    outs += [st(ev_p, j) for j in (2, 3, 4)] + [st(ev_s, j) for j in (2, 3, 4)]
    outs += [st(od_p, j) for j in range(6)] + [st(od_s, j) for j in range(6)]
    return tuple(outs)
```

```python
import functools

import jax
import jax.numpy as jnp
from jax import lax
from jax.experimental import pallas as pl
from jax.experimental.pallas import tpu as pltpu

F32 = jnp.float32
BF16 = jnp.bfloat16
I32 = jnp.int32

D_MODEL = 4096
PAGE_SIZE = 128
HEAD_DIM = 128
ROT_DIM = HEAD_DIM // 4
ROPE_THETA = 500000.0
NORM_EPS = 1e-6

A_WIDTH = D_MODEL // 2
A_HEAD = 64
A_HEADS = A_WIDTH // A_HEAD
DECAY_LORA = 96
AAA_LORA = 96
GN_EPS = 64e-5
SHIFT_W = 3 * A_WIDTH + DECAY_LORA + AAA_LORA

B_WIDTH = D_MODEL // 2
B_HEADS = B_WIDTH // HEAD_DIM
B_KV_HEADS = 4
IDX_HEADS = 16
IDX_DIM = 128
IDX_Q_RANK = 512
IDX_TOPK_MAX = 256
QBLOCK = 128

C_GROUPS = ((128, 1), (512, 4), (2048, 16))
C_HEADS = 16
C_WIDTH = C_HEADS * HEAD_DIM

IN_A = SHIFT_W + A_WIDTH
IN_B = B_WIDTH + 2 * B_KV_HEADS * HEAD_DIM + IDX_Q_RANK + IDX_DIM + IDX_HEADS + B_WIDTH

V7X_LANES = 128
V7X_VMEM_LIMIT_BYTES = 58 * 1024 * 1024
INT_MIN = -2 ** 31

_NT = (((1,), (1,)), ((), ()))


def _matmul_kernel(a_ref, b_ref, o_ref, bq_ref):
    @pl.when(pl.program_id(1) == 0)
    def _():
        bq_ref[...] = b_ref[...].astype(BF16)

    o_ref[...] = jnp.dot(a_ref[...].astype(BF16), bq_ref[...], preferred_element_type=F32)


def _pick_tile(n, cands):
    for c in cands:
        if n % c == 0:
            return c
    return n


def matmul(a, b):
    m, k = a.shape
    _, n = b.shape
    n_pad = -(-n // V7X_LANES) * V7X_LANES
    if n_pad != n:
        b = jnp.pad(b, ((0, 0), (0, n_pad - n)))
    tn = _pick_tile(n_pad, (1024, 512, 256, 128))
    tm = _pick_tile(m, (512, 256, 128, 64, 32, 16, 8))
    out = pl.pallas_call(
        _matmul_kernel,
        grid=(n_pad // tn, m // tm),
        in_specs=[pl.BlockSpec((tm, k), lambda j, i: (i, 0)),
                  pl.BlockSpec((k, tn), lambda j, i: (0, j))],
        out_specs=pl.BlockSpec((tm, tn), lambda j, i: (i, j)),
        out_shape=jax.ShapeDtypeStruct((m, n_pad), F32),
        scratch_shapes=[pltpu.VMEM((k, tn), BF16)],
        compiler_params=pltpu.CompilerParams(
            dimension_semantics=("arbitrary", "arbitrary"),
            vmem_limit_bytes=V7X_VMEM_LIMIT_BYTES),
        name="matmul",
    )(a, b)
    return out[:, :n] if n_pad != n else out


def mm(x, w):
    lead = x.shape[:-1]
    return matmul(x.reshape(-1, x.shape[-1]).astype(BF16), w).reshape(lead + (w.shape[-1],))


def _split_bf16(x):
    hi = x.astype(BF16)
    lo = (x - hi.astype(F32)).astype(BF16)
    return jnp.concatenate([hi, lo], axis=1)


def _rwkv_scan_kernel(nkk_ref, w_ref, kka_ref, k_ref, v_ref, q_ref, vc_ref, vth_ref, vtl_ref, s0_ref,
                      yt_ref, sout_ref, s_scr, *, pairs, steps, use_mxu):
    tchunk = pl.program_id(2)

    @pl.when(tchunk == 0)
    def _():
        for p in range(pairs):
            s_scr[p] = jnp.concatenate([s0_ref[0, 2 * p], s0_ref[0, 2 * p + 1]], axis=1)

    lane = lax.broadcasted_iota(I32, (A_HEAD, V7X_LANES), 1)
    row = lax.broadcasted_iota(I32, (A_HEAD, V7X_LANES), 0)
    lo = lane < A_HEAD
    eye_lo = lane == row
    eye_hi = lane == row + A_HEAD
    eye = jnp.logical_or(eye_lo, eye_hi)
    lane_t = lax.broadcasted_iota(I32, (A_HEAD, steps), 1)
    yt_ref[...] = jnp.zeros(yt_ref.shape, F32)
    if use_mxu:
        kk_i = lax.broadcasted_iota(I32, (2 * V7X_LANES, 2 * V7X_LANES), 0)
        nn_i = lax.broadcasted_iota(I32, (2 * V7X_LANES, 2 * V7X_LANES), 1)
        seg_mat = jnp.where(((kk_i % V7X_LANES) >= A_HEAD) == (nn_i >= V7X_LANES), 1.0, 0.0).astype(BF16)
        oh_k = lax.broadcasted_iota(I32, (2 * V7X_LANES, V7X_LANES), 0) % V7X_LANES

    def seg_sum(x):
        s_lo = jnp.sum(jnp.where(lo, x, 0.0), axis=1, keepdims=True)
        s_hi = jnp.sum(jnp.where(lo, 0.0, x), axis=1, keepdims=True)
        return s_lo, s_hi

    def step(t, carry):
        if use_mxu:
            onehot = jnp.where(oh_k == t, 1.0, 0.0).astype(BF16)
        for p in range(pairs):
            def rowvec(ref):
                return jnp.broadcast_to(ref[p, pl.ds(t, 1), :], (A_HEAD, V7X_LANES))

            s = s_scr[p]
            sa_lo, sa_hi = seg_sum(s * rowvec(nkk_ref))
            py = s * rowvec(q_ref) + jnp.where(eye, rowvec(vc_ref), 0.0)
            r0 = p * V7X_LANES
            if use_mxu:
                yy = jnp.dot(_split_bf16(py), seg_mat, preferred_element_type=F32)
                y_lo, y_hi = yy[:, :V7X_LANES], yy[:, V7X_LANES:]
                vt = jnp.concatenate([vth_ref[r0:r0 + V7X_LANES, :], vtl_ref[r0:r0 + V7X_LANES, :]], axis=1)
                vv = jnp.dot(vt, onehot, preferred_element_type=F32)
                v_b = jnp.where(lo, vv[:A_HEAD], vv[A_HEAD:])
            else:
                y_lo, y_hi = seg_sum(py)
                vrow = rowvec(v_ref)
                v_lo = jnp.sum(jnp.where(eye_lo, vrow, 0.0), axis=1, keepdims=True)
                v_hi = jnp.sum(jnp.where(eye_hi, vrow, 0.0), axis=1, keepdims=True)
                v_b = jnp.where(lo, v_lo, v_hi)
            sa_b = jnp.where(lo, sa_lo, sa_hi)
            s_scr[p] = s * rowvec(w_ref) + sa_b * rowvec(kka_ref) + v_b * rowvec(k_ref)
            yt_ref[0, r0:r0 + A_HEAD, :] = jnp.where(lane_t == t, y_lo, yt_ref[0, r0:r0 + A_HEAD, :])
            yt_ref[0, r0 + A_HEAD:r0 + V7X_LANES, :] = jnp.where(
                lane_t == t, y_hi, yt_ref[0, r0 + A_HEAD:r0 + V7X_LANES, :])
        return carry

    lax.fori_loop(0, steps, step, 0, unroll=RWKV_UNROLL if steps % RWKV_UNROLL == 0 else 1)

    @pl.when(tchunk == pl.num_programs(2) - 1)
    def _():
        for p in range(pairs):
            s = s_scr[p]
            sout_ref[0, 2 * p] = s[:, :A_HEAD]
            sout_ref[0, 2 * p + 1] = s[:, A_HEAD:]


RWKV_PAIRS_PER_STEP = 8
RWKV_UNROLL = 4


def rwkv_scan(nkk, w, kka, k, v, q, vc, vt_hi, vt_lo, s0):
    b, npairs, t, _ = nkk.shape
    pairs = RWKV_PAIRS_PER_STEP
    c = npairs * V7X_LANES
    tc = min(t, V7X_LANES)
    assert t % tc == 0 and npairs % pairs == 0
    bw = pairs * V7X_LANES
    row_spec = pl.BlockSpec((None, pairs, tc, V7X_LANES), lambda bi, hi, ti: (bi, hi, ti, 0))
    vt_spec = pl.BlockSpec((None, bw, tc), lambda bi, hi, ti: (bi, hi, ti))
    st_spec = pl.BlockSpec((1, 2 * pairs, A_HEAD, A_HEAD), lambda bi, hi, ti: (bi, hi, 0, 0))
    return pl.pallas_call(
        functools.partial(_rwkv_scan_kernel, pairs=pairs, steps=tc, use_mxu=(tc == V7X_LANES)),
        grid=(b, npairs // pairs, t // tc),
        in_specs=[row_spec] * 7 + [vt_spec] * 2 + [st_spec],
        out_specs=[pl.BlockSpec((1, bw, tc), lambda bi, hi, ti: (bi, hi, ti)), st_spec],
        out_shape=[jax.ShapeDtypeStruct((b, c, t), F32), jax.ShapeDtypeStruct(s0.shape, F32)],
        scratch_shapes=[pltpu.VMEM((pairs, A_HEAD, V7X_LANES), F32)],
        compiler_params=pltpu.CompilerParams(dimension_semantics=("arbitrary", "arbitrary", "arbitrary")),
        name="rwkv_scan",
    )(nkk, w, kka, k, v, q, vc, vt_hi, vt_lo, s0)


def _select_topk_mask(score, allowed, topk):
    r, l = score.shape
    score = jnp.where(score == 0.0, 0.0, score)
    bits = pltpu.bitcast(score, I32)
    key = jnp.where(bits < 0, bits ^ jnp.int32(0x7FFFFFFF), bits)
    key = jnp.where(allowed, key, jnp.int32(INT_MIN))
    kf = jnp.float32(topk)

    def count(pred):
        return jnp.sum(jnp.where(pred, 1.0, 0.0), axis=1, keepdims=True)

    def bit_step(i, prefix):
        cand = prefix | lax.shift_left(jnp.int32(1), jnp.int32(31) - i)
        ok = count(key >= (cand ^ jnp.int32(INT_MIN))) >= kf
        return jnp.where(ok, cand, prefix)

    prefix = lax.fori_loop(0, 32, bit_step, jnp.zeros((r, 1), I32))
    thr = prefix ^ jnp.int32(INT_MIN)
    gt = key > thr
    eq = jnp.logical_and(key == thr, allowed)
    need = kf - count(gt)
    li = lax.broadcasted_iota(I32, (V7X_LANES, 2 * V7X_LANES), 0)
    lj = lax.broadcasted_iota(I32, (V7X_LANES, 2 * V7X_LANES), 1)
    tri_ones = jnp.where(jnp.logical_or(lj >= V7X_LANES, li < lj), 1.0, 0.0).astype(BF16)
    running = jnp.zeros((r, V7X_LANES), F32)
    sel = []
    for c in range(l // V7X_LANES):
        sl = slice(c * V7X_LANES, (c + 1) * V7X_LANES)
        eq_c = eq[:, sl]
        res = jnp.dot(jnp.where(eq_c, 1.0, 0.0).astype(BF16), tri_ones, preferred_element_type=F32)
        before = res[:, :V7X_LANES] + running
        running = running + res[:, V7X_LANES:]
        sel.append(jnp.logical_or(gt[:, sl], jnp.logical_and(eq_c, before < need)))
    return jnp.concatenate(sel, axis=1)


def _dsa_prompt_kernel(qi_ref, qil_ref, wi_ref, kidx_ref, kidxl_ref, q_ref, k_ref, v_ref, o_ref, *, topk):
    qb = pl.program_id(1)
    l = kidx_ref.shape[0]
    kidx = kidx_ref[...]
    kidx_lo = kidxl_ref[...]
    score = jnp.zeros((QBLOCK, l), F32)
    for h in range(IDX_HEADS):
        hs = slice(h * IDX_DIM, (h + 1) * IDX_DIM)
        s = (lax.dot_general(qi_ref[:, hs], kidx, _NT, preferred_element_type=F32)
             + lax.dot_general(qi_ref[:, hs], kidx_lo, _NT, preferred_element_type=F32)
             + lax.dot_general(qil_ref[:, hs], kidx, _NT, preferred_element_type=F32))
        s = jnp.maximum(s * IDX_DIM ** -0.5, 0.0)
        score = score + s * wi_ref[:, h:h + 1]
    qpos = qb * QBLOCK + lax.broadcasted_iota(I32, (QBLOCK, l), 0)
    kpos = lax.broadcasted_iota(I32, (QBLOCK, l), 1)
    sel = _select_topk_mask(score, kpos <= qpos, topk)
    bias = jnp.where(sel, 0.0, -jnp.inf)
    group = B_HEADS // B_KV_HEADS
    for g in range(B_KV_HEADS):
        kg = k_ref[:, g * HEAD_DIM:(g + 1) * HEAD_DIM]
        vg = v_ref[:, g * HEAD_DIM:(g + 1) * HEAD_DIM]
        for j in range(group):
            h = g * group + j
            s = lax.dot_general(q_ref[:, h * HEAD_DIM:(h + 1) * HEAD_DIM], kg, _NT, preferred_element_type=F32)
            s = s * HEAD_DIM ** -0.5 + bias
            m = jnp.max(s, axis=1, keepdims=True)
            p = jnp.exp(s - m)
            denom = jnp.sum(p, axis=1, keepdims=True)
            o = jnp.dot(p.astype(BF16), vg, preferred_element_type=F32)
            o_ref[:, h * HEAD_DIM:(h + 1) * HEAD_DIM] = o / denom


def dsa_prompt_attend(q, k, v, qi, wi, kidx):
    b, s_len = q.shape[:2]
    topk = min(IDX_TOPK_MAX, s_len // 4)

    def flat16(z):
        return z.reshape(b, s_len, -1).astype(BF16)

    def flat16_lo(z):
        z = z.reshape(b, s_len, -1)
        return (z - z.astype(BF16).astype(F32)).astype(BF16)

    def qspec(w):
        return pl.BlockSpec((None, QBLOCK, w), lambda bi, qb: (bi, qb, 0))

    def kspec(w):
        return pl.BlockSpec((None, s_len, w), lambda bi, qb: (bi, 0, 0))

    return pl.pallas_call(
        functools.partial(_dsa_prompt_kernel, topk=topk),
        grid=(b, s_len // QBLOCK),
        in_specs=[qspec(B_WIDTH), qspec(B_WIDTH), qspec(IDX_HEADS), kspec(IDX_DIM), kspec(IDX_DIM),
                  qspec(B_WIDTH), kspec(B_KV_HEADS * HEAD_DIM), kspec(B_KV_HEADS * HEAD_DIM)],
        out_specs=qspec(B_WIDTH),
        out_shape=jax.ShapeDtypeStruct((b, s_len, B_WIDTH), F32),
        compiler_params=pltpu.CompilerParams(dimension_semantics=("arbitrary", "arbitrary"),
                                             vmem_limit_bytes=V7X_VMEM_LIMIT_BYTES),
        name="dsa_prompt",
    )(flat16(qi), flat16_lo(qi), wi, flat16(kidx), flat16_lo(kidx), flat16(q), flat16(k), flat16(v))


SAMPLE_PAGES_PER_STEP = 8
SAMPLE_T_PAD = 8


def _hi_lo(x):
    hi = x.astype(BF16)
    return hi, (x - hi.astype(F32)).astype(BF16)


def _dsa_sample_score_kernel(pt_ref, qi_ref, qil_ref, wi_ref, *refs, n_steps):
    j = pl.program_id(1)
    pages, new_ref, o_ref = refs[:-2], refs[-2], refs[-1]
    qh, ql, wcol = qi_ref[...], qil_ref[...], wi_ref[...]
    for i, pref in enumerate(pages):
        kidx = pref[...]
        if i == 0:
            kidx = jnp.where(j == n_steps - 1, new_ref[...], kidx)
        kh, kl = _hi_lo(kidx)
        s = (lax.dot_general(qh, kh, _NT, preferred_element_type=F32)
             + lax.dot_general(qh, kl, _NT, preferred_element_type=F32)
             + lax.dot_general(ql, kh, _NT, preferred_element_type=F32))
        s = jnp.maximum(s * IDX_DIM ** -0.5, 0.0) * wcol
        tot = s[0:SAMPLE_T_PAD]
        for h in range(1, IDX_HEADS):
            tot = tot + s[h * SAMPLE_T_PAD:(h + 1) * SAMPLE_T_PAD]
        o_ref[:, i * PAGE_SIZE:(i + 1) * PAGE_SIZE] = tot


def _dsa_sample_attn_kernel(pt_ref, score_ref, q_ref, *refs, n_steps, topk, past, t_len):
    pps = SAMPLE_PAGES_PER_STEP
    k_pages, v_pages = refs[:pps], refs[pps:2 * pps]
    kn_ref, vn_ref, o_ref, bias_scr, m_scr, l_scr, acc_scr = refs[2 * pps:]
    j = pl.program_id(1)
    width = score_ref.shape[1]
    group = B_HEADS // B_KV_HEADS

    @pl.when(j == 0)
    def _():
        qpos = past + lax.broadcasted_iota(I32, (SAMPLE_T_PAD, width), 0)
        kpos = lax.broadcasted_iota(I32, (SAMPLE_T_PAD, width), 1)
        sel = _select_topk_mask(score_ref[...], kpos <= qpos, topk)
        bias_scr[...] = jnp.where(sel, 0.0, -jnp.inf)
        m_scr[...] = jnp.full(m_scr.shape, -jnp.inf, F32)
        l_scr[...] = jnp.zeros(l_scr.shape, F32)
        acc_scr[...] = jnp.zeros(acc_scr.shape, F32)

    last = j == n_steps - 1
    col = pl.multiple_of(j * (pps * PAGE_SIZE), pps * PAGE_SIZE)
    b4 = bias_scr[0:t_len, pl.ds(col, pps * PAGE_SIZE)]
    bias = jnp.concatenate([b4] * group, axis=0)
    for g in range(B_KV_HEADS):
        def head_rows(pages, new_ref):
            first = jnp.where(last, new_ref[:, g, :], pages[0][:, g, :])
            return jnp.concatenate([first] + [r[:, g, :] for r in pages[1:]], axis=0).astype(BF16)

        kg = head_rows(k_pages, kn_ref)
        vg = head_rows(v_pages, vn_ref)
        s = lax.dot_general(q_ref[g], kg, _NT, preferred_element_type=F32) * HEAD_DIM ** -0.5 + bias
        m_old = m_scr[g]
        m_new = jnp.maximum(m_old, jnp.max(s, axis=1, keepdims=True))
        m_safe = jnp.where(m_new == -jnp.inf, 0.0, m_new)
        alpha = jnp.exp(m_old - m_safe)
        p = jnp.exp(s - m_safe)
        l_scr[g] = alpha * l_scr[g] + jnp.sum(p, axis=1, keepdims=True)
        acc_scr[g] = alpha * acc_scr[g] + jnp.dot(p.astype(BF16), vg, preferred_element_type=F32)
        m_scr[g] = m_new

    @pl.when(last)
    def _():
        for g in range(B_KV_HEADS):
            o = acc_scr[g] / l_scr[g]
            for hq in range(group):
                h = g * group + hq
                o_ref[:, h * HEAD_DIM:(h + 1) * HEAD_DIM] = o[hq * t_len:(hq + 1) * t_len]


def dsa_sample_attend(q, k, v, qi, wi, kidx, cache_k, cache_v, cache_kidx, page_table, layer):
    b, t = q.shape[:2]
    q = q.reshape(b, t, B_HEADS, HEAD_DIM)
    k = k.reshape(b, t, B_KV_HEADS, HEAD_DIM)
    v = v.reshape(b, t, B_KV_HEADS, HEAD_DIM)
    qi = qi.reshape(b, t, IDX_HEADS, IDX_DIM)
    n_pages = page_table.shape[1]
    past = n_pages * PAGE_SIZE
    topk = min(IDX_TOPK_MAX, (past + t) // 4)
    pps = SAMPLE_PAGES_PER_STEP
    assert n_pages % pps == 0 and t <= SAMPLE_T_PAD
    n_steps = n_pages // pps + 1
    width = n_steps * pps * PAGE_SIZE

    qi_r = jnp.pad(jnp.swapaxes(qi, 1, 2), ((0, 0), (0, 0), (0, SAMPLE_T_PAD - t), (0, 0)))
    qi_hi, qi_lo = _hi_lo(qi_r.reshape(b, IDX_HEADS * SAMPLE_T_PAD, IDX_DIM))
    wi_r = jnp.pad(jnp.swapaxes(wi, 1, 2), ((0, 0), (0, 0), (0, SAMPLE_T_PAD - t)))
    wi_r = wi_r.reshape(b, IDX_HEADS * SAMPLE_T_PAD, 1)
    kidx_new = jnp.pad(kidx, ((0, 0), (0, PAGE_SIZE - t), (0, 0)))

    def page_spec(i, *tail):
        def imap(bi, j, pt):
            return (layer, pt[bi, jnp.minimum(j * pps + i, n_pages - 1)]) + (0,) * (1 + len(tail))
        return pl.BlockSpec((None, None, PAGE_SIZE) + tail, imap)

    def per_b(shape):
        return pl.BlockSpec((None,) + shape, lambda bi, j, pt: (bi,) + (0,) * len(shape))

    score = pl.pallas_call(
        functools.partial(_dsa_sample_score_kernel, n_steps=n_steps),
        grid_spec=pltpu.PrefetchScalarGridSpec(
            num_scalar_prefetch=1, grid=(b, n_steps),
            in_specs=[per_b((IDX_HEADS * SAMPLE_T_PAD, IDX_DIM)), per_b((IDX_HEADS * SAMPLE_T_PAD, IDX_DIM)),
                      per_b((IDX_HEADS * SAMPLE_T_PAD, 1))]
                     + [page_spec(i, IDX_DIM) for i in range(pps)] + [per_b((PAGE_SIZE, IDX_DIM))],
            out_specs=pl.BlockSpec((None, SAMPLE_T_PAD, pps * PAGE_SIZE), lambda bi, j, pt: (bi, 0, j))),
        out_shape=jax.ShapeDtypeStruct((b, SAMPLE_T_PAD, width), F32),
        compiler_params=pltpu.CompilerParams(dimension_semantics=("arbitrary", "arbitrary")),
        name="dsa_sample_score",
    )(page_table, qi_hi, qi_lo, wi_r, *([cache_kidx] * pps), kidx_new)

    group = B_HEADS // B_KV_HEADS
    q_r = q.reshape(b, t, B_KV_HEADS, group, HEAD_DIM)
    q_r = jnp.transpose(q_r, (0, 2, 3, 1, 4)).reshape(b, B_KV_HEADS, group * t, HEAD_DIM).astype(BF16)
    k_new = jnp.pad(k, ((0, 0), (0, PAGE_SIZE - t), (0, 0), (0, 0)))
    v_new = jnp.pad(v, ((0, 0), (0, PAGE_SIZE - t), (0, 0), (0, 0)))
    return pl.pallas_call(
        functools.partial(_dsa_sample_attn_kernel, n_steps=n_steps, topk=topk, past=past, t_len=t),
        grid_spec=pltpu.PrefetchScalarGridSpec(
            num_scalar_prefetch=1, grid=(b, n_steps),
            in_specs=[per_b((SAMPLE_T_PAD, width)), per_b((B_KV_HEADS, group * t, HEAD_DIM))]
                     + [page_spec(i, B_KV_HEADS, HEAD_DIM) for i in range(pps)] * 2
                     + [per_b((PAGE_SIZE, B_KV_HEADS, HEAD_DIM))] * 2,
            out_specs=pl.BlockSpec((None, t, B_HEADS * HEAD_DIM), lambda bi, j, pt: (bi, 0, 0)),
            scratch_shapes=[pltpu.VMEM((SAMPLE_T_PAD, width), F32),
                            pltpu.VMEM((B_KV_HEADS, group * t, 1), F32),
                            pltpu.VMEM((B_KV_HEADS, group * t, 1), F32),
                            pltpu.VMEM((B_KV_HEADS, group * t, HEAD_DIM), F32)]),
        out_shape=jax.ShapeDtypeStruct((b, t, B_HEADS * HEAD_DIM), F32),
        compiler_params=pltpu.CompilerParams(dimension_semantics=("arbitrary", "arbitrary")),
        name="dsa_sample_attn",
    )(page_table, score, q_r, *([cache_k] * pps), *([cache_v] * pps), k_new, v_new)


DIL_WM = 128


def _dilated_prompt_kernel(q_ref, kp_ref, kc_ref, vp_ref, vc_ref, o_ref, lse_ref, *, dil, heads):
    n = pl.program_id(1)
    iq = lax.broadcasted_iota(I32, (DIL_WM, 2 * DIL_WM), 0)
    ik = lax.broadcasted_iota(I32, (DIL_WM, 2 * DIL_WM), 1)
    dist = iq + DIL_WM - ik
    ok = jnp.logical_and(dist >= 0, dist <= DIL_WM)
    ok = jnp.logical_and(ok, jnp.logical_or(ik >= DIL_WM, n > 0))
    bias = jnp.where(ok, 0.0, -jnp.inf)
    for r in range(dil):
        def rows(ref, hs):
            if dil == 1:
                return ref[:, hs]
            return ref[pl.ds(r, DIL_WM, stride=dil), :]

        for h in range(heads):
            hs = slice(h * HEAD_DIM, (h + 1) * HEAD_DIM)
            q = rows(q_ref, hs).astype(BF16)
            k = jnp.concatenate([rows(kp_ref, hs), rows(kc_ref, hs)], axis=0).astype(BF16)
            v = jnp.concatenate([rows(vp_ref, hs), rows(vc_ref, hs)], axis=0).astype(BF16)
            s = lax.dot_general(q, k, _NT, preferred_element_type=F32) * HEAD_DIM ** -0.5 + bias
            m = jnp.max(s, axis=1, keepdims=True)
            p = jnp.exp(s - m)
            denom = jnp.sum(p, axis=1, keepdims=True)
            o = jnp.dot(p.astype(BF16), v, preferred_element_type=F32) / denom
            lse = jnp.broadcast_to(m + jnp.log(denom), (DIL_WM, HEAD_DIM))
            if dil == 1:
                o_ref[:, hs] = o
                lse_ref[:, hs] = lse
            else:
                o_ref[pl.ds(r, DIL_WM, stride=dil), :] = o
                lse_ref[pl.ds(r, DIL_WM, stride=dil), :] = lse


def dilated_prompt_attend(q, k, v_src, v_col0, window, dil):
    b, s_len, width = q.shape
    assert window // dil == DIL_WM and s_len % (DIL_WM * dil) == 0
    rows = DIL_WM * dil
    heads = C_HEADS if dil == 1 else 1
    bw = heads * HEAD_DIM
    assert v_col0 % bw == 0
    vb = v_col0 // bw
    cur = pl.BlockSpec((None, rows, bw), lambda bi, n, hi: (bi, n, hi))
    prev = pl.BlockSpec((None, rows, bw), lambda bi, n, hi: (bi, jnp.maximum(n - 1, 0), hi))
    vcur = pl.BlockSpec((None, rows, bw), lambda bi, n, hi: (bi, n, vb + hi))
    vprev = pl.BlockSpec((None, rows, bw), lambda bi, n, hi: (bi, jnp.maximum(n - 1, 0), vb + hi))
    return pl.pallas_call(
        functools.partial(_dilated_prompt_kernel, dil=dil, heads=heads),
        grid=(b, s_len // rows, width // bw),
        in_specs=[cur, prev, cur, vprev, vcur],
        out_specs=[cur, cur],
        out_shape=[jax.ShapeDtypeStruct(q.shape, F32)] * 2,
        compiler_params=pltpu.CompilerParams(dimension_semantics=("arbitrary",) * 3),
        name="dilated_prompt",
    )(q, k, k, v_src, v_src)


NORM_ROPE_ROWS = 256


def rope_tables(pos):
    half = ROT_DIM // 2
    freqs = ROPE_THETA ** (-jnp.arange(half, dtype=F32) / half)
    ang = pos.astype(F32)[:, None] * freqs[None, :]
    cos, sin = jnp.cos(ang), jnp.sin(ang)
    t = pos.shape[0]
    ones = jnp.ones((t, HEAD_DIM - ROT_DIM), F32)
    zeros = jnp.zeros((t, HEAD_DIM - ROT_DIM), F32)
    c = jnp.concatenate([cos, cos, ones], axis=1)
    s_dn = jnp.concatenate([-sin, jnp.zeros_like(sin), zeros], axis=1)
    s_up = jnp.concatenate([jnp.zeros_like(sin), sin, zeros], axis=1)
    return c, s_dn, s_up


def _norm_rope_kernel(x_ref, g_ref, b_ref, c_ref, sd_ref, su_ref, o_ref, *, heads, norm):
    half = ROT_DIM // 2
    c, sd, su = c_ref[...], sd_ref[...], su_ref[...]
    for h in range(heads):
        hs = slice(h * HEAD_DIM, (h + 1) * HEAD_DIM)
        x = x_ref[:, hs]
        if norm == "rms":
            x = x * lax.rsqrt(jnp.mean(x * x, axis=1, keepdims=True) + NORM_EPS) * g_ref[...]
        elif norm == "layer":
            xc = x - jnp.mean(x, axis=1, keepdims=True)
            x = xc * lax.rsqrt(jnp.mean(xc * xc, axis=1, keepdims=True) + NORM_EPS) * g_ref[...] + b_ref[...]
        dn = pltpu.roll(x, HEAD_DIM - half, axis=1)
        up = pltpu.roll(x, half, axis=1)
        o_ref[:, hs] = x * c + dn * sd + up * su


def norm_rope(x, col0, heads, gain, bias, tables, t_len, norm):
    r, _ = x.shape
    bw = heads * HEAD_DIM
    assert col0 % bw == 0
    if t_len % 8 == 0:
        rows = min(NORM_ROPE_ROWS, t_len)
    else:
        rows, tables = r, tuple(jnp.tile(t, (r // t_len, 1)) for t in tables)
        t_len = r
    assert t_len % rows == 0 and r % rows == 0
    tb = t_len // rows
    tab = pl.BlockSpec((rows, HEAD_DIM), lambda i: (i % tb, 0))
    vec = pl.BlockSpec((1, HEAD_DIM), lambda i: (0, 0))
    g = (jnp.ones((HEAD_DIM,), F32) if gain is None else gain).reshape(1, HEAD_DIM).astype(F32)
    b = (jnp.zeros((HEAD_DIM,), F32) if bias is None else bias).reshape(1, HEAD_DIM).astype(F32)
    return pl.pallas_call(
        functools.partial(_norm_rope_kernel, heads=heads, norm=norm),
        grid=(r // rows,),
        in_specs=[pl.BlockSpec((rows, bw), lambda i: (i, col0 // bw)), vec, vec, tab, tab, tab],
        out_specs=pl.BlockSpec((rows, bw), lambda i: (i, 0)),
        out_shape=jax.ShapeDtypeStruct((r, bw), F32),
        compiler_params=pltpu.CompilerParams(dimension_semantics=("arbitrary",)),
        name="norm_rope",
    )(x, g, b, *tables)


def _rms(x, eps=NORM_EPS):
    xf = x.astype(F32)
    return xf * lax.rsqrt(jnp.mean(xf * xf, axis=-1, keepdims=True) + eps)


def rms_norm(x, g):
    return (_rms(x) * g.astype(F32)).astype(x.dtype)


def rwkv7_mix(sh, prev_row, s0, mu, w0, w_lora, a0, a_lora, k_k, k_a, r_k, gn_g, gn_b):
    bn, t, _ = sh.shape
    prev = jnp.concatenate([prev_row[:, None, :].astype(sh.dtype), sh[:, :-1]], axis=1)
    xm = sh + (prev - sh) * mu
    r, k, v = (xm[..., j * A_WIDTH:(j + 1) * A_WIDTH] for j in range(3))
    xw = xm[..., 3 * A_WIDTH:3 * A_WIDTH + DECAY_LORA]
    xa = xm[..., 3 * A_WIDTH + DECAY_LORA:]
    wlog = -jax.nn.softplus(-(w0 + mm(jnp.tanh(xw), w_lora))) - 0.5
    decay = jnp.exp(-jnp.exp(wlog.astype(F32)))
    a = jax.nn.sigmoid((a0 + mm(xa, a_lora)).astype(F32))

    def heads(z):
        return z.astype(F32).reshape(bn, t, A_HEADS, A_HEAD)

    def head_sum(z):
        return jnp.broadcast_to(jnp.sum(heads(z), axis=-1, keepdims=True),
                                (bn, t, A_HEADS, A_HEAD)).reshape(bn, t, A_WIDTH)

    kk = heads(k * k_k)
    kk = kk / jnp.maximum(jnp.sqrt(jnp.sum(kk * kk, axis=-1, keepdims=True)), 1e-12)
    kk = kk.reshape(bn, t, A_WIDTH)
    k2 = k * (1.0 + (a - 1.0) * k_a.astype(F32))
    kka = kk * a
    q = decay * r - kk * head_sum(kka * r)
    vc = v * head_sum(k2 * r)

    def pair_major(z):
        return jnp.swapaxes(z.reshape(bn, t, A_WIDTH // V7X_LANES, V7X_LANES), 1, 2)

    vt = jnp.swapaxes(v, 1, 2)
    vt_hi = vt.astype(BF16)
    vt_lo = (vt - vt_hi.astype(F32)).astype(BF16)
    yt, s_fin = rwkv_scan(*[pair_major(z) for z in (-kk, decay, kka, k2, v, q, vc)], vt_hi, vt_lo,
                          s0.astype(F32))
    y = heads(jnp.swapaxes(yt, 1, 2))
    ym = jnp.mean(y, axis=-1, keepdims=True)
    yc = y - ym
    yn = yc * lax.rsqrt(jnp.mean(yc * yc, axis=-1, keepdims=True) + GN_EPS)
    yn = yn.reshape(bn, t, A_WIDTH) * gn_g.astype(F32) + gn_b.astype(F32)
    bonus = (jnp.sum(heads(r * k2) * r_k.astype(F32), axis=-1, keepdims=True) * heads(v)).reshape(bn, t, A_WIDTH)
    return (yn + bonus).astype(sh.dtype), s_fin.astype(s0.dtype), sh[:, -1]


KV_WIDTH = B_KV_HEADS * HEAD_DIM
E_COLS = {}
_acc = 0
for _name, _w, _pad in (('q', B_WIDTH, 0), ('g_a', A_WIDTH, 0), ('g_b', B_WIDTH, 0), ('rkv', 3 * A_WIDTH, 0),
                        ('k', KV_WIDTH, 0), ('v', KV_WIDTH, 0), ('cqi', IDX_Q_RANK, 0), ('kidx', IDX_DIM, 0),
                        ('wi', IDX_HEADS, V7X_LANES - IDX_HEADS),
                        ('lora', DECAY_LORA + AAA_LORA, 2 * V7X_LANES - DECAY_LORA - AAA_LORA)):
    E_COLS[_name] = (_acc, _w)
    _acc += _w + _pad
E_WIDTH = _acc


def repack_even_w_in(w):
    src = {'rkv': 0, 'lora': 3 * A_WIDTH, 'g_a': SHIFT_W, 'q': IN_A, 'k': IN_A + B_WIDTH,
           'v': IN_A + B_WIDTH + KV_WIDTH, 'cqi': IN_A + B_WIDTH + 2 * KV_WIDTH}
    src['kidx'] = src['cqi'] + IDX_Q_RANK
    src['wi'] = src['kidx'] + IDX_DIM
    src['g_b'] = src['wi'] + IDX_HEADS
    parts, pos = [], 0
    for name, (start, width) in E_COLS.items():
        if start > pos:
            parts.append(jnp.zeros((w.shape[0], start - pos), w.dtype))
        parts.append(w[:, src[name]:src[name] + width])
        pos = start + width
    if E_WIDTH > pos:
        parts.append(jnp.zeros((w.shape[0], E_WIDTH - pos), w.dtype))
    return jnp.concatenate(parts, axis=1)


def _cols(u, name):
    start, width = E_COLS[name]
    return u[:, start:start + width]


def even_mixer(xn, bn, t, pos, prev_row, s0, attend, ep):
    u = matmul(xn, ep['w_in_packed'])
    sh = jnp.concatenate([_cols(u, 'rkv'), _cols(u, 'lora')], axis=1).reshape(bn, t, SHIFT_W)
    y_a, s_fin, last_row = rwkv7_mix(sh, prev_row, s0, ep['shift_mu'], ep['w0'], ep['w_lora'],
                                     ep['a0'], ep['a_lora'], ep['k_k'], ep['k_a'], ep['r_k'],
                                     ep['gn_gain'], ep['gn_bias'])
    tabs = rope_tables(pos)
    q = norm_rope(u, E_COLS['q'][0], B_HEADS, ep['q_norm'], None, tabs, t, "rms")
    k = norm_rope(u, E_COLS['k'][0], B_KV_HEADS, ep['k_norm'], None, tabs, t, "rms")
    v = _cols(u, 'v')
    qi = norm_rope(matmul(rms_norm(_cols(u, 'cqi'), ep['qi_norm']).astype(BF16), ep['w_qi']),
                   0, IDX_HEADS, None, None, tabs, t, "none")
    kidx = norm_rope(u, E_COLS['kidx'][0], 1, ep['kidx_gain'], ep['kidx_bias'], tabs, t, "layer")
    wi = _cols(u, 'wi') * IDX_HEADS ** -0.5

    def seq(z):
        return z.reshape(bn, t, -1)

    y_b = attend(seq(q), seq(k), seq(v), seq(qi), seq(wi), seq(kidx))
    y = jnp.concatenate([y_a * jax.nn.silu(seq(_cols(u, 'g_a'))), y_b * jax.nn.silu(seq(_cols(u, 'g_b')))],
                        axis=-1)
    heads4 = (bn, t, B_KV_HEADS, HEAD_DIM)
    return mm(y, ep['w_out']), (s_fin, last_row, k.reshape(heads4), v.reshape(heads4), seq(kidx))


def dilated_sample(q, k, v, buf_k, buf_v, window, dil):
    t = q.shape[1]
    d = q.shape[-1]
    wb = buf_k.shape[1]
    wm = window // dil
    kc = jnp.concatenate([buf_k.astype(k.dtype), k], axis=1)
    vc = jnp.concatenate([buf_v.astype(v.dtype), v], axis=1)
    idx = wb + jnp.arange(t)[:, None] - jnp.arange(wm + 1)[None, :] * dil
    valid = idx >= 0
    idxc = jnp.maximum(idx, 0)
    kg, vg = kc[:, idxc], vc[:, idxc]
    s = jnp.einsum('bthd,btjhd->bthj', q, kg).astype(F32) * d ** -0.5
    s = jnp.where(valid[None, :, None, :], s, -jnp.inf)
    lse = jax.nn.logsumexp(s, axis=-1)
    p = jnp.exp(s - lse[..., None])
    o = jnp.einsum('bthj,btjhd->bthd', p.astype(vg.dtype), vg)
    return o, lse, kc[:, -wb:], vc[:, -wb:]


def odd_mixer(xn, bn, t, pos, sample_bufs, op):
    u = matmul(xn, op['w_in'])
    n_g = len(C_GROUPS)
    tabs = rope_tables(pos)
    heads4 = (bn, t, C_HEADS, HEAD_DIM)
    outs, lses, bufs = [], [], []
    for g, (win, dil) in enumerate(C_GROUPS):
        q = norm_rope(u, (3 * g) * C_WIDTH, C_HEADS, op['q_norm'][g], None, tabs, t, "rms")
        k = norm_rope(u, (3 * g + 1) * C_WIDTH, C_HEADS, op['k_norm'][g], None, tabs, t, "rms")
        v_col0 = (3 * g + 2) * C_WIDTH
        v = u[:, v_col0:v_col0 + C_WIDTH].reshape(heads4)
        if sample_bufs is None:
            o, lse = dilated_prompt_attend(q.reshape(bn, t, C_WIDTH), k.reshape(bn, t, C_WIDTH),
                                           u.reshape(bn, t, -1), v_col0, win, dil)
            keep = min(win, t)
            kb, vb = k.reshape(heads4)[:, -keep:], v[:, -keep:]
        else:
            o, lse, kb, vb = dilated_sample(q.reshape(heads4), k.reshape(heads4), v,
                                            sample_bufs[g][0], sample_bufs[g][1], win, dil)
            o = o.reshape(bn, t, C_WIDTH)
            lse = jnp.broadcast_to(lse[..., None], heads4).reshape(bn, t, C_WIDTH)
        outs.append(o)
        lses.append(lse)
        bufs += [kb, vb]
    alpha = jax.nn.softmax(jnp.stack(lses, axis=0), axis=0)
    o = jnp.sum(alpha * jnp.stack(outs, axis=0), axis=0)
    gate = u[:, 3 * n_g * C_WIDTH:].reshape(bn, t, C_WIDTH)
    return mm(o * jax.nn.silu(gate), op['w_out']), tuple(bufs)


def ple_add(h, p_l, w_proj, w_gate):
    gate = jax.nn.sigmoid(mm(_rms(h).astype(h.dtype), w_gate))
    return h + gate * mm(p_l, w_proj)


def kernel(x_prompt, x_sample, p_prompt, p_sample, state_wkv, state_shift, cache_k, cache_v,
           cache_kidx, page_table, cache_win_k0, cache_win_v0, cache_win_k1, cache_win_v1,
           cache_win_k2, cache_win_v2, ln_gain, e_w_in, e_shift_mu, e_w0, e_w_lora, e_a0,
           e_a_lora, e_k_k, e_k_a, e_r_k, e_gn_gain, e_gn_bias, e_q_norm, e_k_norm, e_qi_norm,
           e_w_qi, e_kidx_gain, e_kidx_bias, e_w_out, o_w_in, o_q_norm, o_k_norm, o_w_out,
           ple_w_proj, ple_w_gate):
    depth = ln_gain.shape[0]
    bp, s_len, _ = x_prompt.shape
    t_len = x_sample.shape[1]
    past = page_table.shape[1] * PAGE_SIZE
    pos_p = jnp.arange(s_len, dtype=jnp.int32)
    pos_s = past + jnp.arange(t_len, dtype=jnp.int32)
    bufs_k = (cache_win_k0, cache_win_k1, cache_win_k2)
    bufs_v = (cache_win_v0, cache_win_v1, cache_win_v2)
    hp, hs = x_prompt, x_sample
    ev_p, ev_s, od_p, od_s = [], [], [], []
    bs = x_sample.shape[0]
    for i in range(depth):
        l = i // 2
        xp = rms_norm(hp, ln_gain[i]).reshape(bp * s_len, -1).astype(BF16)
        xs = rms_norm(hs, ln_gain[i]).reshape(bs * t_len, -1).astype(BF16)
        if i % 2 == 0:
            ep = {'w_in_packed': repack_even_w_in(e_w_in[l]),
                  'shift_mu': e_shift_mu[l], 'w0': e_w0[l], 'w_lora': e_w_lora[l],
                  'a0': e_a0[l], 'a_lora': e_a_lora[l], 'k_k': e_k_k[l], 'k_a': e_k_a[l],
                  'r_k': e_r_k[l], 'gn_gain': e_gn_gain[l], 'gn_bias': e_gn_bias[l],
                  'q_norm': e_q_norm[l], 'k_norm': e_k_norm[l], 'qi_norm': e_qi_norm[l],
                  'w_qi': e_w_qi[l], 'kidx_gain': e_kidx_gain[l], 'kidx_bias': e_kidx_bias[l],
                  'w_out': e_w_out[l]}
            row0 = jnp.zeros((bp, SHIFT_W), hp.dtype)
            st0 = jnp.zeros((bp, A_HEADS, A_HEAD, A_HEAD), hp.dtype)
            mp, stp = even_mixer(xp, bp, s_len, pos_p, row0, st0, dsa_prompt_attend, ep)
            att_s = functools.partial(dsa_sample_attend, cache_k=cache_k, cache_v=cache_v,
                                      cache_kidx=cache_kidx, page_table=page_table, layer=l)
            ms, sts = even_mixer(xs, bs, t_len, pos_s, state_shift[l], state_wkv[l], att_s, ep)
            ev_p.append(stp)
            ev_s.append(sts)
        else:
            op = {'w_in': o_w_in[l], 'q_norm': o_q_norm[l], 'k_norm': o_k_norm[l], 'w_out': o_w_out[l]}
            mp, stp = odd_mixer(xp, bp, s_len, pos_p, None, op)
            sample_bufs = [(bk[l], bv[l]) for bk, bv in zip(bufs_k, bufs_v)]
            ms, sts = odd_mixer(xs, bs, t_len, pos_s, sample_bufs, op)
            od_p.append(stp)
            od_s.append(sts)
        hp = ple_add(hp + mp, p_prompt[i], ple_w_proj[i], ple_w_gate[i])
        hs = ple_add(hs + ms, p_sample[i], ple_w_proj[i], ple_w_gate[i])

    def st(lst, j):
        return jnp.stack([e[j] for e in lst], axis=0)

    outs = [hp, hs, st(ev_p, 0), st(ev_s, 0), st(ev_p, 1), st(ev_s, 1)]
    outs += [st(ev_p, j) for j in (2, 3, 4)] + [st(ev_s, j) for j in (2, 3, 4)]
    outs += [st(od_p, j) for j in range(6)] + [st(od_s, j) for j in range(6)]
    return tuple(outs)
```

```python
import functools

import jax
import jax.numpy as jnp
from jax import lax
from jax.experimental import pallas as pl
from jax.experimental.pallas import tpu as pltpu

F32 = jnp.float32
BF16 = jnp.bfloat16
I32 = jnp.int32

D_MODEL = 4096
PAGE_SIZE = 128
HEAD_DIM = 128
ROT_DIM = HEAD_DIM // 4
ROPE_THETA = 500000.0
NORM_EPS = 1e-6

A_WIDTH = D_MODEL // 2
A_HEAD = 64
A_HEADS = A_WIDTH // A_HEAD
DECAY_LORA = 96
AAA_LORA = 96
GN_EPS = 64e-5
SHIFT_W = 3 * A_WIDTH + DECAY_LORA + AAA_LORA

B_WIDTH = D_MODEL // 2
B_HEADS = B_WIDTH // HEAD_DIM
B_KV_HEADS = 4
IDX_HEADS = 16
IDX_DIM = 128
IDX_Q_RANK = 512
IDX_TOPK_MAX = 256
QBLOCK = 128

C_GROUPS = ((128, 1), (512, 4), (2048, 16))
C_HEADS = 16
C_WIDTH = C_HEADS * HEAD_DIM

IN_A = SHIFT_W + A_WIDTH
IN_B = B_WIDTH + 2 * B_KV_HEADS * HEAD_DIM + IDX_Q_RANK + IDX_DIM + IDX_HEADS + B_WIDTH

V7X_LANES = 128
V7X_VMEM_LIMIT_BYTES = 58 * 1024 * 1024
INT_MIN = -2 ** 31

_NT = (((1,), (1,)), ((), ()))


def _matmul_kernel(a_ref, b_ref, o_ref, bq_ref):
    @pl.when(pl.program_id(1) == 0)
    def _():
        bq_ref[...] = b_ref[...].astype(BF16)

    o_ref[...] = jnp.dot(a_ref[...].astype(BF16), bq_ref[...], preferred_element_type=F32)


def _pick_tile(n, cands):
    for c in cands:
        if n % c == 0:
            return c
    return n


def matmul(a, b):
    m, k = a.shape
    _, n = b.shape
    n_pad = -(-n // V7X_LANES) * V7X_LANES
    if n_pad != n:
        b = jnp.pad(b, ((0, 0), (0, n_pad - n)))
    tn = _pick_tile(n_pad, (1024, 512, 256, 128))
    tm = _pick_tile(m, (512, 256, 128, 64, 32, 16, 8))
    out = pl.pallas_call(
        _matmul_kernel,
        grid=(n_pad // tn, m // tm),
        in_specs=[pl.BlockSpec((tm, k), lambda j, i: (i, 0)),
                  pl.BlockSpec((k, tn), lambda j, i: (0, j))],
        out_specs=pl.BlockSpec((tm, tn), lambda j, i: (i, j)),
        out_shape=jax.ShapeDtypeStruct((m, n_pad), F32),
        scratch_shapes=[pltpu.VMEM((k, tn), BF16)],
        compiler_params=pltpu.CompilerParams(
            dimension_semantics=("arbitrary", "arbitrary"),
            vmem_limit_bytes=V7X_VMEM_LIMIT_BYTES),
        name="matmul",
    )(a, b)
    return out[:, :n] if n_pad != n else out


def mm(x, w):
    lead = x.shape[:-1]
    return matmul(x.reshape(-1, x.shape[-1]).astype(BF16), w).reshape(lead + (w.shape[-1],))


RWKV_PAIRS_PER_STEP = 8
RWKV_UNROLL = 4
RWKV_ROW_INPUTS = 7


def _split_bf16(x):
    hi = x.astype(BF16)
    lo = (x - hi.astype(F32)).astype(BF16)
    return jnp.concatenate([hi, lo], axis=1)


def _rwkv_scan_kernel(*refs, pairs, steps, use_mxu):
    rows = [refs[j * pairs:(j + 1) * pairs] for j in range(RWKV_ROW_INPUTS)]
    nkk_r, w_r, kka_r, k_r, v_r, q_r, vc_r = rows
    s0_ref, y_ref, sout_ref, s_scr, yt_scr, vt_scr = refs[RWKV_ROW_INPUTS * pairs:]
    tchunk = pl.program_id(2)

    @pl.when(tchunk == 0)
    def _():
        for p in range(pairs):
            s_scr[p] = jnp.concatenate([s0_ref[0, 2 * p], s0_ref[0, 2 * p + 1]], axis=1)

    lane = lax.broadcasted_iota(I32, (A_HEAD, V7X_LANES), 1)
    row = lax.broadcasted_iota(I32, (A_HEAD, V7X_LANES), 0)
    lo = lane < A_HEAD
    eye_lo = lane == row
    eye_hi = lane == row + A_HEAD
    eye = jnp.logical_or(eye_lo, eye_hi)
    lane_t = lax.broadcasted_iota(I32, (A_HEAD, steps), 1)
    yt_scr[...] = jnp.zeros(yt_scr.shape, F32)
    if use_mxu:
        kk_i = lax.broadcasted_iota(I32, (2 * V7X_LANES, 2 * V7X_LANES), 0)
        nn_i = lax.broadcasted_iota(I32, (2 * V7X_LANES, 2 * V7X_LANES), 1)
        seg_mat = jnp.where(((kk_i % V7X_LANES) >= A_HEAD) == (nn_i >= V7X_LANES), 1.0, 0.0).astype(BF16)
        oh_k = lax.broadcasted_iota(I32, (2 * V7X_LANES, V7X_LANES), 0) % V7X_LANES
        for p in range(pairs):
            vt_scr[p] = _split_bf16(v_r[p][...].T)

    def seg_sum(x):
        s_lo = jnp.sum(jnp.where(lo, x, 0.0), axis=1, keepdims=True)
        s_hi = jnp.sum(jnp.where(lo, 0.0, x), axis=1, keepdims=True)
        return s_lo, s_hi

    def step(t, carry):
        if use_mxu:
            onehot = jnp.where(oh_k == t, 1.0, 0.0).astype(BF16)
        for p in range(pairs):
            def rowvec(group):
                return jnp.broadcast_to(group[p][pl.ds(t, 1), :], (A_HEAD, V7X_LANES))

            s = s_scr[p]
            sa_lo, sa_hi = seg_sum(s * rowvec(nkk_r))
            py = s * rowvec(q_r) + jnp.where(eye, rowvec(vc_r), 0.0)
            r0 = p * V7X_LANES
            if use_mxu:
                yy = jnp.dot(_split_bf16(py), seg_mat, preferred_element_type=F32)
                y_lo, y_hi = yy[:, :V7X_LANES], yy[:, V7X_LANES:]
                vv = jnp.dot(vt_scr[p], onehot, preferred_element_type=F32)
                v_b = jnp.where(lo, vv[:A_HEAD], vv[A_HEAD:])
            else:
                y_lo, y_hi = seg_sum(py)
                vrow = rowvec(v_r)
                v_lo = jnp.sum(jnp.where(eye_lo, vrow, 0.0), axis=1, keepdims=True)
                v_hi = jnp.sum(jnp.where(eye_hi, vrow, 0.0), axis=1, keepdims=True)
                v_b = jnp.where(lo, v_lo, v_hi)
            sa_b = jnp.where(lo, sa_lo, sa_hi)
            s_scr[p] = s * rowvec(w_r) + sa_b * rowvec(kka_r) + v_b * rowvec(k_r)
            yt_scr[r0:r0 + A_HEAD, :] = jnp.where(lane_t == t, y_lo, yt_scr[r0:r0 + A_HEAD, :])
            yt_scr[r0 + A_HEAD:r0 + V7X_LANES, :] = jnp.where(
                lane_t == t, y_hi, yt_scr[r0 + A_HEAD:r0 + V7X_LANES, :])
        return carry

    lax.fori_loop(0, steps, step, 0, unroll=RWKV_UNROLL if steps % RWKV_UNROLL == 0 else 1)
    for p in range(pairs):
        y_ref[:, p * V7X_LANES:(p + 1) * V7X_LANES] = yt_scr[p * V7X_LANES:(p + 1) * V7X_LANES, :].T

    @pl.when(tchunk == pl.num_programs(2) - 1)
    def _():
        for p in range(pairs):
            s = s_scr[p]
            sout_ref[0, 2 * p] = s[:, :A_HEAD]
            sout_ref[0, 2 * p + 1] = s[:, A_HEAD:]


def rwkv_scan(row_inputs, s0):
    b, t, c = row_inputs[0].shape
    pairs = RWKV_PAIRS_PER_STEP
    npairs = c // V7X_LANES
    tc = min(t, V7X_LANES)
    assert t % tc == 0 and npairs % pairs == 0 and len(row_inputs) == RWKV_ROW_INPUTS
    bw = pairs * V7X_LANES

    def pair_spec(p):
        return pl.BlockSpec((None, tc, V7X_LANES), lambda bi, hi, ti: (bi, ti, hi * pairs + p))

    st_spec = pl.BlockSpec((1, 2 * pairs, A_HEAD, A_HEAD), lambda bi, hi, ti: (bi, hi, 0, 0))
    operands = [x for x in row_inputs for _ in range(pairs)]
    return pl.pallas_call(
        functools.partial(_rwkv_scan_kernel, pairs=pairs, steps=tc, use_mxu=(tc == V7X_LANES)),
        grid=(b, npairs // pairs, t // tc),
        in_specs=[pair_spec(p) for _ in range(RWKV_ROW_INPUTS) for p in range(pairs)] + [st_spec],
        out_specs=[pl.BlockSpec((None, tc, bw), lambda bi, hi, ti: (bi, ti, hi)), st_spec],
        out_shape=[jax.ShapeDtypeStruct((b, t, c), F32), jax.ShapeDtypeStruct(s0.shape, F32)],
        scratch_shapes=[pltpu.VMEM((pairs, A_HEAD, V7X_LANES), F32),
                        pltpu.VMEM((bw, tc), F32),
                        pltpu.VMEM((pairs, V7X_LANES, 2 * V7X_LANES), BF16)],
        compiler_params=pltpu.CompilerParams(dimension_semantics=("arbitrary", "arbitrary", "arbitrary")),
        name="rwkv_scan",
    )(*operands, s0)


def _select_topk_mask(score, allowed, topk):
    r, l = score.shape
    score = jnp.where(score == 0.0, 0.0, score)
    bits = pltpu.bitcast(score, I32)
    key = jnp.where(bits < 0, bits ^ jnp.int32(0x7FFFFFFF), bits)
    key = jnp.where(allowed, key, jnp.int32(INT_MIN))
    kf = jnp.float32(topk)

    def count(pred):
        return jnp.sum(jnp.where(pred, 1.0, 0.0), axis=1, keepdims=True)

    def bit_step(i, prefix):
        cand = prefix | lax.shift_left(jnp.int32(1), jnp.int32(31) - i)
        ok = count(key >= (cand ^ jnp.int32(INT_MIN))) >= kf
        return jnp.where(ok, cand, prefix)

    prefix = lax.fori_loop(0, 32, bit_step, jnp.zeros((r, 1), I32))
    thr = prefix ^ jnp.int32(INT_MIN)
    gt = key > thr
    eq = jnp.logical_and(key == thr, allowed)
    need = kf - count(gt)
    li = lax.broadcasted_iota(I32, (V7X_LANES, 2 * V7X_LANES), 0)
    lj = lax.broadcasted_iota(I32, (V7X_LANES, 2 * V7X_LANES), 1)
    tri_ones = jnp.where(jnp.logical_or(lj >= V7X_LANES, li < lj), 1.0, 0.0).astype(BF16)
    running = jnp.zeros((r, V7X_LANES), F32)
    sel = []
    for c in range(l // V7X_LANES):
        sl = slice(c * V7X_LANES, (c + 1) * V7X_LANES)
        eq_c = eq[:, sl]
        res = jnp.dot(jnp.where(eq_c, 1.0, 0.0).astype(BF16), tri_ones, preferred_element_type=F32)
        before = res[:, :V7X_LANES] + running
        running = running + res[:, V7X_LANES:]
        sel.append(jnp.logical_or(gt[:, sl], jnp.logical_and(eq_c, before < need)))
    return jnp.concatenate(sel, axis=1)


DSA_KEY_BUCKETS = 4


def _dsa_prompt_kernel(qi_ref, qil_ref, wi_ref, kidx_ref, kidxl_ref, q_ref, k_ref, v_ref, o_ref, *, topk):
    qb = pl.program_id(1)
    s_len = kidx_ref.shape[0]
    n_qb = s_len // QBLOCK
    group = B_HEADS // B_KV_HEADS

    def attend(l):
        kidx = kidx_ref[0:l, :]
        kidx_lo = kidxl_ref[0:l, :]
        score = jnp.zeros((QBLOCK, l), F32)
        for h in range(IDX_HEADS):
            hs = slice(h * IDX_DIM, (h + 1) * IDX_DIM)
            s = (lax.dot_general(qi_ref[:, hs], kidx, _NT, preferred_element_type=F32)
                 + lax.dot_general(qi_ref[:, hs], kidx_lo, _NT, preferred_element_type=F32)
                 + lax.dot_general(qil_ref[:, hs], kidx, _NT, preferred_element_type=F32))
            s = jnp.maximum(s * IDX_DIM ** -0.5, 0.0)
            score = score + s * wi_ref[:, h:h + 1]
        qpos = qb * QBLOCK + lax.broadcasted_iota(I32, (QBLOCK, l), 0)
        kpos = lax.broadcasted_iota(I32, (QBLOCK, l), 1)
        sel = _select_topk_mask(score, kpos <= qpos, topk)
        bias = jnp.where(sel, 0.0, -jnp.inf)
        for g in range(B_KV_HEADS):
            kg = k_ref[0:l, g * HEAD_DIM:(g + 1) * HEAD_DIM]
            vg = v_ref[0:l, g * HEAD_DIM:(g + 1) * HEAD_DIM]
            for j in range(group):
                h = g * group + j
                s = lax.dot_general(q_ref[:, h * HEAD_DIM:(h + 1) * HEAD_DIM], kg, _NT,
                                    preferred_element_type=F32)
                s = s * HEAD_DIM ** -0.5 + bias
                m = jnp.max(s, axis=1, keepdims=True)
                p = jnp.exp(s - m)
                denom = jnp.sum(p, axis=1, keepdims=True)
                o = jnp.dot(p.astype(BF16), vg, preferred_element_type=F32)
                o_ref[:, h * HEAD_DIM:(h + 1) * HEAD_DIM] = o / denom

    buckets = DSA_KEY_BUCKETS if n_qb % DSA_KEY_BUCKETS == 0 else 1
    per = n_qb // buckets
    for i in range(buckets):
        pl.when(jnp.logical_and(qb >= i * per, qb < (i + 1) * per))(
            functools.partial(attend, (i + 1) * per * QBLOCK))


def dsa_prompt_attend(q, k, v, qi, wi, kidx):
    b, s_len = q.shape[:2]
    topk = min(IDX_TOPK_MAX, s_len // 4)

    def flat16(z):
        return z.reshape(b, s_len, -1).astype(BF16)

    def flat16_lo(z):
        z = z.reshape(b, s_len, -1)
        return (z - z.astype(BF16).astype(F32)).astype(BF16)

    def qspec(w):
        return pl.BlockSpec((None, QBLOCK, w), lambda bi, qb: (bi, qb, 0))

    def kspec(w):
        return pl.BlockSpec((None, s_len, w), lambda bi, qb: (bi, 0, 0))

    return pl.pallas_call(
        functools.partial(_dsa_prompt_kernel, topk=topk),
        grid=(b, s_len // QBLOCK),
        in_specs=[qspec(B_WIDTH), qspec(B_WIDTH), qspec(IDX_HEADS), kspec(IDX_DIM), kspec(IDX_DIM),
                  qspec(B_WIDTH), kspec(B_KV_HEADS * HEAD_DIM), kspec(B_KV_HEADS * HEAD_DIM)],
        out_specs=qspec(B_WIDTH),
        out_shape=jax.ShapeDtypeStruct((b, s_len, B_WIDTH), F32),
        compiler_params=pltpu.CompilerParams(dimension_semantics=("arbitrary", "arbitrary"),
                                             vmem_limit_bytes=V7X_VMEM_LIMIT_BYTES),
        name="dsa_prompt",
    )(flat16(qi), flat16_lo(qi), wi, flat16(kidx), flat16_lo(kidx), flat16(q), flat16(k), flat16(v))


SAMPLE_PAGES_PER_STEP = 8
SAMPLE_T_PAD = 8


def _hi_lo(x):
    hi = x.astype(BF16)
    return hi, (x - hi.astype(F32)).astype(BF16)


def _dsa_sample_score_kernel(pt_ref, qi_ref, qil_ref, wi_ref, *refs, n_steps):
    j = pl.program_id(1)
    pages, new_ref, o_ref = refs[:-2], refs[-2], refs[-1]
    qh, ql, wcol = qi_ref[...], qil_ref[...], wi_ref[...]
    for i, pref in enumerate(pages):
        kidx = pref[...]
        if i == 0:
            kidx = jnp.where(j == n_steps - 1, new_ref[...], kidx)
        kh, kl = _hi_lo(kidx)
        s = (lax.dot_general(qh, kh, _NT, preferred_element_type=F32)
             + lax.dot_general(qh, kl, _NT, preferred_element_type=F32)
             + lax.dot_general(ql, kh, _NT, preferred_element_type=F32))
        s = jnp.maximum(s * IDX_DIM ** -0.5, 0.0) * wcol
        tot = s[0:SAMPLE_T_PAD]
        for h in range(1, IDX_HEADS):
            tot = tot + s[h * SAMPLE_T_PAD:(h + 1) * SAMPLE_T_PAD]
        o_ref[:, i * PAGE_SIZE:(i + 1) * PAGE_SIZE] = tot


def _dsa_sample_attn_kernel(pt_ref, score_ref, q_ref, *refs, n_steps, topk, past, t_len):
    pps = SAMPLE_PAGES_PER_STEP
    k_pages, v_pages = refs[:pps], refs[pps:2 * pps]
    kn_ref, vn_ref, o_ref, bias_scr, m_scr, l_scr, acc_scr = refs[2 * pps:]
    j = pl.program_id(1)
    width = score_ref.shape[1]
    group = B_HEADS // B_KV_HEADS

    @pl.when(j == 0)
    def _():
        qpos = past + lax.broadcasted_iota(I32, (SAMPLE_T_PAD, width), 0)
        kpos = lax.broadcasted_iota(I32, (SAMPLE_T_PAD, width), 1)
        sel = _select_topk_mask(score_ref[...], kpos <= qpos, topk)
        bias_scr[...] = jnp.where(sel, 0.0, -jnp.inf)
        m_scr[...] = jnp.full(m_scr.shape, -jnp.inf, F32)
        l_scr[...] = jnp.zeros(l_scr.shape, F32)
        acc_scr[...] = jnp.zeros(acc_scr.shape, F32)

    last = j == n_steps - 1
    col = pl.multiple_of(j * (pps * PAGE_SIZE), pps * PAGE_SIZE)
    b4 = bias_scr[0:t_len, pl.ds(col, pps * PAGE_SIZE)]
    bias = jnp.concatenate([b4] * group, axis=0)
    for g in range(B_KV_HEADS):
        def head_rows(pages, new_ref):
            first = jnp.where(last, new_ref[:, g, :], pages[0][:, g, :])
            return jnp.concatenate([first] + [r[:, g, :] for r in pages[1:]], axis=0).astype(BF16)

        kg = head_rows(k_pages, kn_ref)
        vg = head_rows(v_pages, vn_ref)
        s = lax.dot_general(q_ref[g], kg, _NT, preferred_element_type=F32) * HEAD_DIM ** -0.5 + bias
        m_old = m_scr[g]
        m_new = jnp.maximum(m_old, jnp.max(s, axis=1, keepdims=True))
        m_safe = jnp.where(m_new == -jnp.inf, 0.0, m_new)
        alpha = jnp.exp(m_old - m_safe)
        p = jnp.exp(s - m_safe)
        l_scr[g] = alpha * l_scr[g] + jnp.sum(p, axis=1, keepdims=True)
        acc_scr[g] = alpha * acc_scr[g] + jnp.dot(p.astype(BF16), vg, preferred_element_type=F32)
        m_scr[g] = m_new

    @pl.when(last)
    def _():
        for g in range(B_KV_HEADS):
            o = acc_scr[g] / l_scr[g]
            for hq in range(group):
                h = g * group + hq
                o_ref[:, h * HEAD_DIM:(h + 1) * HEAD_DIM] = o[hq * t_len:(hq + 1) * t_len]


def dsa_sample_attend(q, k, v, qi, wi, kidx, cache_k, cache_v, cache_kidx, page_table, layer):
    b, t = q.shape[:2]
    q = q.reshape(b, t, B_HEADS, HEAD_DIM)
    k = k.reshape(b, t, B_KV_HEADS, HEAD_DIM)
    v = v.reshape(b, t, B_KV_HEADS, HEAD_DIM)
    qi = qi.reshape(b, t, IDX_HEADS, IDX_DIM)
    n_pages = page_table.shape[1]
    past = n_pages * PAGE_SIZE
    topk = min(IDX_TOPK_MAX, (past + t) // 4)
    pps = SAMPLE_PAGES_PER_STEP
    assert n_pages % pps == 0 and t <= SAMPLE_T_PAD
    n_steps = n_pages // pps + 1
    width = n_steps * pps * PAGE_SIZE

    qi_r = jnp.pad(jnp.swapaxes(qi, 1, 2), ((0, 0), (0, 0), (0, SAMPLE_T_PAD - t), (0, 0)))
    qi_hi, qi_lo = _hi_lo(qi_r.reshape(b, IDX_HEADS * SAMPLE_T_PAD, IDX_DIM))
    wi_r = jnp.pad(jnp.swapaxes(wi, 1, 2), ((0, 0), (0, 0), (0, SAMPLE_T_PAD - t)))
    wi_r = wi_r.reshape(b, IDX_HEADS * SAMPLE_T_PAD, 1)
    kidx_new = jnp.pad(kidx, ((0, 0), (0, PAGE_SIZE - t), (0, 0)))

    def page_spec(i, *tail):
        def imap(bi, j, pt):
            return (layer, pt[bi, jnp.minimum(j * pps + i, n_pages - 1)]) + (0,) * (1 + len(tail))
        return pl.BlockSpec((None, None, PAGE_SIZE) + tail, imap)

    def per_b(shape):
        return pl.BlockSpec((None,) + shape, lambda bi, j, pt: (bi,) + (0,) * len(shape))

    score = pl.pallas_call(
        functools.partial(_dsa_sample_score_kernel, n_steps=n_steps),
        grid_spec=pltpu.PrefetchScalarGridSpec(
            num_scalar_prefetch=1, grid=(b, n_steps),
            in_specs=[per_b((IDX_HEADS * SAMPLE_T_PAD, IDX_DIM)), per_b((IDX_HEADS * SAMPLE_T_PAD, IDX_DIM)),
                      per_b((IDX_HEADS * SAMPLE_T_PAD, 1))]
                     + [page_spec(i, IDX_DIM) for i in range(pps)] + [per_b((PAGE_SIZE, IDX_DIM))],
            out_specs=pl.BlockSpec((None, SAMPLE_T_PAD, pps * PAGE_SIZE), lambda bi, j, pt: (bi, 0, j))),
        out_shape=jax.ShapeDtypeStruct((b, SAMPLE_T_PAD, width), F32),
        compiler_params=pltpu.CompilerParams(dimension_semantics=("arbitrary", "arbitrary")),
        name="dsa_sample_score",
    )(page_table, qi_hi, qi_lo, wi_r, *([cache_kidx] * pps), kidx_new)

    group = B_HEADS // B_KV_HEADS
    q_r = q.reshape(b, t, B_KV_HEADS, group, HEAD_DIM)
    q_r = jnp.transpose(q_r, (0, 2, 3, 1, 4)).reshape(b, B_KV_HEADS, group * t, HEAD_DIM).astype(BF16)
    k_new = jnp.pad(k, ((0, 0), (0, PAGE_SIZE - t), (0, 0), (0, 0)))
    v_new = jnp.pad(v, ((0, 0), (0, PAGE_SIZE - t), (0, 0), (0, 0)))
    return pl.pallas_call(
        functools.partial(_dsa_sample_attn_kernel, n_steps=n_steps, topk=topk, past=past, t_len=t),
        grid_spec=pltpu.PrefetchScalarGridSpec(
            num_scalar_prefetch=1, grid=(b, n_steps),
            in_specs=[per_b((SAMPLE_T_PAD, width)), per_b((B_KV_HEADS, group * t, HEAD_DIM))]
                     + [page_spec(i, B_KV_HEADS, HEAD_DIM) for i in range(pps)] * 2
                     + [per_b((PAGE_SIZE, B_KV_HEADS, HEAD_DIM))] * 2,
            out_specs=pl.BlockSpec((None, t, B_HEADS * HEAD_DIM), lambda bi, j, pt: (bi, 0, 0)),
            scratch_shapes=[pltpu.VMEM((SAMPLE_T_PAD, width), F32),
                            pltpu.VMEM((B_KV_HEADS, group * t, 1), F32),
                            pltpu.VMEM((B_KV_HEADS, group * t, 1), F32),
                            pltpu.VMEM((B_KV_HEADS, group * t, HEAD_DIM), F32)]),
        out_shape=jax.ShapeDtypeStruct((b, t, B_HEADS * HEAD_DIM), F32),
        compiler_params=pltpu.CompilerParams(dimension_semantics=("arbitrary", "arbitrary")),
        name="dsa_sample_attn",
    )(page_table, score, q_r, *([cache_k] * pps), *([cache_v] * pps), k_new, v_new)


DIL_WM = 128


def _dilated_prompt_kernel(q_ref, kp_ref, kc_ref, vp_ref, vc_ref, o_ref, lse_ref, *, dil, heads):
    n = pl.program_id(1)
    iq = lax.broadcasted_iota(I32, (DIL_WM, 2 * DIL_WM), 0)
    ik = lax.broadcasted_iota(I32, (DIL_WM, 2 * DIL_WM), 1)
    dist = iq + DIL_WM - ik
    ok = jnp.logical_and(dist >= 0, dist <= DIL_WM)
    ok = jnp.logical_and(ok, jnp.logical_or(ik >= DIL_WM, n > 0))
    bias = jnp.where(ok, 0.0, -jnp.inf)
    for r in range(dil):
        def rows(ref, hs):
            if dil == 1:
                return ref[:, hs]
            return ref[pl.ds(r, DIL_WM, stride=dil), :]

        for h in range(heads):
            hs = slice(h * HEAD_DIM, (h + 1) * HEAD_DIM)
            q = rows(q_ref, hs).astype(BF16)
            k = jnp.concatenate([rows(kp_ref, hs), rows(kc_ref, hs)], axis=0).astype(BF16)
            v = jnp.concatenate([rows(vp_ref, hs), rows(vc_ref, hs)], axis=0).astype(BF16)
            s = lax.dot_general(q, k, _NT, preferred_element_type=F32) * HEAD_DIM ** -0.5 + bias
            m = jnp.max(s, axis=1, keepdims=True)
            p = jnp.exp(s - m)
            denom = jnp.sum(p, axis=1, keepdims=True)
            o = jnp.dot(p.astype(BF16), v, preferred_element_type=F32) / denom
            lse = jnp.broadcast_to(m + jnp.log(denom), (DIL_WM, HEAD_DIM))
            if dil == 1:
                o_ref[:, hs] = o
                lse_ref[:, hs] = lse
            else:
                o_ref[pl.ds(r, DIL_WM, stride=dil), :] = o
                lse_ref[pl.ds(r, DIL_WM, stride=dil), :] = lse


def dilated_prompt_attend(q, k, v_src, v_col0, window, dil):
    b, s_len, width = q.shape
    assert window // dil == DIL_WM and s_len % (DIL_WM * dil) == 0
    rows = DIL_WM * dil
    heads = C_HEADS if dil == 1 else 1
    bw = heads * HEAD_DIM
    assert v_col0 % bw == 0
    vb = v_col0 // bw
    cur = pl.BlockSpec((None, rows, bw), lambda bi, n, hi: (bi, n, hi))
    prev = pl.BlockSpec((None, rows, bw), lambda bi, n, hi: (bi, jnp.maximum(n - 1, 0), hi))
    vcur = pl.BlockSpec((None, rows, bw), lambda bi, n, hi: (bi, n, vb + hi))
    vprev = pl.BlockSpec((None, rows, bw), lambda bi, n, hi: (bi, jnp.maximum(n - 1, 0), vb + hi))
    return pl.pallas_call(
        functools.partial(_dilated_prompt_kernel, dil=dil, heads=heads),
        grid=(b, s_len // rows, width // bw),
        in_specs=[cur, prev, cur, vprev, vcur],
        out_specs=[cur, cur],
        out_shape=[jax.ShapeDtypeStruct(q.shape, F32)] * 2,
        compiler_params=pltpu.CompilerParams(dimension_semantics=("arbitrary",) * 3),
        name="dilated_prompt",
    )(q, k, k, v_src, v_src)


NORM_ROPE_ROWS = 256


def rope_tables(pos):
    half = ROT_DIM // 2
    freqs = ROPE_THETA ** (-jnp.arange(half, dtype=F32) / half)
    ang = pos.astype(F32)[:, None] * freqs[None, :]
    cos, sin = jnp.cos(ang), jnp.sin(ang)
    t = pos.shape[0]
    ones = jnp.ones((t, HEAD_DIM - ROT_DIM), F32)
    zeros = jnp.zeros((t, HEAD_DIM - ROT_DIM), F32)
    c = jnp.concatenate([cos, cos, ones], axis=1)
    s_dn = jnp.concatenate([-sin, jnp.zeros_like(sin), zeros], axis=1)
    s_up = jnp.concatenate([jnp.zeros_like(sin), sin, zeros], axis=1)
    return c, s_dn, s_up


def _norm_rope_kernel(x_ref, g_ref, b_ref, c_ref, sd_ref, su_ref, o_ref, *, heads, norm):
    half = ROT_DIM // 2
    c, sd, su = c_ref[...], sd_ref[...], su_ref[...]
    for h in range(heads):
        hs = slice(h * HEAD_DIM, (h + 1) * HEAD_DIM)
        x = x_ref[:, hs]
        if norm == "rms":
            x = x * lax.rsqrt(jnp.mean(x * x, axis=1, keepdims=True) + NORM_EPS) * g_ref[...]
        elif norm == "layer":
            xc = x - jnp.mean(x, axis=1, keepdims=True)
            x = xc * lax.rsqrt(jnp.mean(xc * xc, axis=1, keepdims=True) + NORM_EPS) * g_ref[...] + b_ref[...]
        dn = pltpu.roll(x, HEAD_DIM - half, axis=1)
        up = pltpu.roll(x, half, axis=1)
        o_ref[:, hs] = x * c + dn * sd + up * su


def norm_rope(x, col0, heads, gain, bias, tables, t_len, norm):
    r, _ = x.shape
    bw = heads * HEAD_DIM
    assert col0 % bw == 0
    if t_len % 8 == 0:
        rows = min(NORM_ROPE_ROWS, t_len)
    else:
        rows, tables = r, tuple(jnp.tile(t, (r // t_len, 1)) for t in tables)
        t_len = r
    assert t_len % rows == 0 and r % rows == 0
    tb = t_len // rows
    tab = pl.BlockSpec((rows, HEAD_DIM), lambda i: (i % tb, 0))
    vec = pl.BlockSpec((1, HEAD_DIM), lambda i: (0, 0))
    g = (jnp.ones((HEAD_DIM,), F32) if gain is None else gain).reshape(1, HEAD_DIM).astype(F32)
    b = (jnp.zeros((HEAD_DIM,), F32) if bias is None else bias).reshape(1, HEAD_DIM).astype(F32)
    return pl.pallas_call(
        functools.partial(_norm_rope_kernel, heads=heads, norm=norm),
        grid=(r // rows,),
        in_specs=[pl.BlockSpec((rows, bw), lambda i: (i, col0 // bw)), vec, vec, tab, tab, tab],
        out_specs=pl.BlockSpec((rows, bw), lambda i: (i, 0)),
        out_shape=jax.ShapeDtypeStruct((r, bw), F32),
        compiler_params=pltpu.CompilerParams(dimension_semantics=("arbitrary",)),
        name="norm_rope",
    )(x, g, b, *tables)


def _rms(x, eps=NORM_EPS):
    xf = x.astype(F32)
    return xf * lax.rsqrt(jnp.mean(xf * xf, axis=-1, keepdims=True) + eps)


def rms_norm(x, g):
    return (_rms(x) * g.astype(F32)).astype(x.dtype)


def rwkv7_mix(sh, prev_row, s0, mu, w0, w_lora, a0, a_lora, k_k, k_a, r_k, gn_g, gn_b):
    bn, t, _ = sh.shape
    prev = jnp.concatenate([prev_row[:, None, :].astype(sh.dtype), sh[:, :-1]], axis=1)
    xm = sh + (prev - sh) * mu
    r, k, v = (xm[..., j * A_WIDTH:(j + 1) * A_WIDTH] for j in range(3))
    xw = xm[..., 3 * A_WIDTH:3 * A_WIDTH + DECAY_LORA]
    xa = xm[..., 3 * A_WIDTH + DECAY_LORA:]
    wlog = -jax.nn.softplus(-(w0 + mm(jnp.tanh(xw), w_lora))) - 0.5
    decay = jnp.exp(-jnp.exp(wlog.astype(F32)))
    a = jax.nn.sigmoid((a0 + mm(xa, a_lora)).astype(F32))

    def heads(z):
        return z.astype(F32).reshape(bn, t, A_HEADS, A_HEAD)

    def head_sum(z):
        return jnp.broadcast_to(jnp.sum(heads(z), axis=-1, keepdims=True),
                                (bn, t, A_HEADS, A_HEAD)).reshape(bn, t, A_WIDTH)

    kk = heads(k * k_k)
    kk = kk / jnp.maximum(jnp.sqrt(jnp.sum(kk * kk, axis=-1, keepdims=True)), 1e-12)
    kk = kk.reshape(bn, t, A_WIDTH)
    k2 = k * (1.0 + (a - 1.0) * k_a.astype(F32))
    kka = kk * a
    q = decay * r - kk * head_sum(kka * r)
    vc = v * head_sum(k2 * r)

    y, s_fin = rwkv_scan((-kk, decay, kka, k2, v, q, vc), s0.astype(F32))
    y = heads(y)
    ym = jnp.mean(y, axis=-1, keepdims=True)
    yc = y - ym
    yn = yc * lax.rsqrt(jnp.mean(yc * yc, axis=-1, keepdims=True) + GN_EPS)
    yn = yn.reshape(bn, t, A_WIDTH) * gn_g.astype(F32) + gn_b.astype(F32)
    bonus = (jnp.sum(heads(r * k2) * r_k.astype(F32), axis=-1, keepdims=True) * heads(v)).reshape(bn, t, A_WIDTH)
    return (yn + bonus).astype(sh.dtype), s_fin.astype(s0.dtype), sh[:, -1]


KV_WIDTH = B_KV_HEADS * HEAD_DIM
E_COLS = {}
_acc = 0
for _name, _w, _pad in (('q', B_WIDTH, 0), ('g_a', A_WIDTH, 0), ('g_b', B_WIDTH, 0), ('rkv', 3 * A_WIDTH, 0),
                        ('k', KV_WIDTH, 0), ('v', KV_WIDTH, 0), ('cqi', IDX_Q_RANK, 0), ('kidx', IDX_DIM, 0),
                        ('wi', IDX_HEADS, V7X_LANES - IDX_HEADS),
                        ('lora', DECAY_LORA + AAA_LORA, 2 * V7X_LANES - DECAY_LORA - AAA_LORA)):
    E_COLS[_name] = (_acc, _w)
    _acc += _w + _pad
E_WIDTH = _acc


def repack_even_w_in(w):
    src = {'rkv': 0, 'lora': 3 * A_WIDTH, 'g_a': SHIFT_W, 'q': IN_A, 'k': IN_A + B_WIDTH,
           'v': IN_A + B_WIDTH + KV_WIDTH, 'cqi': IN_A + B_WIDTH + 2 * KV_WIDTH}
    src['kidx'] = src['cqi'] + IDX_Q_RANK
    src['wi'] = src['kidx'] + IDX_DIM
    src['g_b'] = src['wi'] + IDX_HEADS
    parts, pos = [], 0
    for name, (start, width) in E_COLS.items():
        if start > pos:
            parts.append(jnp.zeros((w.shape[0], start - pos), w.dtype))
        parts.append(w[:, src[name]:src[name] + width])
        pos = start + width
    if E_WIDTH > pos:
        parts.append(jnp.zeros((w.shape[0], E_WIDTH - pos), w.dtype))
    return jnp.concatenate(parts, axis=1)


def _cols(u, name):
    start, width = E_COLS[name]
    return u[:, start:start + width]


def even_mixer(xn, bn, t, pos, prev_row, s0, attend, ep):
    u = matmul(xn, ep['w_in_packed'])
    sh = jnp.concatenate([_cols(u, 'rkv'), _cols(u, 'lora')], axis=1).reshape(bn, t, SHIFT_W)
    y_a, s_fin, last_row = rwkv7_mix(sh, prev_row, s0, ep['shift_mu'], ep['w0'], ep['w_lora'],
                                     ep['a0'], ep['a_lora'], ep['k_k'], ep['k_a'], ep['r_k'],
                                     ep['gn_gain'], ep['gn_bias'])
    tabs = rope_tables(pos)
    q = norm_rope(u, E_COLS['q'][0], B_HEADS, ep['q_norm'], None, tabs, t, "rms")
    k = norm_rope(u, E_COLS['k'][0], B_KV_HEADS, ep['k_norm'], None, tabs, t, "rms")
    v = _cols(u, 'v')
    qi = norm_rope(matmul(rms_norm(_cols(u, 'cqi'), ep['qi_norm']).astype(BF16), ep['w_qi']),
                   0, IDX_HEADS, None, None, tabs, t, "none")
    kidx = norm_rope(u, E_COLS['kidx'][0], 1, ep['kidx_gain'], ep['kidx_bias'], tabs, t, "layer")
    wi = _cols(u, 'wi') * IDX_HEADS ** -0.5

    def seq(z):
        return z.reshape(bn, t, -1)

    y_b = attend(seq(q), seq(k), seq(v), seq(qi), seq(wi), seq(kidx))
    y = jnp.concatenate([y_a * jax.nn.silu(seq(_cols(u, 'g_a'))), y_b * jax.nn.silu(seq(_cols(u, 'g_b')))],
                        axis=-1)
    heads4 = (bn, t, B_KV_HEADS, HEAD_DIM)
    return mm(y, ep['w_out']), (s_fin, last_row, k.reshape(heads4), v.reshape(heads4), seq(kidx))


def dilated_sample(q, k, v, buf_k, buf_v, window, dil):
    t = q.shape[1]
    d = q.shape[-1]
    wb = buf_k.shape[1]
    wm = window // dil
    kc = jnp.concatenate([buf_k.astype(k.dtype), k], axis=1)
    vc = jnp.concatenate([buf_v.astype(v.dtype), v], axis=1)
    idx = wb + jnp.arange(t)[:, None] - jnp.arange(wm + 1)[None, :] * dil
    valid = idx >= 0
    idxc = jnp.maximum(idx, 0)
    kg, vg = kc[:, idxc], vc[:, idxc]
    s = jnp.einsum('bthd,btjhd->bthj', q, kg).astype(F32) * d ** -0.5
    s = jnp.where(valid[None, :, None, :], s, -jnp.inf)
    lse = jax.nn.logsumexp(s, axis=-1)
    p = jnp.exp(s - lse[..., None])
    o = jnp.einsum('bthj,btjhd->bthd', p.astype(vg.dtype), vg)
    return o, lse, kc[:, -wb:], vc[:, -wb:]


def odd_mixer(xn, bn, t, pos, sample_bufs, op):
    u = matmul(xn, op['w_in'])
    n_g = len(C_GROUPS)
    tabs = rope_tables(pos)
    heads4 = (bn, t, C_HEADS, HEAD_DIM)
    outs, lses, bufs = [], [], []
    for g, (win, dil) in enumerate(C_GROUPS):
        q = norm_rope(u, (3 * g) * C_WIDTH, C_HEADS, op['q_norm'][g], None, tabs, t, "rms")
        k = norm_rope(u, (3 * g + 1) * C_WIDTH, C_HEADS, op['k_norm'][g], None, tabs, t, "rms")
        v_col0 = (3 * g + 2) * C_WIDTH
        v = u[:, v_col0:v_col0 + C_WIDTH].reshape(heads4)
        if sample_bufs is None:
            o, lse = dilated_prompt_attend(q.reshape(bn, t, C_WIDTH), k.reshape(bn, t, C_WIDTH),
                                           u.reshape(bn, t, -1), v_col0, win, dil)
            keep = min(win, t)
            kb, vb = k.reshape(heads4)[:, -keep:], v[:, -keep:]
        else:
            o, lse, kb, vb = dilated_sample(q.reshape(heads4), k.reshape(heads4), v,
                                            sample_bufs[g][0], sample_bufs[g][1], win, dil)
            o = o.reshape(bn, t, C_WIDTH)
            lse = jnp.broadcast_to(lse[..., None], heads4).reshape(bn, t, C_WIDTH)
        outs.append(o)
        lses.append(lse)
        bufs += [kb, vb]
    alpha = jax.nn.softmax(jnp.stack(lses, axis=0), axis=0)
    o = jnp.sum(alpha * jnp.stack(outs, axis=0), axis=0)
    gate = u[:, 3 * n_g * C_WIDTH:].reshape(bn, t, C_WIDTH)
    return mm(o * jax.nn.silu(gate), op['w_out']), tuple(bufs)


def ple_add(h, p_l, w_proj, w_gate):
    gate = jax.nn.sigmoid(mm(_rms(h).astype(h.dtype), w_gate))
    return h + gate * mm(p_l, w_proj)


def kernel(x_prompt, x_sample, p_prompt, p_sample, state_wkv, state_shift, cache_k, cache_v,
           cache_kidx, page_table, cache_win_k0, cache_win_v0, cache_win_k1, cache_win_v1,
           cache_win_k2, cache_win_v2, ln_gain, e_w_in, e_shift_mu, e_w0, e_w_lora, e_a0,
           e_a_lora, e_k_k, e_k_a, e_r_k, e_gn_gain, e_gn_bias, e_q_norm, e_k_norm, e_qi_norm,
           e_w_qi, e_kidx_gain, e_kidx_bias, e_w_out, o_w_in, o_q_norm, o_k_norm, o_w_out,
           ple_w_proj, ple_w_gate):
    depth = ln_gain.shape[0]
    bp, s_len, _ = x_prompt.shape
    t_len = x_sample.shape[1]
    past = page_table.shape[1] * PAGE_SIZE
    pos_p = jnp.arange(s_len, dtype=jnp.int32)
    pos_s = past + jnp.arange(t_len, dtype=jnp.int32)
    bufs_k = (cache_win_k0, cache_win_k1, cache_win_k2)
    bufs_v = (cache_win_v0, cache_win_v1, cache_win_v2)
    hp, hs = x_prompt, x_sample
    ev_p, ev_s, od_p, od_s = [], [], [], []
    bs = x_sample.shape[0]
    for i in range(depth):
        l = i // 2
        xp = rms_norm(hp, ln_gain[i]).reshape(bp * s_len, -1).astype(BF16)
        xs = rms_norm(hs, ln_gain[i]).reshape(bs * t_len, -1).astype(BF16)
        if i % 2 == 0:
            ep = {'w_in_packed': repack_even_w_in(e_w_in[l]),
                  'shift_mu': e_shift_mu[l], 'w0': e_w0[l], 'w_lora': e_w_lora[l],
                  'a0': e_a0[l], 'a_lora': e_a_lora[l], 'k_k': e_k_k[l], 'k_a': e_k_a[l],
                  'r_k': e_r_k[l], 'gn_gain': e_gn_gain[l], 'gn_bias': e_gn_bias[l],
                  'q_norm': e_q_norm[l], 'k_norm': e_k_norm[l], 'qi_norm': e_qi_norm[l],
                  'w_qi': e_w_qi[l], 'kidx_gain': e_kidx_gain[l], 'kidx_bias': e_kidx_bias[l],
                  'w_out': e_w_out[l]}
            row0 = jnp.zeros((bp, SHIFT_W), hp.dtype)
            st0 = jnp.zeros((bp, A_HEADS, A_HEAD, A_HEAD), hp.dtype)
            mp, stp = even_mixer(xp, bp, s_len, pos_p, row0, st0, dsa_prompt_attend, ep)
            att_s = functools.partial(dsa_sample_attend, cache_k=cache_k, cache_v=cache_v,
                                      cache_kidx=cache_kidx, page_table=page_table, layer=l)
            ms, sts = even_mixer(xs, bs, t_len, pos_s, state_shift[l], state_wkv[l], att_s, ep)
            ev_p.append(stp)
            ev_s.append(sts)
        else:
            op = {'w_in': o_w_in[l], 'q_norm': o_q_norm[l], 'k_norm': o_k_norm[l], 'w_out': o_w_out[l]}
            mp, stp = odd_mixer(xp, bp, s_len, pos_p, None, op)
            sample_bufs = [(bk[l], bv[l]) for bk, bv in zip(bufs_k, bufs_v)]
            ms, sts = odd_mixer(xs, bs, t_len, pos_s, sample_bufs, op)
            od_p.append(stp)
            od_s.append(sts)
        hp = ple_add(hp + mp, p_prompt[i], ple_w_proj[i], ple_w_gate[i])
        hs = ple_add(hs + ms, p_sample[i], ple_w_proj[i], ple_w_gate[i])

    def st(lst, j):
        return jnp.stack([e[j] for e in lst], axis=0)

    outs = [hp, hs, st(ev_p, 0), st(ev_s, 0), st(ev_p, 1), st(ev_s, 1)]
    outs += [st(ev_p, j) for j in (2, 3, 4)] + [st(ev_s, j) for j in (2, 3, 4)]
    outs += [st(od_p, j) for j in range(6)] + [st(od_s, j) for j in range(6)]
    return tuple(outs)
```

```python
import functools

import jax
import jax.numpy as jnp
from jax import lax
from jax.experimental import pallas as pl
from jax.experimental.pallas import tpu as pltpu

F32 = jnp.float32
BF16 = jnp.bfloat16
I32 = jnp.int32

D_MODEL = 4096
PAGE_SIZE = 128
HEAD_DIM = 128
ROT_DIM = HEAD_DIM // 4
ROPE_THETA = 500000.0
NORM_EPS = 1e-6

A_WIDTH = D_MODEL // 2
A_HEAD = 64
A_HEADS = A_WIDTH // A_HEAD
DECAY_LORA = 96
AAA_LORA = 96
GN_EPS = 64e-5
SHIFT_W = 3 * A_WIDTH + DECAY_LORA + AAA_LORA

B_WIDTH = D_MODEL // 2
B_HEADS = B_WIDTH // HEAD_DIM
B_KV_HEADS = 4
IDX_HEADS = 16
IDX_DIM = 128
IDX_Q_RANK = 512
IDX_TOPK_MAX = 256
QBLOCK = 128

C_GROUPS = ((128, 1), (512, 4), (2048, 16))
C_HEADS = 16
C_WIDTH = C_HEADS * HEAD_DIM

IN_A = SHIFT_W + A_WIDTH
IN_B = B_WIDTH + 2 * B_KV_HEADS * HEAD_DIM + IDX_Q_RANK + IDX_DIM + IDX_HEADS + B_WIDTH

V7X_LANES = 128
V7X_VMEM_LIMIT_BYTES = 58 * 1024 * 1024
INT_MIN = -2 ** 31

_NT = (((1,), (1,)), ((), ()))


def _matmul_kernel(a_ref, b_ref, o_ref, bq_ref):
    @pl.when(pl.program_id(1) == 0)
    def _():
        bq_ref[...] = b_ref[...].astype(BF16)

    o_ref[...] = jnp.dot(a_ref[...].astype(BF16), bq_ref[...], preferred_element_type=F32)


def _pick_tile(n, cands):
    for c in cands:
        if n % c == 0:
            return c
    return n


def matmul(a, b):
    m, k = a.shape
    _, n = b.shape
    n_pad = -(-n // V7X_LANES) * V7X_LANES
    if n_pad != n:
        b = jnp.pad(b, ((0, 0), (0, n_pad - n)))
    tn = _pick_tile(n_pad, (1024, 512, 256, 128))
    tm = _pick_tile(m, (512, 256, 128, 64, 32, 16, 8))
    out = pl.pallas_call(
        _matmul_kernel,
        grid=(n_pad // tn, m // tm),
        in_specs=[pl.BlockSpec((tm, k), lambda j, i: (i, 0)),
                  pl.BlockSpec((k, tn), lambda j, i: (0, j))],
        out_specs=pl.BlockSpec((tm, tn), lambda j, i: (i, j)),
        out_shape=jax.ShapeDtypeStruct((m, n_pad), F32),
        scratch_shapes=[pltpu.VMEM((k, tn), BF16)],
        compiler_params=pltpu.CompilerParams(
            dimension_semantics=("arbitrary", "arbitrary"),
            vmem_limit_bytes=V7X_VMEM_LIMIT_BYTES),
        name="matmul",
    )(a, b)
    return out[:, :n] if n_pad != n else out


def mm(x, w):
    lead = x.shape[:-1]
    return matmul(x.reshape(-1, x.shape[-1]).astype(BF16), w).reshape(lead + (w.shape[-1],))


PLE_TM = 512
PLE_TN = 512


def _ple_kernel(a_ref, wg_ref, p_ref, wp_ref, h_ref, m_ref, o_ref, wgq_ref, wpq_ref):
    @pl.when(pl.program_id(1) == 0)
    def _():
        wgq_ref[...] = wg_ref[...].astype(BF16)
        wpq_ref[...] = wp_ref[...].astype(BF16)

    gate = jnp.dot(a_ref[...], wgq_ref[...], preferred_element_type=F32)
    proj = jnp.dot(p_ref[...], wpq_ref[...], preferred_element_type=F32)
    o_ref[...] = (h_ref[...] + m_ref[...]) + proj / (1.0 + jnp.exp(-gate))


def ple_update(a, w_gate, p, w_proj, h, m):
    r, k = a.shape
    n = w_gate.shape[1]
    pk = p.shape[1]
    tm = PLE_TM if r % PLE_TM == 0 else r
    tn = PLE_TN
    assert n % tn == 0 and r % tm == 0
    return pl.pallas_call(
        _ple_kernel,
        grid=(n // tn, r // tm),
        in_specs=[pl.BlockSpec((tm, k), lambda j, i: (i, 0)), pl.BlockSpec((k, tn), lambda j, i: (0, j)),
                  pl.BlockSpec((tm, pk), lambda j, i: (i, 0)), pl.BlockSpec((pk, tn), lambda j, i: (0, j)),
                  pl.BlockSpec((tm, tn), lambda j, i: (i, j)), pl.BlockSpec((tm, tn), lambda j, i: (i, j))],
        out_specs=pl.BlockSpec((tm, tn), lambda j, i: (i, j)),
        out_shape=jax.ShapeDtypeStruct((r, n), F32),
        scratch_shapes=[pltpu.VMEM((k, tn), BF16), pltpu.VMEM((pk, tn), BF16)],
        compiler_params=pltpu.CompilerParams(dimension_semantics=("arbitrary", "arbitrary"),
                                             vmem_limit_bytes=V7X_VMEM_LIMIT_BYTES),
        name="ple_update",
    )(a, w_gate, p, w_proj, h, m)


RWKV_PAIRS_PER_STEP = 8
RWKV_UNROLL = 4
RWKV_ROW_INPUTS = 7


def _split_bf16(x):
    hi = x.astype(BF16)
    lo = (x - hi.astype(F32)).astype(BF16)
    return jnp.concatenate([hi, lo], axis=1)


def _rwkv_scan_kernel(*refs, pairs, steps, use_mxu):
    rows = [refs[j * pairs:(j + 1) * pairs] for j in range(RWKV_ROW_INPUTS)]
    nkk_r, w_r, kka_r, k_r, v_r, q_r, vc_r = rows
    s0_ref, y_ref, sout_ref, s_scr, yt_scr = refs[RWKV_ROW_INPUTS * pairs:]
    tchunk = pl.program_id(2)

    @pl.when(tchunk == 0)
    def _():
        for p in range(pairs):
            s_scr[p] = jnp.concatenate([s0_ref[0, 2 * p], s0_ref[0, 2 * p + 1]], axis=1)

    lane = lax.broadcasted_iota(I32, (A_HEAD, V7X_LANES), 1)
    row = lax.broadcasted_iota(I32, (A_HEAD, V7X_LANES), 0)
    lo = lane < A_HEAD
    eye_lo = lane == row
    eye_hi = lane == row + A_HEAD
    eye = jnp.logical_or(eye_lo, eye_hi)
    lane_t = lax.broadcasted_iota(I32, (A_HEAD, steps), 1)
    yt_scr[...] = jnp.zeros(yt_scr.shape, F32)
    if use_mxu:
        kk_i = lax.broadcasted_iota(I32, (2 * V7X_LANES, 2 * V7X_LANES), 0)
        nn_i = lax.broadcasted_iota(I32, (2 * V7X_LANES, 2 * V7X_LANES), 1)
        seg_mat = jnp.where(((kk_i % V7X_LANES) >= A_HEAD) == (nn_i >= V7X_LANES), 1.0, 0.0).astype(BF16)
        eye_bf = jnp.where(eye, 1.0, 0.0).astype(BF16)

    def seg_sum(x):
        s_lo = jnp.sum(jnp.where(lo, x, 0.0), axis=1, keepdims=True)
        s_hi = jnp.sum(jnp.where(lo, 0.0, x), axis=1, keepdims=True)
        return s_lo, s_hi

    def step(t, carry):
        for p in range(pairs):
            def rowvec(group):
                return jnp.broadcast_to(group[p][pl.ds(t, 1), :], (A_HEAD, V7X_LANES))

            s = s_scr[p]
            sa_lo, sa_hi = seg_sum(s * rowvec(nkk_r))
            py = s * rowvec(q_r) + jnp.where(eye, rowvec(vc_r), 0.0)
            r0 = p * V7X_LANES
            if use_mxu:
                vrow = v_r[p][pl.ds(t, 1), :]
                vh = vrow.astype(BF16)
                vl = (vrow - vh.astype(F32)).astype(BF16)
                vd = jnp.concatenate([eye_bf * vh, eye_bf * vl], axis=1)
                lhs = jnp.concatenate([_split_bf16(py), vd], axis=0)
                yv = jnp.dot(lhs, seg_mat, preferred_element_type=F32)
                y_lo, y_hi = yv[:A_HEAD, :V7X_LANES], yv[:A_HEAD, V7X_LANES:]
                v_b = jnp.where(lo, yv[A_HEAD:, :V7X_LANES], yv[A_HEAD:, V7X_LANES:])
            else:
                y_lo, y_hi = seg_sum(py)
                vrow = rowvec(v_r)
                v_lo = jnp.sum(jnp.where(eye_lo, vrow, 0.0), axis=1, keepdims=True)
                v_hi = jnp.sum(jnp.where(eye_hi, vrow, 0.0), axis=1, keepdims=True)
                v_b = jnp.where(lo, v_lo, v_hi)
            sa_b = jnp.where(lo, sa_lo, sa_hi)
            s_scr[p] = s * rowvec(w_r) + sa_b * rowvec(kka_r) + v_b * rowvec(k_r)
            yt_scr[r0:r0 + A_HEAD, :] = jnp.where(lane_t == t, y_lo, yt_scr[r0:r0 + A_HEAD, :])
            yt_scr[r0 + A_HEAD:r0 + V7X_LANES, :] = jnp.where(
                lane_t == t, y_hi, yt_scr[r0 + A_HEAD:r0 + V7X_LANES, :])
        return carry

    lax.fori_loop(0, steps, step, 0, unroll=RWKV_UNROLL if steps % RWKV_UNROLL == 0 else 1)
    for p in range(pairs):
        y_ref[:, p * V7X_LANES:(p + 1) * V7X_LANES] = yt_scr[p * V7X_LANES:(p + 1) * V7X_LANES, :].T

    @pl.when(tchunk == pl.num_programs(2) - 1)
    def _():
        for p in range(pairs):
            s = s_scr[p]
            sout_ref[0, 2 * p] = s[:, :A_HEAD]
            sout_ref[0, 2 * p + 1] = s[:, A_HEAD:]


def rwkv_scan(row_inputs, s0):
    b, t, c = row_inputs[0].shape
    pairs = RWKV_PAIRS_PER_STEP
    npairs = c // V7X_LANES
    tc = min(t, V7X_LANES)
    assert t % tc == 0 and npairs % pairs == 0 and len(row_inputs) == RWKV_ROW_INPUTS
    bw = pairs * V7X_LANES

    def pair_spec(p):
        return pl.BlockSpec((None, tc, V7X_LANES), lambda bi, hi, ti: (bi, ti, hi * pairs + p))

    st_spec = pl.BlockSpec((1, 2 * pairs, A_HEAD, A_HEAD), lambda bi, hi, ti: (bi, hi, 0, 0))
    operands = [x for x in row_inputs for _ in range(pairs)]
    return pl.pallas_call(
        functools.partial(_rwkv_scan_kernel, pairs=pairs, steps=tc, use_mxu=(tc == V7X_LANES)),
        grid=(b, npairs // pairs, t // tc),
        in_specs=[pair_spec(p) for _ in range(RWKV_ROW_INPUTS) for p in range(pairs)] + [st_spec],
        out_specs=[pl.BlockSpec((None, tc, bw), lambda bi, hi, ti: (bi, ti, hi)), st_spec],
        out_shape=[jax.ShapeDtypeStruct((b, t, c), F32), jax.ShapeDtypeStruct(s0.shape, F32)],
        scratch_shapes=[pltpu.VMEM((pairs, A_HEAD, V7X_LANES), F32),
                        pltpu.VMEM((bw, tc), F32)],
        compiler_params=pltpu.CompilerParams(dimension_semantics=("arbitrary", "arbitrary", "arbitrary")),
        name="rwkv_scan",
    )(*operands, s0)


def _select_topk_mask(score, allowed, topk):
    r, l = score.shape
    score = jnp.where(score == 0.0, 0.0, score)
    bits = pltpu.bitcast(score, I32)
    key = jnp.where(bits < 0, bits ^ jnp.int32(0x7FFFFFFF), bits)
    key = jnp.where(allowed, key, jnp.int32(INT_MIN))
    kf = jnp.float32(topk)

    def count(pred):
        return jnp.sum(jnp.where(pred, 1.0, 0.0), axis=1, keepdims=True)

    def bit_step(i, prefix):
        cand = prefix | lax.shift_left(jnp.int32(1), jnp.int32(31) - i)
        ok = count(key >= (cand ^ jnp.int32(INT_MIN))) >= kf
        return jnp.where(ok, cand, prefix)

    prefix = lax.fori_loop(0, 32, bit_step, jnp.zeros((r, 1), I32))
    thr = prefix ^ jnp.int32(INT_MIN)
    gt = key > thr
    eq = jnp.logical_and(key == thr, allowed)
    need = kf - count(gt)
    li = lax.broadcasted_iota(I32, (V7X_LANES, 2 * V7X_LANES), 0)
    lj = lax.broadcasted_iota(I32, (V7X_LANES, 2 * V7X_LANES), 1)
    tri_ones = jnp.where(jnp.logical_or(lj >= V7X_LANES, li < lj), 1.0, 0.0).astype(BF16)
    running = jnp.zeros((r, V7X_LANES), F32)
    sel = []
    for c in range(l // V7X_LANES):
        sl = slice(c * V7X_LANES, (c + 1) * V7X_LANES)
        eq_c = eq[:, sl]
        res = jnp.dot(jnp.where(eq_c, 1.0, 0.0).astype(BF16), tri_ones, preferred_element_type=F32)
        before = res[:, :V7X_LANES] + running
        running = running + res[:, V7X_LANES:]
        sel.append(jnp.logical_or(gt[:, sl], jnp.logical_and(eq_c, before < need)))
    return jnp.concatenate(sel, axis=1)


DSA_KEY_BUCKETS = 4


def _dsa_prompt_kernel(qi_ref, qil_ref, wi_ref, kidx_ref, kidxl_ref, q_ref, k_ref, v_ref, o_ref, *, topk):
    qb = pl.program_id(1)
    s_len = kidx_ref.shape[0]
    n_qb = s_len // QBLOCK
    group = B_HEADS // B_KV_HEADS

    def attend(l):
        kidx = kidx_ref[0:l, :]
        kidx_lo = kidxl_ref[0:l, :]
        score = jnp.zeros((QBLOCK, l), F32)
        for h in range(IDX_HEADS):
            hs = slice(h * IDX_DIM, (h + 1) * IDX_DIM)
            s = (lax.dot_general(qi_ref[:, hs], kidx, _NT, preferred_element_type=F32)
                 + lax.dot_general(qi_ref[:, hs], kidx_lo, _NT, preferred_element_type=F32)
                 + lax.dot_general(qil_ref[:, hs], kidx, _NT, preferred_element_type=F32))
            s = jnp.maximum(s * IDX_DIM ** -0.5, 0.0)
            score = score + s * wi_ref[:, h:h + 1]
        qpos = qb * QBLOCK + lax.broadcasted_iota(I32, (QBLOCK, l), 0)
        kpos = lax.broadcasted_iota(I32, (QBLOCK, l), 1)
        sel = _select_topk_mask(score, kpos <= qpos, topk)
        bias = jnp.where(sel, 0.0, -jnp.inf)
        for g in range(B_KV_HEADS):
            kg = k_ref[0:l, g * HEAD_DIM:(g + 1) * HEAD_DIM]
            vg = v_ref[0:l, g * HEAD_DIM:(g + 1) * HEAD_DIM]
            for j in range(group):
                h = g * group + j
                s = lax.dot_general(q_ref[:, h * HEAD_DIM:(h + 1) * HEAD_DIM], kg, _NT,
                                    preferred_element_type=F32)
                s = s * HEAD_DIM ** -0.5 + bias
                m = jnp.max(s, axis=1, keepdims=True)
                p = jnp.exp(s - m)
                denom = jnp.sum(p, axis=1, keepdims=True)
                o = jnp.dot(p.astype(BF16), vg, preferred_element_type=F32)
                o_ref[:, h * HEAD_DIM:(h + 1) * HEAD_DIM] = o / denom

    buckets = DSA_KEY_BUCKETS if n_qb % DSA_KEY_BUCKETS == 0 else 1
    per = n_qb // buckets
    for i in range(buckets):
        pl.when(jnp.logical_and(qb >= i * per, qb < (i + 1) * per))(
            functools.partial(attend, (i + 1) * per * QBLOCK))


def dsa_prompt_attend(q, k, v, qi, wi, kidx):
    b, s_len = q.shape[:2]
    topk = min(IDX_TOPK_MAX, s_len // 4)

    def flat16(z):
        return z.reshape(b, s_len, -1).astype(BF16)

    def flat16_lo(z):
        z = z.reshape(b, s_len, -1)
        return (z - z.astype(BF16).astype(F32)).astype(BF16)

    def qspec(w):
        return pl.BlockSpec((None, QBLOCK, w), lambda bi, qb: (bi, qb, 0))

    def kspec(w):
        return pl.BlockSpec((None, s_len, w), lambda bi, qb: (bi, 0, 0))

    return pl.pallas_call(
        functools.partial(_dsa_prompt_kernel, topk=topk),
        grid=(b, s_len // QBLOCK),
        in_specs=[qspec(B_WIDTH), qspec(B_WIDTH), qspec(IDX_HEADS), kspec(IDX_DIM), kspec(IDX_DIM),
                  qspec(B_WIDTH), kspec(B_KV_HEADS * HEAD_DIM), kspec(B_KV_HEADS * HEAD_DIM)],
        out_specs=qspec(B_WIDTH),
        out_shape=jax.ShapeDtypeStruct((b, s_len, B_WIDTH), F32),
        compiler_params=pltpu.CompilerParams(dimension_semantics=("arbitrary", "arbitrary"),
                                             vmem_limit_bytes=V7X_VMEM_LIMIT_BYTES),
        name="dsa_prompt",
    )(flat16(qi), flat16_lo(qi), wi, flat16(kidx), flat16_lo(kidx), flat16(q), flat16(k), flat16(v))


SAMPLE_PAGES_PER_STEP = 8
SAMPLE_T_PAD = 8


def _hi_lo(x):
    hi = x.astype(BF16)
    return hi, (x - hi.astype(F32)).astype(BF16)


def _dsa_sample_score_kernel(pt_ref, qi_ref, qil_ref, wi_ref, *refs, n_steps):
    j = pl.program_id(1)
    pages, new_ref, o_ref = refs[:-2], refs[-2], refs[-1]
    qh, ql, wcol = qi_ref[...], qil_ref[...], wi_ref[...]
    for i, pref in enumerate(pages):
        kidx = pref[...]
        if i == 0:
            kidx = jnp.where(j == n_steps - 1, new_ref[...], kidx)
        kh, kl = _hi_lo(kidx)
        s = (lax.dot_general(qh, kh, _NT, preferred_element_type=F32)
             + lax.dot_general(qh, kl, _NT, preferred_element_type=F32)
             + lax.dot_general(ql, kh, _NT, preferred_element_type=F32))
        s = jnp.maximum(s * IDX_DIM ** -0.5, 0.0) * wcol
        tot = s[0:SAMPLE_T_PAD]
        for h in range(1, IDX_HEADS):
            tot = tot + s[h * SAMPLE_T_PAD:(h + 1) * SAMPLE_T_PAD]
        o_ref[:, i * PAGE_SIZE:(i + 1) * PAGE_SIZE] = tot


def _dsa_sample_attn_kernel(pt_ref, score_ref, q_ref, *refs, n_steps, topk, past, t_len):
    pps = SAMPLE_PAGES_PER_STEP
    k_pages, v_pages = refs[:pps], refs[pps:2 * pps]
    kn_ref, vn_ref, o_ref, bias_scr, m_scr, l_scr, acc_scr = refs[2 * pps:]
    j = pl.program_id(1)
    width = score_ref.shape[1]
    group = B_HEADS // B_KV_HEADS

    @pl.when(j == 0)
    def _():
        qpos = past + lax.broadcasted_iota(I32, (SAMPLE_T_PAD, width), 0)
        kpos = lax.broadcasted_iota(I32, (SAMPLE_T_PAD, width), 1)
        sel = _select_topk_mask(score_ref[...], kpos <= qpos, topk)
        bias_scr[...] = jnp.where(sel, 0.0, -jnp.inf)
        m_scr[...] = jnp.full(m_scr.shape, -jnp.inf, F32)
        l_scr[...] = jnp.zeros(l_scr.shape, F32)
        acc_scr[...] = jnp.zeros(acc_scr.shape, F32)

    last = j == n_steps - 1
    col = pl.multiple_of(j * (pps * PAGE_SIZE), pps * PAGE_SIZE)
    b4 = bias_scr[0:t_len, pl.ds(col, pps * PAGE_SIZE)]
    bias = jnp.concatenate([b4] * group, axis=0)
    for g in range(B_KV_HEADS):
        def head_rows(pages, new_ref):
            first = jnp.where(last, new_ref[:, g, :], pages[0][:, g, :])
            return jnp.concatenate([first] + [r[:, g, :] for r in pages[1:]], axis=0).astype(BF16)

        kg = head_rows(k_pages, kn_ref)
        vg = head_rows(v_pages, vn_ref)
        s = lax.dot_general(q_ref[g], kg, _NT, preferred_element_type=F32) * HEAD_DIM ** -0.5 + bias
        m_old = m_scr[g]
        m_new = jnp.maximum(m_old, jnp.max(s, axis=1, keepdims=True))
        m_safe = jnp.where(m_new == -jnp.inf, 0.0, m_new)
        alpha = jnp.exp(m_old - m_safe)
        p = jnp.exp(s - m_safe)
        l_scr[g] = alpha * l_scr[g] + jnp.sum(p, axis=1, keepdims=True)
        acc_scr[g] = alpha * acc_scr[g] + jnp.dot(p.astype(BF16), vg, preferred_element_type=F32)
        m_scr[g] = m_new

    @pl.when(last)
    def _():
        for g in range(B_KV_HEADS):
            o = acc_scr[g] / l_scr[g]
            for hq in range(group):
                h = g * group + hq
                o_ref[:, h * HEAD_DIM:(h + 1) * HEAD_DIM] = o[hq * t_len:(hq + 1) * t_len]


def dsa_sample_attend(q, k, v, qi, wi, kidx, cache_k, cache_v, cache_kidx, page_table, layer):
    b, t = q.shape[:2]
    q = q.reshape(b, t, B_HEADS, HEAD_DIM)
    k = k.reshape(b, t, B_KV_HEADS, HEAD_DIM)
    v = v.reshape(b, t, B_KV_HEADS, HEAD_DIM)
    qi = qi.reshape(b, t, IDX_HEADS, IDX_DIM)
    n_pages = page_table.shape[1]
    past = n_pages * PAGE_SIZE
    topk = min(IDX_TOPK_MAX, (past + t) // 4)
    pps = SAMPLE_PAGES_PER_STEP
    assert n_pages % pps == 0 and t <= SAMPLE_T_PAD
    n_steps = n_pages // pps + 1
    width = n_steps * pps * PAGE_SIZE

    qi_r = jnp.pad(jnp.swapaxes(qi, 1, 2), ((0, 0), (0, 0), (0, SAMPLE_T_PAD - t), (0, 0)))
    qi_hi, qi_lo = _hi_lo(qi_r.reshape(b, IDX_HEADS * SAMPLE_T_PAD, IDX_DIM))
    wi_r = jnp.pad(jnp.swapaxes(wi, 1, 2), ((0, 0), (0, 0), (0, SAMPLE_T_PAD - t)))
    wi_r = wi_r.reshape(b, IDX_HEADS * SAMPLE_T_PAD, 1)
    kidx_new = jnp.pad(kidx, ((0, 0), (0, PAGE_SIZE - t), (0, 0)))

    def page_spec(i, *tail):
        def imap(bi, j, pt):
            return (layer, pt[bi, jnp.minimum(j * pps + i, n_pages - 1)]) + (0,) * (1 + len(tail))
        return pl.BlockSpec((None, None, PAGE_SIZE) + tail, imap)

    def per_b(shape):
        return pl.BlockSpec((None,) + shape, lambda bi, j, pt: (bi,) + (0,) * len(shape))

    score = pl.pallas_call(
        functools.partial(_dsa_sample_score_kernel, n_steps=n_steps),
        grid_spec=pltpu.PrefetchScalarGridSpec(
            num_scalar_prefetch=1, grid=(b, n_steps),
            in_specs=[per_b((IDX_HEADS * SAMPLE_T_PAD, IDX_DIM)), per_b((IDX_HEADS * SAMPLE_T_PAD, IDX_DIM)),
                      per_b((IDX_HEADS * SAMPLE_T_PAD, 1))]
                     + [page_spec(i, IDX_DIM) for i in range(pps)] + [per_b((PAGE_SIZE, IDX_DIM))],
            out_specs=pl.BlockSpec((None, SAMPLE_T_PAD, pps * PAGE_SIZE), lambda bi, j, pt: (bi, 0, j))),
        out_shape=jax.ShapeDtypeStruct((b, SAMPLE_T_PAD, width), F32),
        compiler_params=pltpu.CompilerParams(dimension_semantics=("arbitrary", "arbitrary")),
        name="dsa_sample_score",
    )(page_table, qi_hi, qi_lo, wi_r, *([cache_kidx] * pps), kidx_new)

    group = B_HEADS // B_KV_HEADS
    q_r = q.reshape(b, t, B_KV_HEADS, group, HEAD_DIM)
    q_r = jnp.transpose(q_r, (0, 2, 3, 1, 4)).reshape(b, B_KV_HEADS, group * t, HEAD_DIM).astype(BF16)
    k_new = jnp.pad(k, ((0, 0), (0, PAGE_SIZE - t), (0, 0), (0, 0)))
    v_new = jnp.pad(v, ((0, 0), (0, PAGE_SIZE - t), (0, 0), (0, 0)))
    return pl.pallas_call(
        functools.partial(_dsa_sample_attn_kernel, n_steps=n_steps, topk=topk, past=past, t_len=t),
        grid_spec=pltpu.PrefetchScalarGridSpec(
            num_scalar_prefetch=1, grid=(b, n_steps),
            in_specs=[per_b((SAMPLE_T_PAD, width)), per_b((B_KV_HEADS, group * t, HEAD_DIM))]
                     + [page_spec(i, B_KV_HEADS, HEAD_DIM) for i in range(pps)] * 2
                     + [per_b((PAGE_SIZE, B_KV_HEADS, HEAD_DIM))] * 2,
            out_specs=pl.BlockSpec((None, t, B_HEADS * HEAD_DIM), lambda bi, j, pt: (bi, 0, 0)),
            scratch_shapes=[pltpu.VMEM((SAMPLE_T_PAD, width), F32),
                            pltpu.VMEM((B_KV_HEADS, group * t, 1), F32),
                            pltpu.VMEM((B_KV_HEADS, group * t, 1), F32),
                            pltpu.VMEM((B_KV_HEADS, group * t, HEAD_DIM), F32)]),
        out_shape=jax.ShapeDtypeStruct((b, t, B_HEADS * HEAD_DIM), F32),
        compiler_params=pltpu.CompilerParams(dimension_semantics=("arbitrary", "arbitrary")),
        name="dsa_sample_attn",
    )(page_table, score, q_r, *([cache_k] * pps), *([cache_v] * pps), k_new, v_new)


DIL_WM = 128


def _dilated_prompt_kernel(q_ref, kp_ref, kc_ref, vp_ref, vc_ref, o_ref, lse_ref, *, dil, heads):
    n = pl.program_id(1)
    iq = lax.broadcasted_iota(I32, (DIL_WM, 2 * DIL_WM), 0)
    ik = lax.broadcasted_iota(I32, (DIL_WM, 2 * DIL_WM), 1)
    dist = iq + DIL_WM - ik
    ok = jnp.logical_and(dist >= 0, dist <= DIL_WM)
    ok = jnp.logical_and(ok, jnp.logical_or(ik >= DIL_WM, n > 0))
    bias = jnp.where(ok, 0.0, -jnp.inf)
    for r in range(dil):
        def rows(ref, hs):
            if dil == 1:
                return ref[:, hs]
            return ref[pl.ds(r, DIL_WM, stride=dil), :]

        for h in range(heads):
            hs = slice(h * HEAD_DIM, (h + 1) * HEAD_DIM)
            q = rows(q_ref, hs).astype(BF16)
            k = jnp.concatenate([rows(kp_ref, hs), rows(kc_ref, hs)], axis=0).astype(BF16)
            v = jnp.concatenate([rows(vp_ref, hs), rows(vc_ref, hs)], axis=0).astype(BF16)
            s = lax.dot_general(q, k, _NT, preferred_element_type=F32) * HEAD_DIM ** -0.5 + bias
            m = jnp.max(s, axis=1, keepdims=True)
            p = jnp.exp(s - m)
            denom = jnp.sum(p, axis=1, keepdims=True)
            o = jnp.dot(p.astype(BF16), v, preferred_element_type=F32) / denom
            lse = jnp.broadcast_to(m + jnp.log(denom), (DIL_WM, HEAD_DIM))
            if dil == 1:
                o_ref[:, hs] = o
                lse_ref[:, hs] = lse
            else:
                o_ref[pl.ds(r, DIL_WM, stride=dil), :] = o
                lse_ref[pl.ds(r, DIL_WM, stride=dil), :] = lse


def dilated_prompt_attend(q, k, v_src, v_col0, window, dil):
    b, s_len, width = q.shape
    assert window // dil == DIL_WM and s_len % (DIL_WM * dil) == 0
    rows = DIL_WM * dil
    heads = C_HEADS if dil == 1 else 1
    bw = heads * HEAD_DIM
    assert v_col0 % bw == 0
    vb = v_col0 // bw
    cur = pl.BlockSpec((None, rows, bw), lambda bi, n, hi: (bi, n, hi))
    prev = pl.BlockSpec((None, rows, bw), lambda bi, n, hi: (bi, jnp.maximum(n - 1, 0), hi))
    vcur = pl.BlockSpec((None, rows, bw), lambda bi, n, hi: (bi, n, vb + hi))
    vprev = pl.BlockSpec((None, rows, bw), lambda bi, n, hi: (bi, jnp.maximum(n - 1, 0), vb + hi))
    return pl.pallas_call(
        functools.partial(_dilated_prompt_kernel, dil=dil, heads=heads),
        grid=(b, s_len // rows, width // bw),
        in_specs=[cur, prev, cur, vprev, vcur],
        out_specs=[cur, cur],
        out_shape=[jax.ShapeDtypeStruct(q.shape, F32)] * 2,
        compiler_params=pltpu.CompilerParams(dimension_semantics=("arbitrary",) * 3),
        name="dilated_prompt",
    )(q, k, k, v_src, v_src)


NORM_ROPE_ROWS = 256


def rope_tables(pos):
    half = ROT_DIM // 2
    freqs = ROPE_THETA ** (-jnp.arange(half, dtype=F32) / half)
    ang = pos.astype(F32)[:, None] * freqs[None, :]
    cos, sin = jnp.cos(ang), jnp.sin(ang)
    t = pos.shape[0]
    ones = jnp.ones((t, HEAD_DIM - ROT_DIM), F32)
    zeros = jnp.zeros((t, HEAD_DIM - ROT_DIM), F32)
    c = jnp.concatenate([cos, cos, ones], axis=1)
    s_dn = jnp.concatenate([-sin, jnp.zeros_like(sin), zeros], axis=1)
    s_up = jnp.concatenate([jnp.zeros_like(sin), sin, zeros], axis=1)
    return c, s_dn, s_up


def _norm_rope_kernel(x_ref, g_ref, b_ref, c_ref, sd_ref, su_ref, o_ref, *, heads, norm):
    half = ROT_DIM // 2
    c, sd, su = c_ref[...], sd_ref[...], su_ref[...]
    for h in range(heads):
        hs = slice(h * HEAD_DIM, (h + 1) * HEAD_DIM)
        x = x_ref[:, hs]
        if norm == "rms":
            x = x * lax.rsqrt(jnp.mean(x * x, axis=1, keepdims=True) + NORM_EPS) * g_ref[...]
        elif norm == "layer":
            xc = x - jnp.mean(x, axis=1, keepdims=True)
            x = xc * lax.rsqrt(jnp.mean(xc * xc, axis=1, keepdims=True) + NORM_EPS) * g_ref[...] + b_ref[...]
        dn = pltpu.roll(x, HEAD_DIM - half, axis=1)
        up = pltpu.roll(x, half, axis=1)
        o_ref[:, hs] = x * c + dn * sd + up * su


def norm_rope(x, col0, heads, gain, bias, tables, t_len, norm):
    r, _ = x.shape
    bw = heads * HEAD_DIM
    assert col0 % bw == 0
    if t_len % 8 == 0:
        rows = min(NORM_ROPE_ROWS, t_len)
    else:
        rows, tables = r, tuple(jnp.tile(t, (r // t_len, 1)) for t in tables)
        t_len = r
    assert t_len % rows == 0 and r % rows == 0
    tb = t_len // rows
    tab = pl.BlockSpec((rows, HEAD_DIM), lambda i: (i % tb, 0))
    vec = pl.BlockSpec((1, HEAD_DIM), lambda i: (0, 0))
    g = (jnp.ones((HEAD_DIM,), F32) if gain is None else gain).reshape(1, HEAD_DIM).astype(F32)
    b = (jnp.zeros((HEAD_DIM,), F32) if bias is None else bias).reshape(1, HEAD_DIM).astype(F32)
    return pl.pallas_call(
        functools.partial(_norm_rope_kernel, heads=heads, norm=norm),
        grid=(r // rows,),
        in_specs=[pl.BlockSpec((rows, bw), lambda i: (i, col0 // bw)), vec, vec, tab, tab, tab],
        out_specs=pl.BlockSpec((rows, bw), lambda i: (i, 0)),
        out_shape=jax.ShapeDtypeStruct((r, bw), F32),
        compiler_params=pltpu.CompilerParams(dimension_semantics=("arbitrary",)),
        name="norm_rope",
    )(x, g, b, *tables)


def _rms(x, eps=NORM_EPS):
    xf = x.astype(F32)
    return xf * lax.rsqrt(jnp.mean(xf * xf, axis=-1, keepdims=True) + eps)


def rms_norm(x, g):
    return (_rms(x) * g.astype(F32)).astype(x.dtype)


def rwkv7_mix(sh, prev_row, s0, mu, w0, w_lora, a0, a_lora, k_k, k_a, r_k, gn_g, gn_b):
    bn, t, _ = sh.shape
    prev = jnp.concatenate([prev_row[:, None, :].astype(sh.dtype), sh[:, :-1]], axis=1)
    xm = sh + (prev - sh) * mu
    r, k, v = (xm[..., j * A_WIDTH:(j + 1) * A_WIDTH] for j in range(3))
    xw = xm[..., 3 * A_WIDTH:3 * A_WIDTH + DECAY_LORA]
    xa = xm[..., 3 * A_WIDTH + DECAY_LORA:]
    wlog = -jax.nn.softplus(-(w0 + mm(jnp.tanh(xw), w_lora))) - 0.5
    decay = jnp.exp(-jnp.exp(wlog.astype(F32)))
    a = jax.nn.sigmoid((a0 + mm(xa, a_lora)).astype(F32))

    def heads(z):
        return z.astype(F32).reshape(bn, t, A_HEADS, A_HEAD)

    def head_sum(z):
        return jnp.broadcast_to(jnp.sum(heads(z), axis=-1, keepdims=True),
                                (bn, t, A_HEADS, A_HEAD)).reshape(bn, t, A_WIDTH)

    kk = heads(k * k_k)
    kk = kk / jnp.maximum(jnp.sqrt(jnp.sum(kk * kk, axis=-1, keepdims=True)), 1e-12)
    kk = kk.reshape(bn, t, A_WIDTH)
    k2 = k * (1.0 + (a - 1.0) * k_a.astype(F32))
    kka = kk * a
    q = decay * r - kk * head_sum(kka * r)
    vc = v * head_sum(k2 * r)

    y, s_fin = rwkv_scan((-kk, decay, kka, k2, v, q, vc), s0.astype(F32))
    y = heads(y)
    ym = jnp.mean(y, axis=-1, keepdims=True)
    yc = y - ym
    yn = yc * lax.rsqrt(jnp.mean(yc * yc, axis=-1, keepdims=True) + GN_EPS)
    yn = yn.reshape(bn, t, A_WIDTH) * gn_g.astype(F32) + gn_b.astype(F32)
    bonus = (jnp.sum(heads(r * k2) * r_k.astype(F32), axis=-1, keepdims=True) * heads(v)).reshape(bn, t, A_WIDTH)
    return (yn + bonus).astype(sh.dtype), s_fin.astype(s0.dtype), sh[:, -1]


KV_WIDTH = B_KV_HEADS * HEAD_DIM
E_COLS = {}
_acc = 0
for _name, _w, _pad in (('q', B_WIDTH, 0), ('g_a', A_WIDTH, 0), ('g_b', B_WIDTH, 0), ('rkv', 3 * A_WIDTH, 0),
                        ('k', KV_WIDTH, 0), ('v', KV_WIDTH, 0), ('cqi', IDX_Q_RANK, 0), ('kidx', IDX_DIM, 0),
                        ('wi', IDX_HEADS, V7X_LANES - IDX_HEADS),
                        ('lora', DECAY_LORA + AAA_LORA, 2 * V7X_LANES - DECAY_LORA - AAA_LORA)):
    E_COLS[_name] = (_acc, _w)
    _acc += _w + _pad
E_WIDTH = _acc


def repack_even_w_in(w_all, layer):
    d_in = w_all.shape[1]
    src = {'rkv': 0, 'lora': 3 * A_WIDTH, 'g_a': SHIFT_W, 'q': IN_A, 'k': IN_A + B_WIDTH,
           'v': IN_A + B_WIDTH + KV_WIDTH, 'cqi': IN_A + B_WIDTH + 2 * KV_WIDTH}
    src['kidx'] = src['cqi'] + IDX_Q_RANK
    src['wi'] = src['kidx'] + IDX_DIM
    src['g_b'] = src['wi'] + IDX_HEADS
    parts, pos = [], 0
    for name, (start, width) in E_COLS.items():
        if start > pos:
            parts.append(jnp.zeros((d_in, start - pos), w_all.dtype))
        parts.append(w_all[layer, :, src[name]:src[name] + width])
        pos = start + width
    if E_WIDTH > pos:
        parts.append(jnp.zeros((d_in, E_WIDTH - pos), w_all.dtype))
    return jnp.concatenate(parts, axis=1)


def _cols(u, name):
    start, width = E_COLS[name]
    return u[:, start:start + width]


def even_mixer(xn, bn, t, pos, prev_row, s0, attend, ep):
    u = matmul(xn, ep['w_in_packed'])
    sh = jnp.concatenate([_cols(u, 'rkv'), _cols(u, 'lora')], axis=1).reshape(bn, t, SHIFT_W)
    y_a, s_fin, last_row = rwkv7_mix(sh, prev_row, s0, ep['shift_mu'], ep['w0'], ep['w_lora'],
                                     ep['a0'], ep['a_lora'], ep['k_k'], ep['k_a'], ep['r_k'],
                                     ep['gn_gain'], ep['gn_bias'])
    tabs = rope_tables(pos)
    q = norm_rope(u, E_COLS['q'][0], B_HEADS, ep['q_norm'], None, tabs, t, "rms")
    k = norm_rope(u, E_COLS['k'][0], B_KV_HEADS, ep['k_norm'], None, tabs, t, "rms")
    v = _cols(u, 'v')
    qi = norm_rope(matmul(rms_norm(_cols(u, 'cqi'), ep['qi_norm']).astype(BF16), ep['w_qi']),
                   0, IDX_HEADS, None, None, tabs, t, "none")
    kidx = norm_rope(u, E_COLS['kidx'][0], 1, ep['kidx_gain'], ep['kidx_bias'], tabs, t, "layer")
    wi = _cols(u, 'wi') * IDX_HEADS ** -0.5

    def seq(z):
        return z.reshape(bn, t, -1)

    y_b = attend(seq(q), seq(k), seq(v), seq(qi), seq(wi), seq(kidx))
    y = jnp.concatenate([y_a * jax.nn.silu(seq(_cols(u, 'g_a'))), y_b * jax.nn.silu(seq(_cols(u, 'g_b')))],
                        axis=-1)
    heads4 = (bn, t, B_KV_HEADS, HEAD_DIM)
    return mm(y, ep['w_out']), (s_fin, last_row, k.reshape(heads4), v.reshape(heads4), seq(kidx))


def dilated_sample(q, k, v, buf_k, buf_v, window, dil):
    t = q.shape[1]
    d = q.shape[-1]
    wb = buf_k.shape[1]
    wm = window // dil
    kc = jnp.concatenate([buf_k.astype(k.dtype), k], axis=1)
    vc = jnp.concatenate([buf_v.astype(v.dtype), v], axis=1)
    idx = wb + jnp.arange(t)[:, None] - jnp.arange(wm + 1)[None, :] * dil
    valid = idx >= 0
    idxc = jnp.maximum(idx, 0)
    kg, vg = kc[:, idxc], vc[:, idxc]
    s = jnp.einsum('bthd,btjhd->bthj', q, kg).astype(F32) * d ** -0.5
    s = jnp.where(valid[None, :, None, :], s, -jnp.inf)
    lse = jax.nn.logsumexp(s, axis=-1)
    p = jnp.exp(s - lse[..., None])
    o = jnp.einsum('bthj,btjhd->bthd', p.astype(vg.dtype), vg)
    return o, lse, kc[:, -wb:], vc[:, -wb:]


def odd_mixer(xn, bn, t, pos, sample_bufs, op):
    u = matmul(xn, op['w_in'])
    n_g = len(C_GROUPS)
    tabs = rope_tables(pos)
    heads4 = (bn, t, C_HEADS, HEAD_DIM)
    outs, lses, bufs = [], [], []
    for g, (win, dil) in enumerate(C_GROUPS):
        q = norm_rope(u, (3 * g) * C_WIDTH, C_HEADS, op['q_norm'][g], None, tabs, t, "rms")
        k = norm_rope(u, (3 * g + 1) * C_WIDTH, C_HEADS, op['k_norm'][g], None, tabs, t, "rms")
        v_col0 = (3 * g + 2) * C_WIDTH
        v = u[:, v_col0:v_col0 + C_WIDTH].reshape(heads4)
        if sample_bufs is None:
            o, lse = dilated_prompt_attend(q.reshape(bn, t, C_WIDTH), k.reshape(bn, t, C_WIDTH),
                                           u.reshape(bn, t, -1), v_col0, win, dil)
            keep = min(win, t)
            kb, vb = k.reshape(heads4)[:, -keep:], v[:, -keep:]
        else:
            o, lse, kb, vb = dilated_sample(q.reshape(heads4), k.reshape(heads4), v,
                                            sample_bufs[g][0], sample_bufs[g][1], win, dil)
            o = o.reshape(bn, t, C_WIDTH)
            lse = jnp.broadcast_to(lse[..., None], heads4).reshape(bn, t, C_WIDTH)
        outs.append(o)
        lses.append(lse)
        bufs += [kb, vb]
    alpha = jax.nn.softmax(jnp.stack(lses, axis=0), axis=0)
    o = jnp.sum(alpha * jnp.stack(outs, axis=0), axis=0)
    gate = u[:, 3 * n_g * C_WIDTH:].reshape(bn, t, C_WIDTH)
    return mm(o * jax.nn.silu(gate), op['w_out']), tuple(bufs)


def ple_add(h, m, p_l, w_proj, w_gate):
    shape = h.shape
    a = _rms(h + m).reshape(-1, shape[-1]).astype(BF16)
    out = ple_update(a, w_gate, p_l.reshape(-1, p_l.shape[-1]).astype(BF16), w_proj,
                     h.reshape(-1, shape[-1]), m.reshape(-1, shape[-1]))
    return out.reshape(shape)


def kernel(x_prompt, x_sample, p_prompt, p_sample, state_wkv, state_shift, cache_k, cache_v,
           cache_kidx, page_table, cache_win_k0, cache_win_v0, cache_win_k1, cache_win_v1,
           cache_win_k2, cache_win_v2, ln_gain, e_w_in, e_shift_mu, e_w0, e_w_lora, e_a0,
           e_a_lora, e_k_k, e_k_a, e_r_k, e_gn_gain, e_gn_bias, e_q_norm, e_k_norm, e_qi_norm,
           e_w_qi, e_kidx_gain, e_kidx_bias, e_w_out, o_w_in, o_q_norm, o_k_norm, o_w_out,
           ple_w_proj, ple_w_gate):
    depth = ln_gain.shape[0]
    bp, s_len, _ = x_prompt.shape
    t_len = x_sample.shape[1]
    past = page_table.shape[1] * PAGE_SIZE
    pos_p = jnp.arange(s_len, dtype=jnp.int32)
    pos_s = past + jnp.arange(t_len, dtype=jnp.int32)
    bufs_k = (cache_win_k0, cache_win_k1, cache_win_k2)
    bufs_v = (cache_win_v0, cache_win_v1, cache_win_v2)
    hp, hs = x_prompt, x_sample
    ev_p, ev_s, od_p, od_s = [], [], [], []
    bs = x_sample.shape[0]
    for i in range(depth):
        l = i // 2
        xp = rms_norm(hp, ln_gain[i]).reshape(bp * s_len, -1).astype(BF16)
        xs = rms_norm(hs, ln_gain[i]).reshape(bs * t_len, -1).astype(BF16)
        if i % 2 == 0:
            ep = {'w_in_packed': repack_even_w_in(e_w_in, l),
                  'shift_mu': e_shift_mu[l], 'w0': e_w0[l], 'w_lora': e_w_lora[l],
                  'a0': e_a0[l], 'a_lora': e_a_lora[l], 'k_k': e_k_k[l], 'k_a': e_k_a[l],
                  'r_k': e_r_k[l], 'gn_gain': e_gn_gain[l], 'gn_bias': e_gn_bias[l],
                  'q_norm': e_q_norm[l], 'k_norm': e_k_norm[l], 'qi_norm': e_qi_norm[l],
                  'w_qi': e_w_qi[l], 'kidx_gain': e_kidx_gain[l], 'kidx_bias': e_kidx_bias[l],
                  'w_out': e_w_out[l]}
            row0 = jnp.zeros((bp, SHIFT_W), hp.dtype)
            st0 = jnp.zeros((bp, A_HEADS, A_HEAD, A_HEAD), hp.dtype)
            mp, stp = even_mixer(xp, bp, s_len, pos_p, row0, st0, dsa_prompt_attend, ep)
            att_s = functools.partial(dsa_sample_attend, cache_k=cache_k, cache_v=cache_v,
                                      cache_kidx=cache_kidx, page_table=page_table, layer=l)
            ms, sts = even_mixer(xs, bs, t_len, pos_s, state_shift[l], state_wkv[l], att_s, ep)
            ev_p.append(stp)
            ev_s.append(sts)
        else:
            op = {'w_in': o_w_in[l], 'q_norm': o_q_norm[l], 'k_norm': o_k_norm[l], 'w_out': o_w_out[l]}
            mp, stp = odd_mixer(xp, bp, s_len, pos_p, None, op)
            sample_bufs = [(bk[l], bv[l]) for bk, bv in zip(bufs_k, bufs_v)]
            ms, sts = odd_mixer(xs, bs, t_len, pos_s, sample_bufs, op)
            od_p.append(stp)
            od_s.append(sts)
        hp = ple_add(hp, mp, p_prompt[i], ple_w_proj[i], ple_w_gate[i])
        hs = ple_add(hs, ms, p_sample[i], ple_w_proj[i], ple_w_gate[i])

    def st(lst, j):
        if len(lst) == 1:
            return lst[0][j][None]
        return jnp.stack([e[j] for e in lst], axis=0)

    outs = [hp, hs, st(ev_p, 0), st(ev_s, 0), st(ev_p, 1), st(ev_s, 1)]
    outs += [st(ev_p, j) for j in (2, 3, 4)] + [st(ev_s, j) for j in (2, 3, 4)]
    outs += [st(od_p, j) for j in range(6)] + [st(od_s, j) for j in range(6)]
    return tuple(outs)
```

```python
import functools

import jax
import jax.numpy as jnp
from jax import lax
from jax.experimental import pallas as pl
from jax.experimental.pallas import tpu as pltpu

F32 = jnp.float32
BF16 = jnp.bfloat16
I32 = jnp.int32

D_MODEL = 4096
PAGE_SIZE = 128
HEAD_DIM = 128
ROT_DIM = HEAD_DIM // 4
ROPE_THETA = 500000.0
NORM_EPS = 1e-6

A_WIDTH = D_MODEL // 2
A_HEAD = 64
A_HEADS = A_WIDTH // A_HEAD
DECAY_LORA = 96
AAA_LORA = 96
GN_EPS = 64e-5
SHIFT_W = 3 * A_WIDTH + DECAY_LORA + AAA_LORA

B_WIDTH = D_MODEL // 2
B_HEADS = B_WIDTH // HEAD_DIM
B_KV_HEADS = 4
IDX_HEADS = 16
IDX_DIM = 128
IDX_Q_RANK = 512
IDX_TOPK_MAX = 256
QBLOCK = 128

C_GROUPS = ((128, 1), (512, 4), (2048, 16))
C_HEADS = 16
C_WIDTH = C_HEADS * HEAD_DIM

IN_A = SHIFT_W + A_WIDTH
IN_B = B_WIDTH + 2 * B_KV_HEADS * HEAD_DIM + IDX_Q_RANK + IDX_DIM + IDX_HEADS + B_WIDTH

V7X_LANES = 128
V7X_VMEM_LIMIT_BYTES = 58 * 1024 * 1024
INT_MIN = -2 ** 31

_NT = (((1,), (1,)), ((), ()))


def _matmul_kernel(a_ref, b_ref, o_ref, bq_ref):
    @pl.when(pl.program_id(1) == 0)
    def _():
        bq_ref[...] = b_ref[...].astype(BF16)

    o_ref[...] = jnp.dot(a_ref[...].astype(BF16), bq_ref[...], preferred_element_type=F32)


def _pick_tile(n, cands):
    for c in cands:
        if n % c == 0:
            return c
    return n


def matmul(a, b):
    m, k = a.shape
    _, n = b.shape
    n_pad = -(-n // V7X_LANES) * V7X_LANES
    if n_pad != n:
        b = jnp.pad(b, ((0, 0), (0, n_pad - n)))
    tn = _pick_tile(n_pad, (1024, 512, 256, 128))
    tm = _pick_tile(m, (512, 256, 128, 64, 32, 16, 8))
    out = pl.pallas_call(
        _matmul_kernel,
        grid=(n_pad // tn, m // tm),
        in_specs=[pl.BlockSpec((tm, k), lambda j, i: (i, 0)),
                  pl.BlockSpec((k, tn), lambda j, i: (0, j))],
        out_specs=pl.BlockSpec((tm, tn), lambda j, i: (i, j)),
        out_shape=jax.ShapeDtypeStruct((m, n_pad), F32),
        scratch_shapes=[pltpu.VMEM((k, tn), BF16)],
        compiler_params=pltpu.CompilerParams(
            dimension_semantics=("arbitrary", "arbitrary"),
            vmem_limit_bytes=V7X_VMEM_LIMIT_BYTES),
        name="matmul",
    )(a, b)
    return out[:, :n] if n_pad != n else out


def mm(x, w):
    lead = x.shape[:-1]
    return matmul(x.reshape(-1, x.shape[-1]).astype(BF16), w).reshape(lead + (w.shape[-1],))


def _matmul2_kernel(a_ref, as_ref, b_ref, o_ref, os_ref, bq_ref, *, m_tiles):
    i = pl.program_id(1)

    @pl.when(i == 0)
    def _():
        bq_ref[...] = b_ref[...].astype(BF16)

    @pl.when(i < m_tiles)
    def _():
        o_ref[...] = jnp.dot(a_ref[...], bq_ref[...], preferred_element_type=F32)

    @pl.when(i == m_tiles)
    def _():
        os_ref[...] = jnp.dot(as_ref[...], bq_ref[...], preferred_element_type=F32)


def matmul2(a, a_s, b):
    m, k = a.shape
    ms = a_s.shape[0]
    n = b.shape[1]
    tn = _pick_tile(n, (1024, 512, 256, 128))
    tm = _pick_tile(m, (512, 256, 128, 64, 32, 16, 8))
    assert n % tn == 0 and m % tm == 0
    m_tiles = m // tm

    def big(j, i):
        return jnp.minimum(i, m_tiles - 1)

    return pl.pallas_call(
        functools.partial(_matmul2_kernel, m_tiles=m_tiles),
        grid=(n // tn, m_tiles + 1),
        in_specs=[pl.BlockSpec((tm, k), lambda j, i: (big(j, i), 0)),
                  pl.BlockSpec((ms, k), lambda j, i: (0, 0)),
                  pl.BlockSpec((k, tn), lambda j, i: (0, j))],
        out_specs=[pl.BlockSpec((tm, tn), lambda j, i: (big(j, i), j)),
                   pl.BlockSpec((ms, tn), lambda j, i: (0, j))],
        out_shape=[jax.ShapeDtypeStruct((m, n), F32), jax.ShapeDtypeStruct((ms, n), F32)],
        scratch_shapes=[pltpu.VMEM((k, tn), BF16)],
        compiler_params=pltpu.CompilerParams(
            dimension_semantics=("arbitrary", "arbitrary"),
            vmem_limit_bytes=V7X_VMEM_LIMIT_BYTES),
        name="matmul2",
    )(a, a_s, b)


PLE_TM = 512
PLE_TN = 512


def _ple_kernel(wg_ref, wp_ref, a_ref, p_ref, h_ref, m_ref, as_ref, ps_ref, hs_ref, ms_ref,
                o_ref, os_ref, wgq_ref, wpq_ref, *, m_tiles):
    i = pl.program_id(1)

    @pl.when(i == 0)
    def _():
        wgq_ref[...] = wg_ref[...].astype(BF16)
        wpq_ref[...] = wp_ref[...].astype(BF16)

    def update(a, p, h, m, o):
        gate = jnp.dot(a[...], wgq_ref[...], preferred_element_type=F32)
        proj = jnp.dot(p[...], wpq_ref[...], preferred_element_type=F32)
        o[...] = (h[...] + m[...]) + proj / (1.0 + jnp.exp(-gate))

    pl.when(i < m_tiles)(functools.partial(update, a_ref, p_ref, h_ref, m_ref, o_ref))
    pl.when(i == m_tiles)(functools.partial(update, as_ref, ps_ref, hs_ref, ms_ref, os_ref))


def ple_update(w_gate, w_proj, big, small):
    a, p, h, m = big
    a_s, p_s, h_s, m_s = small
    r, k = a.shape
    rs = a_s.shape[0]
    n = w_gate.shape[1]
    pk = p.shape[1]
    tm, tn = PLE_TM, PLE_TN
    assert n % tn == 0 and r % tm == 0
    m_tiles = r // tm

    def rows(width, full):
        if full:
            return pl.BlockSpec((tm, width), lambda j, i: (jnp.minimum(i, m_tiles - 1), 0))
        return pl.BlockSpec((tm, tn), lambda j, i: (jnp.minimum(i, m_tiles - 1), j))

    def rows_s(width, full):
        if full:
            return pl.BlockSpec((rs, width), lambda j, i: (0, 0))
        return pl.BlockSpec((rs, tn), lambda j, i: (0, j))

    return pl.pallas_call(
        functools.partial(_ple_kernel, m_tiles=m_tiles),
        grid=(n // tn, m_tiles + 1),
        in_specs=[pl.BlockSpec((k, tn), lambda j, i: (0, j)), pl.BlockSpec((pk, tn), lambda j, i: (0, j)),
                  rows(k, True), rows(pk, True), rows(tn, False), rows(tn, False),
                  rows_s(k, True), rows_s(pk, True), rows_s(tn, False), rows_s(tn, False)],
        out_specs=[rows(tn, False), rows_s(tn, False)],
        out_shape=[jax.ShapeDtypeStruct((r, n), F32), jax.ShapeDtypeStruct((rs, n), F32)],
        scratch_shapes=[pltpu.VMEM((k, tn), BF16), pltpu.VMEM((pk, tn), BF16)],
        compiler_params=pltpu.CompilerParams(dimension_semantics=("arbitrary", "arbitrary"),
                                             vmem_limit_bytes=V7X_VMEM_LIMIT_BYTES),
        name="ple_update",
    )(w_gate, w_proj, a, p, h, m, a_s, p_s, h_s, m_s)


RWKV_PAIRS_PER_STEP = 8
RWKV_UNROLL = 4
RWKV_ROW_INPUTS = 7


def _split_bf16(x):
    hi = x.astype(BF16)
    lo = (x - hi.astype(F32)).astype(BF16)
    return jnp.concatenate([hi, lo], axis=1)


def _rwkv_scan_kernel(*refs, pairs, steps, use_mxu):
    rows = [refs[j * pairs:(j + 1) * pairs] for j in range(RWKV_ROW_INPUTS)]
    nkk_r, w_r, kka_r, k_r, v_r, q_r, vc_r = rows
    s0_ref, y_ref, sout_ref, s_scr, yt_scr = refs[RWKV_ROW_INPUTS * pairs:]
    tchunk = pl.program_id(2)

    @pl.when(tchunk == 0)
    def _():
        for p in range(pairs):
            s_scr[p] = jnp.concatenate([s0_ref[0, 2 * p], s0_ref[0, 2 * p + 1]], axis=1)

    lane = lax.broadcasted_iota(I32, (A_HEAD, V7X_LANES), 1)
    row = lax.broadcasted_iota(I32, (A_HEAD, V7X_LANES), 0)
    lo = lane < A_HEAD
    eye_lo = lane == row
    eye_hi = lane == row + A_HEAD
    eye = jnp.logical_or(eye_lo, eye_hi)
    lane_t = lax.broadcasted_iota(I32, (A_HEAD, steps), 1)
    yt_scr[...] = jnp.zeros(yt_scr.shape, F32)
    if use_mxu:
        kk_i = lax.broadcasted_iota(I32, (2 * V7X_LANES, 2 * V7X_LANES), 0)
        nn_i = lax.broadcasted_iota(I32, (2 * V7X_LANES, 2 * V7X_LANES), 1)
        seg_mat = jnp.where(((kk_i % V7X_LANES) >= A_HEAD) == (nn_i >= V7X_LANES), 1.0, 0.0).astype(BF16)
        eye_bf = jnp.where(eye, 1.0, 0.0).astype(BF16)

    def seg_sum(x):
        s_lo = jnp.sum(jnp.where(lo, x, 0.0), axis=1, keepdims=True)
        s_hi = jnp.sum(jnp.where(lo, 0.0, x), axis=1, keepdims=True)
        return s_lo, s_hi

    def step(t, carry):
        for p in range(pairs):
            def rowvec(group):
                return jnp.broadcast_to(group[p][pl.ds(t, 1), :], (A_HEAD, V7X_LANES))

            s = s_scr[p]
            sa_lo, sa_hi = seg_sum(s * rowvec(nkk_r))
            py = s * rowvec(q_r) + jnp.where(eye, rowvec(vc_r), 0.0)
            r0 = p * V7X_LANES
            if use_mxu:
                vrow = v_r[p][pl.ds(t, 1), :]
                vh = vrow.astype(BF16)
                vl = (vrow - vh.astype(F32)).astype(BF16)
                vd = jnp.concatenate([eye_bf * vh, eye_bf * vl], axis=1)
                lhs = jnp.concatenate([_split_bf16(py), vd], axis=0)
                yv = jnp.dot(lhs, seg_mat, preferred_element_type=F32)
                y_lo, y_hi = yv[:A_HEAD, :V7X_LANES], yv[:A_HEAD, V7X_LANES:]
                v_b = jnp.where(lo, yv[A_HEAD:, :V7X_LANES], yv[A_HEAD:, V7X_LANES:])
            else:
                y_lo, y_hi = seg_sum(py)
                vrow = rowvec(v_r)
                v_lo = jnp.sum(jnp.where(eye_lo, vrow, 0.0), axis=1, keepdims=True)
                v_hi = jnp.sum(jnp.where(eye_hi, vrow, 0.0), axis=1, keepdims=True)
                v_b = jnp.where(lo, v_lo, v_hi)
            sa_b = jnp.where(lo, sa_lo, sa_hi)
            s_scr[p] = s * rowvec(w_r) + sa_b * rowvec(kka_r) + v_b * rowvec(k_r)
            yt_scr[r0:r0 + A_HEAD, :] = jnp.where(lane_t == t, y_lo, yt_scr[r0:r0 + A_HEAD, :])
            yt_scr[r0 + A_HEAD:r0 + V7X_LANES, :] = jnp.where(
                lane_t == t, y_hi, yt_scr[r0 + A_HEAD:r0 + V7X_LANES, :])
        return carry

    lax.fori_loop(0, steps, step, 0, unroll=RWKV_UNROLL if steps % RWKV_UNROLL == 0 else 1)
    for p in range(pairs):
        y_ref[:, p * V7X_LANES:(p + 1) * V7X_LANES] = yt_scr[p * V7X_LANES:(p + 1) * V7X_LANES, :].T

    @pl.when(tchunk == pl.num_programs(2) - 1)
    def _():
        for p in range(pairs):
            s = s_scr[p]
            sout_ref[0, 2 * p] = s[:, :A_HEAD]
            sout_ref[0, 2 * p + 1] = s[:, A_HEAD:]


def rwkv_scan(row_inputs, s0):
    b, t, c = row_inputs[0].shape
    pairs = RWKV_PAIRS_PER_STEP
    npairs = c // V7X_LANES
    tc = min(t, V7X_LANES)
    assert t % tc == 0 and npairs % pairs == 0 and len(row_inputs) == RWKV_ROW_INPUTS
    bw = pairs * V7X_LANES

    def pair_spec(p):
        return pl.BlockSpec((None, tc, V7X_LANES), lambda bi, hi, ti: (bi, ti, hi * pairs + p))

    st_spec = pl.BlockSpec((1, 2 * pairs, A_HEAD, A_HEAD), lambda bi, hi, ti: (bi, hi, 0, 0))
    operands = [x for x in row_inputs for _ in range(pairs)]
    return pl.pallas_call(
        functools.partial(_rwkv_scan_kernel, pairs=pairs, steps=tc, use_mxu=(tc == V7X_LANES)),
        grid=(b, npairs // pairs, t // tc),
        in_specs=[pair_spec(p) for _ in range(RWKV_ROW_INPUTS) for p in range(pairs)] + [st_spec],
        out_specs=[pl.BlockSpec((None, tc, bw), lambda bi, hi, ti: (bi, ti, hi)), st_spec],
        out_shape=[jax.ShapeDtypeStruct((b, t, c), F32), jax.ShapeDtypeStruct(s0.shape, F32)],
        scratch_shapes=[pltpu.VMEM((pairs, A_HEAD, V7X_LANES), F32),
                        pltpu.VMEM((bw, tc), F32)],
        compiler_params=pltpu.CompilerParams(dimension_semantics=("arbitrary", "arbitrary", "arbitrary")),
        name="rwkv_scan",
    )(*operands, s0)


def _select_topk_mask(score, allowed, topk):
    r, l = score.shape
    score = jnp.where(score == 0.0, 0.0, score)
    bits = pltpu.bitcast(score, I32)
    key = jnp.where(bits < 0, bits ^ jnp.int32(0x7FFFFFFF), bits)
    key = jnp.where(allowed, key, jnp.int32(INT_MIN))
    kf = jnp.float32(topk)

    def count(pred):
        return jnp.sum(jnp.where(pred, 1.0, 0.0), axis=1, keepdims=True)

    def bit_step(i, prefix):
        cand = prefix | lax.shift_left(jnp.int32(1), jnp.int32(31) - i)
        ok = count(key >= (cand ^ jnp.int32(INT_MIN))) >= kf
        return jnp.where(ok, cand, prefix)

    prefix = lax.fori_loop(0, 32, bit_step, jnp.zeros((r, 1), I32))
    thr = prefix ^ jnp.int32(INT_MIN)
    gt = key > thr
    eq = jnp.logical_and(key == thr, allowed)
    need = kf - count(gt)
    li = lax.broadcasted_iota(I32, (V7X_LANES, 2 * V7X_LANES), 0)
    lj = lax.broadcasted_iota(I32, (V7X_LANES, 2 * V7X_LANES), 1)
    tri_ones = jnp.where(jnp.logical_or(lj >= V7X_LANES, li < lj), 1.0, 0.0).astype(BF16)
    running = jnp.zeros((r, V7X_LANES), F32)
    sel = []
    for c in range(l // V7X_LANES):
        sl = slice(c * V7X_LANES, (c + 1) * V7X_LANES)
        eq_c = eq[:, sl]
        res = jnp.dot(jnp.where(eq_c, 1.0, 0.0).astype(BF16), tri_ones, preferred_element_type=F32)
        before = res[:, :V7X_LANES] + running
        running = running + res[:, V7X_LANES:]
        sel.append(jnp.logical_or(gt[:, sl], jnp.logical_and(eq_c, before < need)))
    return jnp.concatenate(sel, axis=1)


DSA_KEY_BUCKETS = 4


def _dsa_prompt_kernel(qi_ref, qil_ref, wi_ref, kidx_ref, kidxl_ref, q_ref, k_ref, v_ref, o_ref, *, topk):
    qb = pl.program_id(1)
    s_len = kidx_ref.shape[0]
    n_qb = s_len // QBLOCK
    group = B_HEADS // B_KV_HEADS

    def attend(l):
        kidx = kidx_ref[0:l, :]
        kidx_lo = kidxl_ref[0:l, :]
        score = jnp.zeros((QBLOCK, l), F32)
        for h in range(IDX_HEADS):
            hs = slice(h * IDX_DIM, (h + 1) * IDX_DIM)
            s = (lax.dot_general(qi_ref[:, hs], kidx, _NT, preferred_element_type=F32)
                 + lax.dot_general(qi_ref[:, hs], kidx_lo, _NT, preferred_element_type=F32)
                 + lax.dot_general(qil_ref[:, hs], kidx, _NT, preferred_element_type=F32))
            s = jnp.maximum(s * IDX_DIM ** -0.5, 0.0)
            score = score + s * wi_ref[:, h:h + 1]
        qpos = qb * QBLOCK + lax.broadcasted_iota(I32, (QBLOCK, l), 0)
        kpos = lax.broadcasted_iota(I32, (QBLOCK, l), 1)
        sel = _select_topk_mask(score, kpos <= qpos, topk)
        bias = jnp.where(sel, 0.0, -jnp.inf)
        for g in range(B_KV_HEADS):
            kg = k_ref[0:l, g * HEAD_DIM:(g + 1) * HEAD_DIM]
            vg = v_ref[0:l, g * HEAD_DIM:(g + 1) * HEAD_DIM]
            for j in range(group):
                h = g * group + j
                s = lax.dot_general(q_ref[:, h * HEAD_DIM:(h + 1) * HEAD_DIM], kg, _NT,
                                    preferred_element_type=F32)
                s = s * HEAD_DIM ** -0.5 + bias
                m = jnp.max(s, axis=1, keepdims=True)
                p = jnp.exp(s - m)
                denom = jnp.sum(p, axis=1, keepdims=True)
                o = jnp.dot(p.astype(BF16), vg, preferred_element_type=F32)
                o_ref[:, h * HEAD_DIM:(h + 1) * HEAD_DIM] = o / denom

    buckets = DSA_KEY_BUCKETS if n_qb % DSA_KEY_BUCKETS == 0 else 1
    per = n_qb // buckets
    for i in range(buckets):
        pl.when(jnp.logical_and(qb >= i * per, qb < (i + 1) * per))(
            functools.partial(attend, (i + 1) * per * QBLOCK))


def dsa_prompt_attend(q, k, v, qi, wi, kidx):
    b, s_len = q.shape[:2]
    topk = min(IDX_TOPK_MAX, s_len // 4)

    def flat16(z):
        return z.reshape(b, s_len, -1).astype(BF16)

    def flat16_lo(z):
        z = z.reshape(b, s_len, -1)
        return (z - z.astype(BF16).astype(F32)).astype(BF16)

    def qspec(w):
        return pl.BlockSpec((None, QBLOCK, w), lambda bi, qb: (bi, qb, 0))

    def kspec(w):
        return pl.BlockSpec((None, s_len, w), lambda bi, qb: (bi, 0, 0))

    return pl.pallas_call(
        functools.partial(_dsa_prompt_kernel, topk=topk),
        grid=(b, s_len // QBLOCK),
        in_specs=[qspec(B_WIDTH), qspec(B_WIDTH), qspec(IDX_HEADS), kspec(IDX_DIM), kspec(IDX_DIM),
                  qspec(B_WIDTH), kspec(B_KV_HEADS * HEAD_DIM), kspec(B_KV_HEADS * HEAD_DIM)],
        out_specs=qspec(B_WIDTH),
        out_shape=jax.ShapeDtypeStruct((b, s_len, B_WIDTH), F32),
        compiler_params=pltpu.CompilerParams(dimension_semantics=("arbitrary", "arbitrary"),
                                             vmem_limit_bytes=V7X_VMEM_LIMIT_BYTES),
        name="dsa_prompt",
    )(flat16(qi), flat16_lo(qi), wi, flat16(kidx), flat16_lo(kidx), flat16(q), flat16(k), flat16(v))


SAMPLE_PAGES_PER_STEP = 8
SAMPLE_T_PAD = 8


def _hi_lo(x):
    hi = x.astype(BF16)
    return hi, (x - hi.astype(F32)).astype(BF16)


def _dsa_sample_score_kernel(pt_ref, qi_ref, qil_ref, wi_ref, *refs, n_steps):
    j = pl.program_id(1)
    pages, new_ref, o_ref = refs[:-2], refs[-2], refs[-1]
    qh, ql, wcol = qi_ref[...], qil_ref[...], wi_ref[...]
    for i, pref in enumerate(pages):
        kidx = pref[...]
        if i == 0:
            kidx = jnp.where(j == n_steps - 1, new_ref[...], kidx)
        kh, kl = _hi_lo(kidx)
        s = (lax.dot_general(qh, kh, _NT, preferred_element_type=F32)
             + lax.dot_general(qh, kl, _NT, preferred_element_type=F32)
             + lax.dot_general(ql, kh, _NT, preferred_element_type=F32))
        s = jnp.maximum(s * IDX_DIM ** -0.5, 0.0) * wcol
        tot = s[0:SAMPLE_T_PAD]
        for h in range(1, IDX_HEADS):
            tot = tot + s[h * SAMPLE_T_PAD:(h + 1) * SAMPLE_T_PAD]
        o_ref[:, i * PAGE_SIZE:(i + 1) * PAGE_SIZE] = tot


def _dsa_sample_attn_kernel(pt_ref, score_ref, q_ref, *refs, n_steps, topk, past, t_len):
    pps = SAMPLE_PAGES_PER_STEP
    k_pages, v_pages = refs[:pps], refs[pps:2 * pps]
    kn_ref, vn_ref, o_ref, bias_scr, m_scr, l_scr, acc_scr = refs[2 * pps:]
    j = pl.program_id(1)
    width = score_ref.shape[1]
    group = B_HEADS // B_KV_HEADS

    @pl.when(j == 0)
    def _():
        qpos = past + lax.broadcasted_iota(I32, (SAMPLE_T_PAD, width), 0)
        kpos = lax.broadcasted_iota(I32, (SAMPLE_T_PAD, width), 1)
        sel = _select_topk_mask(score_ref[...], kpos <= qpos, topk)
        bias_scr[...] = jnp.where(sel, 0.0, -jnp.inf)
        m_scr[...] = jnp.full(m_scr.shape, -jnp.inf, F32)
        l_scr[...] = jnp.zeros(l_scr.shape, F32)
        acc_scr[...] = jnp.zeros(acc_scr.shape, F32)

    last = j == n_steps - 1
    col = pl.multiple_of(j * (pps * PAGE_SIZE), pps * PAGE_SIZE)
    b4 = bias_scr[0:t_len, pl.ds(col, pps * PAGE_SIZE)]
    bias = jnp.concatenate([b4] * group, axis=0)
    for g in range(B_KV_HEADS):
        def head_rows(pages, new_ref):
            first = jnp.where(last, new_ref[:, g, :], pages[0][:, g, :])
            return jnp.concatenate([first] + [r[:, g, :] for r in pages[1:]], axis=0).astype(BF16)

        kg = head_rows(k_pages, kn_ref)
        vg = head_rows(v_pages, vn_ref)
        s = lax.dot_general(q_ref[g], kg, _NT, preferred_element_type=F32) * HEAD_DIM ** -0.5 + bias
        m_old = m_scr[g]
        m_new = jnp.maximum(m_old, jnp.max(s, axis=1, keepdims=True))
        m_safe = jnp.where(m_new == -jnp.inf, 0.0, m_new)
        alpha = jnp.exp(m_old - m_safe)
        p = jnp.exp(s - m_safe)
        l_scr[g] = alpha * l_scr[g] + jnp.sum(p, axis=1, keepdims=True)
        acc_scr[g] = alpha * acc_scr[g] + jnp.dot(p.astype(BF16), vg, preferred_element_type=F32)
        m_scr[g] = m_new

    @pl.when(last)
    def _():
        for g in range(B_KV_HEADS):
            o = acc_scr[g] / l_scr[g]
            for hq in range(group):
                h = g * group + hq
                o_ref[:, h * HEAD_DIM:(h + 1) * HEAD_DIM] = o[hq * t_len:(hq + 1) * t_len]


def dsa_sample_attend(q, k, v, qi, wi, kidx, cache_k, cache_v, cache_kidx, page_table, layer):
    b, t = q.shape[:2]
    q = q.reshape(b, t, B_HEADS, HEAD_DIM)
    k = k.reshape(b, t, B_KV_HEADS, HEAD_DIM)
    v = v.reshape(b, t, B_KV_HEADS, HEAD_DIM)
    qi = qi.reshape(b, t, IDX_HEADS, IDX_DIM)
    n_pages = page_table.shape[1]
    past = n_pages * PAGE_SIZE
    topk = min(IDX_TOPK_MAX, (past + t) // 4)
    pps = SAMPLE_PAGES_PER_STEP
    assert n_pages % pps == 0 and t <= SAMPLE_T_PAD
    n_steps = n_pages // pps + 1
    width = n_steps * pps * PAGE_SIZE

    qi_r = jnp.pad(jnp.swapaxes(qi, 1, 2), ((0, 0), (0, 0), (0, SAMPLE_T_PAD - t), (0, 0)))
    qi_hi, qi_lo = _hi_lo(qi_r.reshape(b, IDX_HEADS * SAMPLE_T_PAD, IDX_DIM))
    wi_r = jnp.pad(jnp.swapaxes(wi, 1, 2), ((0, 0), (0, 0), (0, SAMPLE_T_PAD - t)))
    wi_r = wi_r.reshape(b, IDX_HEADS * SAMPLE_T_PAD, 1)
    kidx_new = jnp.pad(kidx, ((0, 0), (0, PAGE_SIZE - t), (0, 0)))

    def page_spec(i, *tail):
        def imap(bi, j, pt):
            return (layer, pt[bi, jnp.minimum(j * pps + i, n_pages - 1)]) + (0,) * (1 + len(tail))
        return pl.BlockSpec((None, None, PAGE_SIZE) + tail, imap)

    def per_b(shape):
        return pl.BlockSpec((None,) + shape, lambda bi, j, pt: (bi,) + (0,) * len(shape))

    score = pl.pallas_call(
        functools.partial(_dsa_sample_score_kernel, n_steps=n_steps),
        grid_spec=pltpu.PrefetchScalarGridSpec(
            num_scalar_prefetch=1, grid=(b, n_steps),
            in_specs=[per_b((IDX_HEADS * SAMPLE_T_PAD, IDX_DIM)), per_b((IDX_HEADS * SAMPLE_T_PAD, IDX_DIM)),
                      per_b((IDX_HEADS * SAMPLE_T_PAD, 1))]
                     + [page_spec(i, IDX_DIM) for i in range(pps)] + [per_b((PAGE_SIZE, IDX_DIM))],
            out_specs=pl.BlockSpec((None, SAMPLE_T_PAD, pps * PAGE_SIZE), lambda bi, j, pt: (bi, 0, j))),
        out_shape=jax.ShapeDtypeStruct((b, SAMPLE_T_PAD, width), F32),
        compiler_params=pltpu.CompilerParams(dimension_semantics=("arbitrary", "arbitrary")),
        name="dsa_sample_score",
    )(page_table, qi_hi, qi_lo, wi_r, *([cache_kidx] * pps), kidx_new)

    group = B_HEADS // B_KV_HEADS
    q_r = q.reshape(b, t, B_KV_HEADS, group, HEAD_DIM)
    q_r = jnp.transpose(q_r, (0, 2, 3, 1, 4)).reshape(b, B_KV_HEADS, group * t, HEAD_DIM).astype(BF16)
    k_new = jnp.pad(k, ((0, 0), (0, PAGE_SIZE - t), (0, 0), (0, 0)))
    v_new = jnp.pad(v, ((0, 0), (0, PAGE_SIZE - t), (0, 0), (0, 0)))
    return pl.pallas_call(
        functools.partial(_dsa_sample_attn_kernel, n_steps=n_steps, topk=topk, past=past, t_len=t),
        grid_spec=pltpu.PrefetchScalarGridSpec(
            num_scalar_prefetch=1, grid=(b, n_steps),
            in_specs=[per_b((SAMPLE_T_PAD, width)), per_b((B_KV_HEADS, group * t, HEAD_DIM))]
                     + [page_spec(i, B_KV_HEADS, HEAD_DIM) for i in range(pps)] * 2
                     + [per_b((PAGE_SIZE, B_KV_HEADS, HEAD_DIM))] * 2,
            out_specs=pl.BlockSpec((None, t, B_HEADS * HEAD_DIM), lambda bi, j, pt: (bi, 0, 0)),
            scratch_shapes=[pltpu.VMEM((SAMPLE_T_PAD, width), F32),
                            pltpu.VMEM((B_KV_HEADS, group * t, 1), F32),
                            pltpu.VMEM((B_KV_HEADS, group * t, 1), F32),
                            pltpu.VMEM((B_KV_HEADS, group * t, HEAD_DIM), F32)]),
        out_shape=jax.ShapeDtypeStruct((b, t, B_HEADS * HEAD_DIM), F32),
        compiler_params=pltpu.CompilerParams(dimension_semantics=("arbitrary", "arbitrary")),
        name="dsa_sample_attn",
    )(page_table, score, q_r, *([cache_k] * pps), *([cache_v] * pps), k_new, v_new)


DIL_WM = 128


def _dilated_prompt_kernel(q_ref, kp_ref, kc_ref, vp_ref, vc_ref, o_ref, lse_ref, *, dil, heads):
    n = pl.program_id(1)
    iq = lax.broadcasted_iota(I32, (DIL_WM, 2 * DIL_WM), 0)
    ik = lax.broadcasted_iota(I32, (DIL_WM, 2 * DIL_WM), 1)
    dist = iq + DIL_WM - ik
    ok = jnp.logical_and(dist >= 0, dist <= DIL_WM)
    ok = jnp.logical_and(ok, jnp.logical_or(ik >= DIL_WM, n > 0))
    bias = jnp.where(ok, 0.0, -jnp.inf)
    for r in range(dil):
        def rows(ref, hs):
            if dil == 1:
                return ref[:, hs]
            return ref[pl.ds(r, DIL_WM, stride=dil), :]

        for h in range(heads):
            hs = slice(h * HEAD_DIM, (h + 1) * HEAD_DIM)
            q = rows(q_ref, hs).astype(BF16)
            k = jnp.concatenate([rows(kp_ref, hs), rows(kc_ref, hs)], axis=0).astype(BF16)
            v = jnp.concatenate([rows(vp_ref, hs), rows(vc_ref, hs)], axis=0).astype(BF16)
            s = lax.dot_general(q, k, _NT, preferred_element_type=F32) * HEAD_DIM ** -0.5 + bias
            m = jnp.max(s, axis=1, keepdims=True)
            p = jnp.exp(s - m)
            denom = jnp.sum(p, axis=1, keepdims=True)
            o = jnp.dot(p.astype(BF16), v, preferred_element_type=F32) / denom
            lse = jnp.broadcast_to(m + jnp.log(denom), (DIL_WM, HEAD_DIM))
            if dil == 1:
                o_ref[:, hs] = o
                lse_ref[:, hs] = lse
            else:
                o_ref[pl.ds(r, DIL_WM, stride=dil), :] = o
                lse_ref[pl.ds(r, DIL_WM, stride=dil), :] = lse


def dilated_prompt_attend(q, k, v_src, v_col0, window, dil):
    b, s_len, width = q.shape
    assert window // dil == DIL_WM and s_len % (DIL_WM * dil) == 0
    rows = DIL_WM * dil
    heads = C_HEADS if dil == 1 else 1
    bw = heads * HEAD_DIM
    assert v_col0 % bw == 0
    vb = v_col0 // bw
    cur = pl.BlockSpec((None, rows, bw), lambda bi, n, hi: (bi, n, hi))
    prev = pl.BlockSpec((None, rows, bw), lambda bi, n, hi: (bi, jnp.maximum(n - 1, 0), hi))
    vcur = pl.BlockSpec((None, rows, bw), lambda bi, n, hi: (bi, n, vb + hi))
    vprev = pl.BlockSpec((None, rows, bw), lambda bi, n, hi: (bi, jnp.maximum(n - 1, 0), vb + hi))
    return pl.pallas_call(
        functools.partial(_dilated_prompt_kernel, dil=dil, heads=heads),
        grid=(b, s_len // rows, width // bw),
        in_specs=[cur, prev, cur, vprev, vcur],
        out_specs=[cur, cur],
        out_shape=[jax.ShapeDtypeStruct(q.shape, F32)] * 2,
        compiler_params=pltpu.CompilerParams(dimension_semantics=("arbitrary",) * 3),
        name="dilated_prompt",
    )(q, k, k, v_src, v_src)


NORM_ROPE_ROWS = 256


def rope_tables(pos):
    half = ROT_DIM // 2
    freqs = ROPE_THETA ** (-jnp.arange(half, dtype=F32) / half)
    ang = pos.astype(F32)[:, None] * freqs[None, :]
    cos, sin = jnp.cos(ang), jnp.sin(ang)
    t = pos.shape[0]
    ones = jnp.ones((t, HEAD_DIM - ROT_DIM), F32)
    zeros = jnp.zeros((t, HEAD_DIM - ROT_DIM), F32)
    c = jnp.concatenate([cos, cos, ones], axis=1)
    s_dn = jnp.concatenate([-sin, jnp.zeros_like(sin), zeros], axis=1)
    s_up = jnp.concatenate([jnp.zeros_like(sin), sin, zeros], axis=1)
    return c, s_dn, s_up


def _norm_rope_kernel(x_ref, g_ref, b_ref, c_ref, sd_ref, su_ref, o_ref, *, heads, norm):
    half = ROT_DIM // 2
    c, sd, su = c_ref[...], sd_ref[...], su_ref[...]
    for h in range(heads):
        hs = slice(h * HEAD_DIM, (h + 1) * HEAD_DIM)
        x = x_ref[:, hs]
        if norm == "rms":
            x = x * lax.rsqrt(jnp.mean(x * x, axis=1, keepdims=True) + NORM_EPS) * g_ref[...]
        elif norm == "layer":
            xc = x - jnp.mean(x, axis=1, keepdims=True)
            x = xc * lax.rsqrt(jnp.mean(xc * xc, axis=1, keepdims=True) + NORM_EPS) * g_ref[...] + b_ref[...]
        dn = pltpu.roll(x, HEAD_DIM - half, axis=1)
        up = pltpu.roll(x, half, axis=1)
        o_ref[:, hs] = x * c + dn * sd + up * su


def norm_rope(x, col0, heads, gain, bias, tables, t_len, norm):
    r, _ = x.shape
    bw = heads * HEAD_DIM
    assert col0 % bw == 0
    if t_len % 8 == 0:
        rows = min(NORM_ROPE_ROWS, t_len)
    else:
        rows, tables = r, tuple(jnp.tile(t, (r // t_len, 1)) for t in tables)
        t_len = r
    assert t_len % rows == 0 and r % rows == 0
    tb = t_len // rows
    tab = pl.BlockSpec((rows, HEAD_DIM), lambda i: (i % tb, 0))
    vec = pl.BlockSpec((1, HEAD_DIM), lambda i: (0, 0))
    g = (jnp.ones((HEAD_DIM,), F32) if gain is None else gain).reshape(1, HEAD_DIM).astype(F32)
    b = (jnp.zeros((HEAD_DIM,), F32) if bias is None else bias).reshape(1, HEAD_DIM).astype(F32)
    return pl.pallas_call(
        functools.partial(_norm_rope_kernel, heads=heads, norm=norm),
        grid=(r // rows,),
        in_specs=[pl.BlockSpec((rows, bw), lambda i: (i, col0 // bw)), vec, vec, tab, tab, tab],
        out_specs=pl.BlockSpec((rows, bw), lambda i: (i, 0)),
        out_shape=jax.ShapeDtypeStruct((r, bw), F32),
        compiler_params=pltpu.CompilerParams(dimension_semantics=("arbitrary",)),
        name="norm_rope",
    )(x, g, b, *tables)


def _rms(x, eps=NORM_EPS):
    xf = x.astype(F32)
    return xf * lax.rsqrt(jnp.mean(xf * xf, axis=-1, keepdims=True) + eps)


def rms_norm(x, g):
    return (_rms(x) * g.astype(F32)).astype(x.dtype)


def rwkv7_mix(sh, prev_row, s0, mu, w0, w_lora, a0, a_lora, k_k, k_a, r_k, gn_g, gn_b):
    bn, t, _ = sh.shape
    prev = jnp.concatenate([prev_row[:, None, :].astype(sh.dtype), sh[:, :-1]], axis=1)
    xm = sh + (prev - sh) * mu
    r, k, v = (xm[..., j * A_WIDTH:(j + 1) * A_WIDTH] for j in range(3))
    xw = xm[..., 3 * A_WIDTH:3 * A_WIDTH + DECAY_LORA]
    xa = xm[..., 3 * A_WIDTH + DECAY_LORA:]
    wlog = -jax.nn.softplus(-(w0 + mm(jnp.tanh(xw), w_lora))) - 0.5
    decay = jnp.exp(-jnp.exp(wlog.astype(F32)))
    a = jax.nn.sigmoid((a0 + mm(xa, a_lora)).astype(F32))

    def heads(z):
        return z.astype(F32).reshape(bn, t, A_HEADS, A_HEAD)

    def head_sum(z):
        return jnp.broadcast_to(jnp.sum(heads(z), axis=-1, keepdims=True),
                                (bn, t, A_HEADS, A_HEAD)).reshape(bn, t, A_WIDTH)

    kk = heads(k * k_k)
    kk = kk / jnp.maximum(jnp.sqrt(jnp.sum(kk * kk, axis=-1, keepdims=True)), 1e-12)
    kk = kk.reshape(bn, t, A_WIDTH)
    k2 = k * (1.0 + (a - 1.0) * k_a.astype(F32))
    kka = kk * a
    q = decay * r - kk * head_sum(kka * r)
    vc = v * head_sum(k2 * r)

    y, s_fin = rwkv_scan((-kk, decay, kka, k2, v, q, vc), s0.astype(F32))
    y = heads(y)
    ym = jnp.mean(y, axis=-1, keepdims=True)
    yc = y - ym
    yn = yc * lax.rsqrt(jnp.mean(yc * yc, axis=-1, keepdims=True) + GN_EPS)
    yn = yn.reshape(bn, t, A_WIDTH) * gn_g.astype(F32) + gn_b.astype(F32)
    bonus = (jnp.sum(heads(r * k2) * r_k.astype(F32), axis=-1, keepdims=True) * heads(v)).reshape(bn, t, A_WIDTH)
    return (yn + bonus).astype(sh.dtype), s_fin.astype(s0.dtype), sh[:, -1]


KV_WIDTH = B_KV_HEADS * HEAD_DIM
E_COLS = {}
_acc = 0
for _name, _w, _pad in (('q', B_WIDTH, 0), ('g_a', A_WIDTH, 0), ('g_b', B_WIDTH, 0), ('rkv', 3 * A_WIDTH, 0),
                        ('k', KV_WIDTH, 0), ('v', KV_WIDTH, 0), ('cqi', IDX_Q_RANK, 0), ('kidx', IDX_DIM, 0),
                        ('wi', IDX_HEADS, V7X_LANES - IDX_HEADS),
                        ('lora', DECAY_LORA + AAA_LORA, 2 * V7X_LANES - DECAY_LORA - AAA_LORA)):
    E_COLS[_name] = (_acc, _w)
    _acc += _w + _pad
E_WIDTH = _acc


def repack_even_w_in(w_all, layer):
    d_in = w_all.shape[1]
    src = {'rkv': 0, 'lora': 3 * A_WIDTH, 'g_a': SHIFT_W, 'q': IN_A, 'k': IN_A + B_WIDTH,
           'v': IN_A + B_WIDTH + KV_WIDTH, 'cqi': IN_A + B_WIDTH + 2 * KV_WIDTH}
    src['kidx'] = src['cqi'] + IDX_Q_RANK
    src['wi'] = src['kidx'] + IDX_DIM
    src['g_b'] = src['wi'] + IDX_HEADS
    parts, pos = [], 0
    for name, (start, width) in E_COLS.items():
        if start > pos:
            parts.append(jnp.zeros((d_in, start - pos), w_all.dtype))
        parts.append(w_all[layer, :, src[name]:src[name] + width])
        pos = start + width
    if E_WIDTH > pos:
        parts.append(jnp.zeros((d_in, E_WIDTH - pos), w_all.dtype))
    return jnp.concatenate(parts, axis=1)


def _cols(u, name):
    start, width = E_COLS[name]
    return u[:, start:start + width]


def even_mixer(u, bn, t, pos, prev_row, s0, attend, ep):
    sh = jnp.concatenate([_cols(u, 'rkv'), _cols(u, 'lora')], axis=1).reshape(bn, t, SHIFT_W)
    y_a, s_fin, last_row = rwkv7_mix(sh, prev_row, s0, ep['shift_mu'], ep['w0'], ep['w_lora'],
                                     ep['a0'], ep['a_lora'], ep['k_k'], ep['k_a'], ep['r_k'],
                                     ep['gn_gain'], ep['gn_bias'])
    tabs = rope_tables(pos)
    q = norm_rope(u, E_COLS['q'][0], B_HEADS, ep['q_norm'], None, tabs, t, "rms")
    k = norm_rope(u, E_COLS['k'][0], B_KV_HEADS, ep['k_norm'], None, tabs, t, "rms")
    v = _cols(u, 'v')
    qi = norm_rope(matmul(rms_norm(_cols(u, 'cqi'), ep['qi_norm']).astype(BF16), ep['w_qi']),
                   0, IDX_HEADS, None, None, tabs, t, "none")
    kidx = norm_rope(u, E_COLS['kidx'][0], 1, ep['kidx_gain'], ep['kidx_bias'], tabs, t, "layer")
    wi = _cols(u, 'wi') * IDX_HEADS ** -0.5

    def seq(z):
        return z.reshape(bn, t, -1)

    y_b = attend(seq(q), seq(k), seq(v), seq(qi), seq(wi), seq(kidx))
    y = jnp.concatenate([y_a * jax.nn.silu(seq(_cols(u, 'g_a'))), y_b * jax.nn.silu(seq(_cols(u, 'g_b')))],
                        axis=-1)
    heads4 = (bn, t, B_KV_HEADS, HEAD_DIM)
    y = y.reshape(bn * t, -1).astype(BF16)
    return y, (s_fin, last_row, k.reshape(heads4), v.reshape(heads4), seq(kidx))


def dilated_sample(q, k, v, buf_k, buf_v, window, dil):
    t = q.shape[1]
    d = q.shape[-1]
    wb = buf_k.shape[1]
    wm = window // dil
    kc = jnp.concatenate([buf_k.astype(k.dtype), k], axis=1)
    vc = jnp.concatenate([buf_v.astype(v.dtype), v], axis=1)
    idx = wb + jnp.arange(t)[:, None] - jnp.arange(wm + 1)[None, :] * dil
    valid = idx >= 0
    idxc = jnp.maximum(idx, 0)
    kg, vg = kc[:, idxc], vc[:, idxc]
    s = jnp.einsum('bthd,btjhd->bthj', q, kg).astype(F32) * d ** -0.5
    s = jnp.where(valid[None, :, None, :], s, -jnp.inf)
    lse = jax.nn.logsumexp(s, axis=-1)
    p = jnp.exp(s - lse[..., None])
    o = jnp.einsum('bthj,btjhd->bthd', p.astype(vg.dtype), vg)
    return o, lse, kc[:, -wb:], vc[:, -wb:]


def odd_mixer(u, bn, t, pos, sample_bufs, op):
    n_g = len(C_GROUPS)
    tabs = rope_tables(pos)
    heads4 = (bn, t, C_HEADS, HEAD_DIM)
    outs, lses, bufs = [], [], []
    for g, (win, dil) in enumerate(C_GROUPS):
        q = norm_rope(u, (3 * g) * C_WIDTH, C_HEADS, op['q_norm'][g], None, tabs, t, "rms")
        k = norm_rope(u, (3 * g + 1) * C_WIDTH, C_HEADS, op['k_norm'][g], None, tabs, t, "rms")
        v_col0 = (3 * g + 2) * C_WIDTH
        v = u[:, v_col0:v_col0 + C_WIDTH].reshape(heads4)
        if sample_bufs is None:
            o, lse = dilated_prompt_attend(q.reshape(bn, t, C_WIDTH), k.reshape(bn, t, C_WIDTH),
                                           u.reshape(bn, t, -1), v_col0, win, dil)
            keep = min(win, t)
            kb, vb = k.reshape(heads4)[:, -keep:], v[:, -keep:]
        else:
            o, lse, kb, vb = dilated_sample(q.reshape(heads4), k.reshape(heads4), v,
                                            sample_bufs[g][0], sample_bufs[g][1], win, dil)
            o = o.reshape(bn, t, C_WIDTH)
            lse = jnp.broadcast_to(lse[..., None], heads4).reshape(bn, t, C_WIDTH)
        outs.append(o)
        lses.append(lse)
        bufs += [kb, vb]
    alpha = jax.nn.softmax(jnp.stack(lses, axis=0), axis=0)
    o = jnp.sum(alpha * jnp.stack(outs, axis=0), axis=0)
    gate = u[:, 3 * n_g * C_WIDTH:].reshape(bn, t, C_WIDTH)
    return (o * jax.nn.silu(gate)).reshape(bn * t, C_WIDTH).astype(BF16), tuple(bufs)


def _ple_rows(h, m, p_l):
    return (_rms(h + m).astype(BF16), p_l.reshape(-1, p_l.shape[-1]).astype(BF16), h, m)


def kernel(x_prompt, x_sample, p_prompt, p_sample, state_wkv, state_shift, cache_k, cache_v,
           cache_kidx, page_table, cache_win_k0, cache_win_v0, cache_win_k1, cache_win_v1,
           cache_win_k2, cache_win_v2, ln_gain, e_w_in, e_shift_mu, e_w0, e_w_lora, e_a0,
           e_a_lora, e_k_k, e_k_a, e_r_k, e_gn_gain, e_gn_bias, e_q_norm, e_k_norm, e_qi_norm,
           e_w_qi, e_kidx_gain, e_kidx_bias, e_w_out, o_w_in, o_q_norm, o_k_norm, o_w_out,
           ple_w_proj, ple_w_gate):
    depth = ln_gain.shape[0]
    bp, s_len, _ = x_prompt.shape
    t_len = x_sample.shape[1]
    past = page_table.shape[1] * PAGE_SIZE
    pos_p = jnp.arange(s_len, dtype=jnp.int32)
    pos_s = past + jnp.arange(t_len, dtype=jnp.int32)
    bufs_k = (cache_win_k0, cache_win_k1, cache_win_k2)
    bufs_v = (cache_win_v0, cache_win_v1, cache_win_v2)
    bs = x_sample.shape[0]
    hp, hs = x_prompt.reshape(bp * s_len, -1), x_sample.reshape(bs * t_len, -1)
    ev_p, ev_s, od_p, od_s = [], [], [], []
    for i in range(depth):
        l = i // 2
        xp = rms_norm(hp, ln_gain[i]).astype(BF16)
        xs = rms_norm(hs, ln_gain[i]).astype(BF16)
        if i % 2 == 0:
            ep = {'shift_mu': e_shift_mu[l], 'w0': e_w0[l], 'w_lora': e_w_lora[l],
                  'a0': e_a0[l], 'a_lora': e_a_lora[l], 'k_k': e_k_k[l], 'k_a': e_k_a[l],
                  'r_k': e_r_k[l], 'gn_gain': e_gn_gain[l], 'gn_bias': e_gn_bias[l],
                  'q_norm': e_q_norm[l], 'k_norm': e_k_norm[l], 'qi_norm': e_qi_norm[l],
                  'w_qi': e_w_qi[l], 'kidx_gain': e_kidx_gain[l], 'kidx_bias': e_kidx_bias[l]}
            up, us = matmul2(xp, xs, repack_even_w_in(e_w_in, l))
            row0 = jnp.zeros((bp, SHIFT_W), F32)
            st0 = jnp.zeros((bp, A_HEADS, A_HEAD, A_HEAD), F32)
            yp, stp = even_mixer(up, bp, s_len, pos_p, row0, st0, dsa_prompt_attend, ep)
            att_s = functools.partial(dsa_sample_attend, cache_k=cache_k, cache_v=cache_v,
                                      cache_kidx=cache_kidx, page_table=page_table, layer=l)
            ys, sts = even_mixer(us, bs, t_len, pos_s, state_shift[l], state_wkv[l], att_s, ep)
            mp, ms = matmul2(yp, ys, e_w_out[l])
            ev_p.append(stp)
            ev_s.append(sts)
        else:
            op = {'q_norm': o_q_norm[l], 'k_norm': o_k_norm[l]}
            up, us = matmul2(xp, xs, o_w_in[l])
            yp, stp = odd_mixer(up, bp, s_len, pos_p, None, op)
            sample_bufs = [(bk[l], bv[l]) for bk, bv in zip(bufs_k, bufs_v)]
            ys, sts = odd_mixer(us, bs, t_len, pos_s, sample_bufs, op)
            mp, ms = matmul2(yp, ys, o_w_out[l])
            od_p.append(stp)
            od_s.append(sts)
        hp, hs = ple_update(ple_w_gate[i], ple_w_proj[i], _ple_rows(hp, mp, p_prompt[i]),
                            _ple_rows(hs, ms, p_sample[i]))
    hp, hs = hp.reshape(x_prompt.shape), hs.reshape(x_sample.shape)

    def st(lst, j):
        if len(lst) == 1:
            return lst[0][j][None]
        return jnp.stack([e[j] for e in lst], axis=0)

    outs = [hp, hs, st(ev_p, 0), st(ev_s, 0), st(ev_p, 1), st(ev_s, 1)]
    outs += [st(ev_p, j) for j in (2, 3, 4)] + [st(ev_s, j) for j in (2, 3, 4)]
    outs += [st(od_p, j) for j in range(6)] + [st(od_s, j) for j in range(6)]
    return tuple(outs)
```

```python
import functools

import jax
import jax.numpy as jnp
from jax import lax
from jax.experimental import pallas as pl
from jax.experimental.pallas import tpu as pltpu

F32 = jnp.float32
BF16 = jnp.bfloat16
I32 = jnp.int32

D_MODEL = 4096
PAGE_SIZE = 128
HEAD_DIM = 128
ROT_DIM = HEAD_DIM // 4
ROPE_THETA = 500000.0
NORM_EPS = 1e-6

A_WIDTH = D_MODEL // 2
A_HEAD = 64
A_HEADS = A_WIDTH // A_HEAD
DECAY_LORA = 96
AAA_LORA = 96
GN_EPS = 64e-5
SHIFT_W = 3 * A_WIDTH + DECAY_LORA + AAA_LORA

B_WIDTH = D_MODEL // 2
B_HEADS = B_WIDTH // HEAD_DIM
B_KV_HEADS = 4
IDX_HEADS = 16
IDX_DIM = 128
IDX_Q_RANK = 512
IDX_TOPK_MAX = 256
QBLOCK = 128

C_GROUPS = ((128, 1), (512, 4), (2048, 16))
C_HEADS = 16
C_WIDTH = C_HEADS * HEAD_DIM

IN_A = SHIFT_W + A_WIDTH
IN_B = B_WIDTH + 2 * B_KV_HEADS * HEAD_DIM + IDX_Q_RANK + IDX_DIM + IDX_HEADS + B_WIDTH

V7X_LANES = 128
V7X_VMEM_LIMIT_BYTES = 58 * 1024 * 1024
INT_MIN = -2 ** 31

_NT = (((1,), (1,)), ((), ()))


def _matmul_kernel(a_ref, b_ref, o_ref, bq_ref):
    @pl.when(pl.program_id(1) == 0)
    def _():
        bq_ref[...] = b_ref[...].astype(BF16)

    o_ref[...] = jnp.dot(a_ref[...].astype(BF16), bq_ref[...], preferred_element_type=F32)


def _pick_tile(n, cands):
    for c in cands:
        if n % c == 0:
            return c
    return n


def matmul(a, b):
    m, k = a.shape
    _, n = b.shape
    n_pad = -(-n // V7X_LANES) * V7X_LANES
    if n_pad != n:
        b = jnp.pad(b, ((0, 0), (0, n_pad - n)))
    tn = _pick_tile(n_pad, (1024, 512, 256, 128))
    tm = _pick_tile(m, (512, 256, 128, 64, 32, 16, 8))
    out = pl.pallas_call(
        _matmul_kernel,
        grid=(n_pad // tn, m // tm),
        in_specs=[pl.BlockSpec((tm, k), lambda j, i: (i, 0)),
                  pl.BlockSpec((k, tn), lambda j, i: (0, j))],
        out_specs=pl.BlockSpec((tm, tn), lambda j, i: (i, j)),
        out_shape=jax.ShapeDtypeStruct((m, n_pad), F32),
        scratch_shapes=[pltpu.VMEM((k, tn), BF16)],
        compiler_params=pltpu.CompilerParams(
            dimension_semantics=("arbitrary", "arbitrary"),
            vmem_limit_bytes=V7X_VMEM_LIMIT_BYTES),
        name="matmul",
    )(a, b)
    return out[:, :n] if n_pad != n else out


def mm(x, w):
    lead = x.shape[:-1]
    return matmul(x.reshape(-1, x.shape[-1]).astype(BF16), w).reshape(lead + (w.shape[-1],))


def _matmul2_kernel(a_ref, as_ref, b_ref, o_ref, os_ref, bq_ref, *, m_tiles):
    i = pl.program_id(1)

    @pl.when(i == 0)
    def _():
        bq_ref[...] = b_ref[...].astype(BF16)

    @pl.when(i < m_tiles)
    def _():
        o_ref[...] = jnp.dot(a_ref[...], bq_ref[...], preferred_element_type=F32)

    @pl.when(i == m_tiles)
    def _():
        os_ref[...] = jnp.dot(as_ref[...], bq_ref[...], preferred_element_type=F32)


def matmul2(a, a_s, b):
    m, k = a.shape
    ms = a_s.shape[0]
    n = b.shape[1]
    tn = _pick_tile(n, (1024, 512, 256, 128))
    tm = _pick_tile(m, (512, 256, 128, 64, 32, 16, 8))
    assert n % tn == 0 and m % tm == 0
    m_tiles = m // tm

    def big(j, i):
        return jnp.minimum(i, m_tiles - 1)

    return pl.pallas_call(
        functools.partial(_matmul2_kernel, m_tiles=m_tiles),
        grid=(n // tn, m_tiles + 1),
        in_specs=[pl.BlockSpec((tm, k), lambda j, i: (big(j, i), 0)),
                  pl.BlockSpec((ms, k), lambda j, i: (0, 0)),
                  pl.BlockSpec((k, tn), lambda j, i: (0, j))],
        out_specs=[pl.BlockSpec((tm, tn), lambda j, i: (big(j, i), j)),
                   pl.BlockSpec((ms, tn), lambda j, i: (0, j))],
        out_shape=[jax.ShapeDtypeStruct((m, n), F32), jax.ShapeDtypeStruct((ms, n), F32)],
        scratch_shapes=[pltpu.VMEM((k, tn), BF16)],
        compiler_params=pltpu.CompilerParams(
            dimension_semantics=("arbitrary", "arbitrary"),
            vmem_limit_bytes=V7X_VMEM_LIMIT_BYTES),
        name="matmul2",
    )(a, a_s, b)


PLE_TM = 512
PLE_TN = 512


def _ple_kernel(wg_ref, wp_ref, a_ref, p_ref, h_ref, m_ref, as_ref, ps_ref, hs_ref, ms_ref,
                o_ref, os_ref, wgq_ref, wpq_ref, *, m_tiles):
    i = pl.program_id(1)

    @pl.when(i == 0)
    def _():
        wgq_ref[...] = wg_ref[...].astype(BF16)
        wpq_ref[...] = wp_ref[...].astype(BF16)

    def update(a, p, h, m, o):
        gate = jnp.dot(a[...], wgq_ref[...], preferred_element_type=F32)
        proj = jnp.dot(p[...], wpq_ref[...], preferred_element_type=F32)
        o[...] = (h[...] + m[...]) + proj / (1.0 + jnp.exp(-gate))

    pl.when(i < m_tiles)(functools.partial(update, a_ref, p_ref, h_ref, m_ref, o_ref))
    pl.when(i == m_tiles)(functools.partial(update, as_ref, ps_ref, hs_ref, ms_ref, os_ref))


def ple_update(w_gate, w_proj, big, small):
    a, p, h, m = big
    a_s, p_s, h_s, m_s = small
    r, k = a.shape
    rs = a_s.shape[0]
    n = w_gate.shape[1]
    pk = p.shape[1]
    tm, tn = PLE_TM, PLE_TN
    assert n % tn == 0 and r % tm == 0
    m_tiles = r // tm

    def rows(width, full):
        if full:
            return pl.BlockSpec((tm, width), lambda j, i: (jnp.minimum(i, m_tiles - 1), 0))
        return pl.BlockSpec((tm, tn), lambda j, i: (jnp.minimum(i, m_tiles - 1), j))

    def rows_s(width, full):
        if full:
            return pl.BlockSpec((rs, width), lambda j, i: (0, 0))
        return pl.BlockSpec((rs, tn), lambda j, i: (0, j))

    return pl.pallas_call(
        functools.partial(_ple_kernel, m_tiles=m_tiles),
        grid=(n // tn, m_tiles + 1),
        in_specs=[pl.BlockSpec((k, tn), lambda j, i: (0, j)), pl.BlockSpec((pk, tn), lambda j, i: (0, j)),
                  rows(k, True), rows(pk, True), rows(tn, False), rows(tn, False),
                  rows_s(k, True), rows_s(pk, True), rows_s(tn, False), rows_s(tn, False)],
        out_specs=[rows(tn, False), rows_s(tn, False)],
        out_shape=[jax.ShapeDtypeStruct((r, n), F32), jax.ShapeDtypeStruct((rs, n), F32)],
        scratch_shapes=[pltpu.VMEM((k, tn), BF16), pltpu.VMEM((pk, tn), BF16)],
        compiler_params=pltpu.CompilerParams(dimension_semantics=("arbitrary", "arbitrary"),
                                             vmem_limit_bytes=V7X_VMEM_LIMIT_BYTES),
        name="ple_update",
    )(w_gate, w_proj, a, p, h, m, a_s, p_s, h_s, m_s)


RWKV_PAIRS_PER_STEP = 8
RWKV_UNROLL = 4
RWKV_ROW_INPUTS = 7


def _split_bf16(x):
    hi = x.astype(BF16)
    lo = (x - hi.astype(F32)).astype(BF16)
    return jnp.concatenate([hi, lo], axis=1)


def _rwkv_scan_kernel(*refs, pairs, steps, use_mxu):
    rows = [refs[j * pairs:(j + 1) * pairs] for j in range(RWKV_ROW_INPUTS)]
    nkk_r, w_r, kka_r, k_r, v_r, q_r, vc_r = rows
    s0_ref, y_ref, sout_ref, s_scr, yt_scr = refs[RWKV_ROW_INPUTS * pairs:]
    tchunk = pl.program_id(2)

    @pl.when(tchunk == 0)
    def _():
        for p in range(pairs):
            s_scr[p] = jnp.concatenate([s0_ref[0, 2 * p], s0_ref[0, 2 * p + 1]], axis=1)

    lane = lax.broadcasted_iota(I32, (A_HEAD, V7X_LANES), 1)
    row = lax.broadcasted_iota(I32, (A_HEAD, V7X_LANES), 0)
    lo = lane < A_HEAD
    eye_lo = lane == row
    eye_hi = lane == row + A_HEAD
    eye = jnp.logical_or(eye_lo, eye_hi)
    lane_t = lax.broadcasted_iota(I32, (A_HEAD, steps), 1)
    yt_scr[...] = jnp.zeros(yt_scr.shape, F32)
    if use_mxu:
        kk_i = lax.broadcasted_iota(I32, (2 * V7X_LANES, 2 * V7X_LANES), 0)
        nn_i = lax.broadcasted_iota(I32, (2 * V7X_LANES, 2 * V7X_LANES), 1)
        seg_mat = jnp.where(((kk_i % V7X_LANES) >= A_HEAD) == (nn_i >= V7X_LANES), 1.0, 0.0).astype(BF16)
        eye_bf = jnp.where(eye, 1.0, 0.0).astype(BF16)
        zero_bf = jnp.zeros((A_HEAD, V7X_LANES), BF16)

    def seg_sum(x):
        s_lo = jnp.sum(jnp.where(lo, x, 0.0), axis=1, keepdims=True)
        s_hi = jnp.sum(jnp.where(lo, 0.0, x), axis=1, keepdims=True)
        return s_lo, s_hi

    def step(t, carry):
        for p in range(pairs):
            def rowvec(group):
                return jnp.broadcast_to(group[p][pl.ds(t, 1), :], (A_HEAD, V7X_LANES))

            s = s_scr[p]
            sa_lo, sa_hi = seg_sum(s * rowvec(nkk_r))
            py = s * rowvec(q_r) + jnp.where(eye, rowvec(vc_r), 0.0)
            r0 = p * V7X_LANES
            if use_mxu:
                vrow = v_r[p][pl.ds(t, 1), :]
                vh = vrow.astype(BF16)
                vl = (vrow - vh.astype(F32)).astype(BF16)
                vd = jnp.concatenate([eye_bf * vh, eye_bf * vl], axis=1)
                py2 = jnp.concatenate([py.astype(BF16), zero_bf], axis=1)
                lhs = jnp.concatenate([py2, vd], axis=0)
                yv = jnp.dot(lhs, seg_mat, preferred_element_type=F32)
                y_lo, y_hi = yv[:A_HEAD, :V7X_LANES], yv[:A_HEAD, V7X_LANES:]
                v_b = jnp.where(lo, yv[A_HEAD:, :V7X_LANES], yv[A_HEAD:, V7X_LANES:])
            else:
                y_lo, y_hi = seg_sum(py)
                vrow = rowvec(v_r)
                v_lo = jnp.sum(jnp.where(eye_lo, vrow, 0.0), axis=1, keepdims=True)
                v_hi = jnp.sum(jnp.where(eye_hi, vrow, 0.0), axis=1, keepdims=True)
                v_b = jnp.where(lo, v_lo, v_hi)
            sa_b = jnp.where(lo, sa_lo, sa_hi)
            s_scr[p] = s * rowvec(w_r) + sa_b * rowvec(kka_r) + v_b * rowvec(k_r)
            yt_scr[r0:r0 + A_HEAD, :] = jnp.where(lane_t == t, y_lo, yt_scr[r0:r0 + A_HEAD, :])
            yt_scr[r0 + A_HEAD:r0 + V7X_LANES, :] = jnp.where(
                lane_t == t, y_hi, yt_scr[r0 + A_HEAD:r0 + V7X_LANES, :])
        return carry

    lax.fori_loop(0, steps, step, 0, unroll=RWKV_UNROLL if steps % RWKV_UNROLL == 0 else 1)
    for p in range(pairs):
        y_ref[:, p * V7X_LANES:(p + 1) * V7X_LANES] = yt_scr[p * V7X_LANES:(p + 1) * V7X_LANES, :].T

    @pl.when(tchunk == pl.num_programs(2) - 1)
    def _():
        for p in range(pairs):
            s = s_scr[p]
            sout_ref[0, 2 * p] = s[:, :A_HEAD]
            sout_ref[0, 2 * p + 1] = s[:, A_HEAD:]


def rwkv_scan(row_inputs, s0):
    b, t, c = row_inputs[0].shape
    pairs = RWKV_PAIRS_PER_STEP
    npairs = c // V7X_LANES
    tc = min(t, V7X_LANES)
    assert t % tc == 0 and npairs % pairs == 0 and len(row_inputs) == RWKV_ROW_INPUTS
    bw = pairs * V7X_LANES

    def pair_spec(p):
        return pl.BlockSpec((None, tc, V7X_LANES), lambda bi, hi, ti: (bi, ti, hi * pairs + p))

    st_spec = pl.BlockSpec((1, 2 * pairs, A_HEAD, A_HEAD), lambda bi, hi, ti: (bi, hi, 0, 0))
    operands = [x for x in row_inputs for _ in range(pairs)]
    return pl.pallas_call(
        functools.partial(_rwkv_scan_kernel, pairs=pairs, steps=tc, use_mxu=(tc == V7X_LANES)),
        grid=(b, npairs // pairs, t // tc),
        in_specs=[pair_spec(p) for _ in range(RWKV_ROW_INPUTS) for p in range(pairs)] + [st_spec],
        out_specs=[pl.BlockSpec((None, tc, bw), lambda bi, hi, ti: (bi, ti, hi)), st_spec],
        out_shape=[jax.ShapeDtypeStruct((b, t, c), F32), jax.ShapeDtypeStruct(s0.shape, F32)],
        scratch_shapes=[pltpu.VMEM((pairs, A_HEAD, V7X_LANES), F32),
                        pltpu.VMEM((bw, tc), F32)],
        compiler_params=pltpu.CompilerParams(dimension_semantics=("arbitrary", "arbitrary", "arbitrary")),
        name="rwkv_scan",
    )(*operands, s0)


def _select_topk_mask(score, allowed, topk):
    r, l = score.shape
    score = jnp.where(score == 0.0, 0.0, score)
    bits = pltpu.bitcast(score, I32)
    key = jnp.where(bits < 0, bits ^ jnp.int32(0x7FFFFFFF), bits)
    key = jnp.where(allowed, key, jnp.int32(INT_MIN))
    kf = jnp.float32(topk)

    def count(pred):
        return jnp.sum(jnp.where(pred, 1.0, 0.0), axis=1, keepdims=True)

    def bit_step(i, prefix):
        cand = prefix | lax.shift_left(jnp.int32(1), jnp.int32(31) - i)
        ok = count(key >= (cand ^ jnp.int32(INT_MIN))) >= kf
        return jnp.where(ok, cand, prefix)

    prefix = lax.fori_loop(0, 32, bit_step, jnp.zeros((r, 1), I32))
    thr = prefix ^ jnp.int32(INT_MIN)
    gt = key > thr
    eq = jnp.logical_and(key == thr, allowed)
    need = kf - count(gt)
    li = lax.broadcasted_iota(I32, (V7X_LANES, 2 * V7X_LANES), 0)
    lj = lax.broadcasted_iota(I32, (V7X_LANES, 2 * V7X_LANES), 1)
    tri_ones = jnp.where(jnp.logical_or(lj >= V7X_LANES, li < lj), 1.0, 0.0).astype(BF16)
    running = jnp.zeros((r, V7X_LANES), F32)
    sel = []
    for c in range(l // V7X_LANES):
        sl = slice(c * V7X_LANES, (c + 1) * V7X_LANES)
        eq_c = eq[:, sl]
        res = jnp.dot(jnp.where(eq_c, 1.0, 0.0).astype(BF16), tri_ones, preferred_element_type=F32)
        before = res[:, :V7X_LANES] + running
        running = running + res[:, V7X_LANES:]
        sel.append(jnp.logical_or(gt[:, sl], jnp.logical_and(eq_c, before < need)))
    return jnp.concatenate(sel, axis=1)


DSA_KEY_BUCKETS = 4


def _dsa_prompt_kernel(qi_ref, qil_ref, wi_ref, kidx_ref, kidxl_ref, q_ref, k_ref, v_ref, o_ref, *, topk):
    qb = pl.program_id(1)
    s_len = kidx_ref.shape[0]
    n_qb = s_len // QBLOCK
    group = B_HEADS // B_KV_HEADS

    def attend(l):
        kidx = kidx_ref[0:l, :]
        kidx_lo = kidxl_ref[0:l, :]
        score = jnp.zeros((QBLOCK, l), F32)
        for h in range(IDX_HEADS):
            hs = slice(h * IDX_DIM, (h + 1) * IDX_DIM)
            s = (lax.dot_general(qi_ref[:, hs], kidx, _NT, preferred_element_type=F32)
                 + lax.dot_general(qi_ref[:, hs], kidx_lo, _NT, preferred_element_type=F32)
                 + lax.dot_general(qil_ref[:, hs], kidx, _NT, preferred_element_type=F32))
            s = jnp.maximum(s * IDX_DIM ** -0.5, 0.0)
            score = score + s * wi_ref[:, h:h + 1]
        qpos = qb * QBLOCK + lax.broadcasted_iota(I32, (QBLOCK, l), 0)
        kpos = lax.broadcasted_iota(I32, (QBLOCK, l), 1)
        sel = _select_topk_mask(score, kpos <= qpos, topk)
        bias = jnp.where(sel, 0.0, -jnp.inf)
        for g in range(B_KV_HEADS):
            kg = k_ref[0:l, g * HEAD_DIM:(g + 1) * HEAD_DIM]
            vg = v_ref[0:l, g * HEAD_DIM:(g + 1) * HEAD_DIM]
            for j in range(group):
                h = g * group + j
                s = lax.dot_general(q_ref[:, h * HEAD_DIM:(h + 1) * HEAD_DIM], kg, _NT,
                                    preferred_element_type=F32)
                s = s * HEAD_DIM ** -0.5 + bias
                m = jnp.max(s, axis=1, keepdims=True)
                p = jnp.exp(s - m)
                denom = jnp.sum(p, axis=1, keepdims=True)
                o = jnp.dot(p.astype(BF16), vg, preferred_element_type=F32)
                o_ref[:, h * HEAD_DIM:(h + 1) * HEAD_DIM] = o / denom

    buckets = DSA_KEY_BUCKETS if n_qb % DSA_KEY_BUCKETS == 0 else 1
    per = n_qb // buckets
    for i in range(buckets):
        pl.when(jnp.logical_and(qb >= i * per, qb < (i + 1) * per))(
            functools.partial(attend, (i + 1) * per * QBLOCK))


def dsa_prompt_attend(q, k, v, qi, wi, kidx):
    b, s_len = q.shape[:2]
    topk = min(IDX_TOPK_MAX, s_len // 4)

    def flat16(z):
        return z.reshape(b, s_len, -1).astype(BF16)

    def flat16_lo(z):
        z = z.reshape(b, s_len, -1)
        return (z - z.astype(BF16).astype(F32)).astype(BF16)

    def qspec(w):
        return pl.BlockSpec((None, QBLOCK, w), lambda bi, qb: (bi, qb, 0))

    def kspec(w):
        return pl.BlockSpec((None, s_len, w), lambda bi, qb: (bi, 0, 0))

    return pl.pallas_call(
        functools.partial(_dsa_prompt_kernel, topk=topk),
        grid=(b, s_len // QBLOCK),
        in_specs=[qspec(B_WIDTH), qspec(B_WIDTH), qspec(IDX_HEADS), kspec(IDX_DIM), kspec(IDX_DIM),
                  qspec(B_WIDTH), kspec(B_KV_HEADS * HEAD_DIM), kspec(B_KV_HEADS * HEAD_DIM)],
        out_specs=qspec(B_WIDTH),
        out_shape=jax.ShapeDtypeStruct((b, s_len, B_WIDTH), F32),
        compiler_params=pltpu.CompilerParams(dimension_semantics=("arbitrary", "arbitrary"),
                                             vmem_limit_bytes=V7X_VMEM_LIMIT_BYTES),
        name="dsa_prompt",
    )(flat16(qi), flat16_lo(qi), wi, flat16(kidx), flat16_lo(kidx), flat16(q), flat16(k), flat16(v))


SAMPLE_PAGES_PER_STEP = 8
SAMPLE_T_PAD = 8


def _hi_lo(x):
    hi = x.astype(BF16)
    return hi, (x - hi.astype(F32)).astype(BF16)


def _dsa_sample_score_kernel(pt_ref, qi_ref, qil_ref, wi_ref, *refs, n_steps):
    j = pl.program_id(1)
    pages, new_ref, o_ref = refs[:-2], refs[-2], refs[-1]
    qh, ql, wcol = qi_ref[...], qil_ref[...], wi_ref[...]
    for i, pref in enumerate(pages):
        kidx = pref[...]
        if i == 0:
            kidx = jnp.where(j == n_steps - 1, new_ref[...], kidx)
        kh, kl = _hi_lo(kidx)
        s = (lax.dot_general(qh, kh, _NT, preferred_element_type=F32)
             + lax.dot_general(qh, kl, _NT, preferred_element_type=F32)
             + lax.dot_general(ql, kh, _NT, preferred_element_type=F32))
        s = jnp.maximum(s * IDX_DIM ** -0.5, 0.0) * wcol
        tot = s[0:SAMPLE_T_PAD]
        for h in range(1, IDX_HEADS):
            tot = tot + s[h * SAMPLE_T_PAD:(h + 1) * SAMPLE_T_PAD]
        o_ref[:, i * PAGE_SIZE:(i + 1) * PAGE_SIZE] = tot


def _dsa_sample_attn_kernel(pt_ref, score_ref, q_ref, *refs, n_steps, topk, past, t_len):
    pps = SAMPLE_PAGES_PER_STEP
    k_pages, v_pages = refs[:pps], refs[pps:2 * pps]
    kn_ref, vn_ref, o_ref, bias_scr, m_scr, l_scr, acc_scr = refs[2 * pps:]
    j = pl.program_id(1)
    width = score_ref.shape[1]
    group = B_HEADS // B_KV_HEADS

    @pl.when(j == 0)
    def _():
        qpos = past + lax.broadcasted_iota(I32, (SAMPLE_T_PAD, width), 0)
        kpos = lax.broadcasted_iota(I32, (SAMPLE_T_PAD, width), 1)
        sel = _select_topk_mask(score_ref[...], kpos <= qpos, topk)
        bias_scr[...] = jnp.where(sel, 0.0, -jnp.inf)
        m_scr[...] = jnp.full(m_scr.shape, -jnp.inf, F32)
        l_scr[...] = jnp.zeros(l_scr.shape, F32)
        acc_scr[...] = jnp.zeros(acc_scr.shape, F32)

    last = j == n_steps - 1
    col = pl.multiple_of(j * (pps * PAGE_SIZE), pps * PAGE_SIZE)
    b4 = bias_scr[0:t_len, pl.ds(col, pps * PAGE_SIZE)]
    bias = jnp.concatenate([b4] * group, axis=0)
    for g in range(B_KV_HEADS):
        def head_rows(pages, new_ref):
            first = jnp.where(last, new_ref[:, g, :], pages[0][:, g, :])
            return jnp.concatenate([first] + [r[:, g, :] for r in pages[1:]], axis=0).astype(BF16)

        kg = head_rows(k_pages, kn_ref)
        vg = head_rows(v_pages, vn_ref)
        s = lax.dot_general(q_ref[g], kg, _NT, preferred_element_type=F32) * HEAD_DIM ** -0.5 + bias
        m_old = m_scr[g]
        m_new = jnp.maximum(m_old, jnp.max(s, axis=1, keepdims=True))
        m_safe = jnp.where(m_new == -jnp.inf, 0.0, m_new)
        alpha = jnp.exp(m_old - m_safe)
        p = jnp.exp(s - m_safe)
        l_scr[g] = alpha * l_scr[g] + jnp.sum(p, axis=1, keepdims=True)
        acc_scr[g] = alpha * acc_scr[g] + jnp.dot(p.astype(BF16), vg, preferred_element_type=F32)
        m_scr[g] = m_new

    @pl.when(last)
    def _():
        for g in range(B_KV_HEADS):
            o = acc_scr[g] / l_scr[g]
            for hq in range(group):
                h = g * group + hq
                o_ref[:, h * HEAD_DIM:(h + 1) * HEAD_DIM] = o[hq * t_len:(hq + 1) * t_len]


def dsa_sample_attend(q, k, v, qi, wi, kidx, cache_k, cache_v, cache_kidx, page_table, layer):
    b, t = q.shape[:2]
    q = q.reshape(b, t, B_HEADS, HEAD_DIM)
    k = k.reshape(b, t, B_KV_HEADS, HEAD_DIM)
    v = v.reshape(b, t, B_KV_HEADS, HEAD_DIM)
    qi = qi.reshape(b, t, IDX_HEADS, IDX_DIM)
    n_pages = page_table.shape[1]
    past = n_pages * PAGE_SIZE
    topk = min(IDX_TOPK_MAX, (past + t) // 4)
    pps = SAMPLE_PAGES_PER_STEP
    assert n_pages % pps == 0 and t <= SAMPLE_T_PAD
    n_steps = n_pages // pps + 1
    width = n_steps * pps * PAGE_SIZE

    qi_r = jnp.pad(jnp.swapaxes(qi, 1, 2), ((0, 0), (0, 0), (0, SAMPLE_T_PAD - t), (0, 0)))
    qi_hi, qi_lo = _hi_lo(qi_r.reshape(b, IDX_HEADS * SAMPLE_T_PAD, IDX_DIM))
    wi_r = jnp.pad(jnp.swapaxes(wi, 1, 2), ((0, 0), (0, 0), (0, SAMPLE_T_PAD - t)))
    wi_r = wi_r.reshape(b, IDX_HEADS * SAMPLE_T_PAD, 1)
    kidx_new = jnp.pad(kidx, ((0, 0), (0, PAGE_SIZE - t), (0, 0)))

    def page_spec(i, *tail):
        def imap(bi, j, pt):
            return (layer, pt[bi, jnp.minimum(j * pps + i, n_pages - 1)]) + (0,) * (1 + len(tail))
        return pl.BlockSpec((None, None, PAGE_SIZE) + tail, imap)

    def per_b(shape):
        return pl.BlockSpec((None,) + shape, lambda bi, j, pt: (bi,) + (0,) * len(shape))

    score = pl.pallas_call(
        functools.partial(_dsa_sample_score_kernel, n_steps=n_steps),
        grid_spec=pltpu.PrefetchScalarGridSpec(
            num_scalar_prefetch=1, grid=(b, n_steps),
            in_specs=[per_b((IDX_HEADS * SAMPLE_T_PAD, IDX_DIM)), per_b((IDX_HEADS * SAMPLE_T_PAD, IDX_DIM)),
                      per_b((IDX_HEADS * SAMPLE_T_PAD, 1))]
                     + [page_spec(i, IDX_DIM) for i in range(pps)] + [per_b((PAGE_SIZE, IDX_DIM))],
            out_specs=pl.BlockSpec((None, SAMPLE_T_PAD, pps * PAGE_SIZE), lambda bi, j, pt: (bi, 0, j))),
        out_shape=jax.ShapeDtypeStruct((b, SAMPLE_T_PAD, width), F32),
        compiler_params=pltpu.CompilerParams(dimension_semantics=("arbitrary", "arbitrary")),
        name="dsa_sample_score",
    )(page_table, qi_hi, qi_lo, wi_r, *([cache_kidx] * pps), kidx_new)

    group = B_HEADS // B_KV_HEADS
    q_r = q.reshape(b, t, B_KV_HEADS, group, HEAD_DIM)
    q_r = jnp.transpose(q_r, (0, 2, 3, 1, 4)).reshape(b, B_KV_HEADS, group * t, HEAD_DIM).astype(BF16)
    k_new = jnp.pad(k, ((0, 0), (0, PAGE_SIZE - t), (0, 0), (0, 0)))
    v_new = jnp.pad(v, ((0, 0), (0, PAGE_SIZE - t), (0, 0), (0, 0)))
    return pl.pallas_call(
        functools.partial(_dsa_sample_attn_kernel, n_steps=n_steps, topk=topk, past=past, t_len=t),
        grid_spec=pltpu.PrefetchScalarGridSpec(
            num_scalar_prefetch=1, grid=(b, n_steps),
            in_specs=[per_b((SAMPLE_T_PAD, width)), per_b((B_KV_HEADS, group * t, HEAD_DIM))]
                     + [page_spec(i, B_KV_HEADS, HEAD_DIM) for i in range(pps)] * 2
                     + [per_b((PAGE_SIZE, B_KV_HEADS, HEAD_DIM))] * 2,
            out_specs=pl.BlockSpec((None, t, B_HEADS * HEAD_DIM), lambda bi, j, pt: (bi, 0, 0)),
            scratch_shapes=[pltpu.VMEM((SAMPLE_T_PAD, width), F32),
                            pltpu.VMEM((B_KV_HEADS, group * t, 1), F32),
                            pltpu.VMEM((B_KV_HEADS, group * t, 1), F32),
                            pltpu.VMEM((B_KV_HEADS, group * t, HEAD_DIM), F32)]),
        out_shape=jax.ShapeDtypeStruct((b, t, B_HEADS * HEAD_DIM), F32),
        compiler_params=pltpu.CompilerParams(dimension_semantics=("arbitrary", "arbitrary")),
        name="dsa_sample_attn",
    )(page_table, score, q_r, *([cache_k] * pps), *([cache_v] * pps), k_new, v_new)


DIL_WM = 128


def _dilated_prompt_kernel(q_ref, kp_ref, kc_ref, vp_ref, vc_ref, o_ref, lse_ref, *, dil, heads):
    n = pl.program_id(1)
    iq = lax.broadcasted_iota(I32, (DIL_WM, 2 * DIL_WM), 0)
    ik = lax.broadcasted_iota(I32, (DIL_WM, 2 * DIL_WM), 1)
    dist = iq + DIL_WM - ik
    ok = jnp.logical_and(dist >= 0, dist <= DIL_WM)
    ok = jnp.logical_and(ok, jnp.logical_or(ik >= DIL_WM, n > 0))
    bias = jnp.where(ok, 0.0, -jnp.inf)
    for r in range(dil):
        def rows(ref, hs):
            if dil == 1:
                return ref[:, hs]
            return ref[pl.ds(r, DIL_WM, stride=dil), :]

        for h in range(heads):
            hs = slice(h * HEAD_DIM, (h + 1) * HEAD_DIM)
            q = rows(q_ref, hs).astype(BF16)
            k = jnp.concatenate([rows(kp_ref, hs), rows(kc_ref, hs)], axis=0).astype(BF16)
            v = jnp.concatenate([rows(vp_ref, hs), rows(vc_ref, hs)], axis=0).astype(BF16)
            s = lax.dot_general(q, k, _NT, preferred_element_type=F32) * HEAD_DIM ** -0.5 + bias
            m = jnp.max(s, axis=1, keepdims=True)
            p = jnp.exp(s - m)
            denom = jnp.sum(p, axis=1, keepdims=True)
            o = jnp.dot(p.astype(BF16), v, preferred_element_type=F32) / denom
            lse = jnp.broadcast_to(m + jnp.log(denom), (DIL_WM, HEAD_DIM))
            if dil == 1:
                o_ref[:, hs] = o
                lse_ref[:, hs] = lse
            else:
                o_ref[pl.ds(r, DIL_WM, stride=dil), :] = o
                lse_ref[pl.ds(r, DIL_WM, stride=dil), :] = lse


def dilated_prompt_attend(q, k, v_src, v_col0, window, dil):
    b, s_len, width = q.shape
    assert window // dil == DIL_WM and s_len % (DIL_WM * dil) == 0
    rows = DIL_WM * dil
    heads = C_HEADS if dil == 1 else 1
    bw = heads * HEAD_DIM
    assert v_col0 % bw == 0
    vb = v_col0 // bw
    cur = pl.BlockSpec((None, rows, bw), lambda bi, n, hi: (bi, n, hi))
    prev = pl.BlockSpec((None, rows, bw), lambda bi, n, hi: (bi, jnp.maximum(n - 1, 0), hi))
    vcur = pl.BlockSpec((None, rows, bw), lambda bi, n, hi: (bi, n, vb + hi))
    vprev = pl.BlockSpec((None, rows, bw), lambda bi, n, hi: (bi, jnp.maximum(n - 1, 0), vb + hi))
    return pl.pallas_call(
        functools.partial(_dilated_prompt_kernel, dil=dil, heads=heads),
        grid=(b, s_len // rows, width // bw),
        in_specs=[cur, prev, cur, vprev, vcur],
        out_specs=[cur, cur],
        out_shape=[jax.ShapeDtypeStruct(q.shape, F32)] * 2,
        compiler_params=pltpu.CompilerParams(dimension_semantics=("arbitrary",) * 3),
        name="dilated_prompt",
    )(q, k, k, v_src, v_src)


NORM_ROPE_ROWS = 256


def rope_tables(pos):
    half = ROT_DIM // 2
    freqs = ROPE_THETA ** (-jnp.arange(half, dtype=F32) / half)
    ang = pos.astype(F32)[:, None] * freqs[None, :]
    cos, sin = jnp.cos(ang), jnp.sin(ang)
    t = pos.shape[0]
    ones = jnp.ones((t, HEAD_DIM - ROT_DIM), F32)
    zeros = jnp.zeros((t, HEAD_DIM - ROT_DIM), F32)
    c = jnp.concatenate([cos, cos, ones], axis=1)
    s_dn = jnp.concatenate([-sin, jnp.zeros_like(sin), zeros], axis=1)
    s_up = jnp.concatenate([jnp.zeros_like(sin), sin, zeros], axis=1)
    return c, s_dn, s_up


def _norm_rope_kernel(x_ref, g_ref, b_ref, c_ref, sd_ref, su_ref, o_ref, *, heads, norm):
    half = ROT_DIM // 2
    c, sd, su = c_ref[...], sd_ref[...], su_ref[...]
    for h in range(heads):
        hs = slice(h * HEAD_DIM, (h + 1) * HEAD_DIM)
        x = x_ref[:, hs]
        if norm == "rms":
            x = x * lax.rsqrt(jnp.mean(x * x, axis=1, keepdims=True) + NORM_EPS) * g_ref[...]
        elif norm == "layer":
            xc = x - jnp.mean(x, axis=1, keepdims=True)
            x = xc * lax.rsqrt(jnp.mean(xc * xc, axis=1, keepdims=True) + NORM_EPS) * g_ref[...] + b_ref[...]
        dn = pltpu.roll(x, HEAD_DIM - half, axis=1)
        up = pltpu.roll(x, half, axis=1)
        o_ref[:, hs] = x * c + dn * sd + up * su


def norm_rope(x, col0, heads, gain, bias, tables, t_len, norm):
    r, _ = x.shape
    bw = heads * HEAD_DIM
    assert col0 % bw == 0
    if t_len % 8 == 0:
        rows = min(NORM_ROPE_ROWS, t_len)
    else:
        rows, tables = r, tuple(jnp.tile(t, (r // t_len, 1)) for t in tables)
        t_len = r
    assert t_len % rows == 0 and r % rows == 0
    tb = t_len // rows
    tab = pl.BlockSpec((rows, HEAD_DIM), lambda i: (i % tb, 0))
    vec = pl.BlockSpec((1, HEAD_DIM), lambda i: (0, 0))
    g = (jnp.ones((HEAD_DIM,), F32) if gain is None else gain).reshape(1, HEAD_DIM).astype(F32)
    b = (jnp.zeros((HEAD_DIM,), F32) if bias is None else bias).reshape(1, HEAD_DIM).astype(F32)
    return pl.pallas_call(
        functools.partial(_norm_rope_kernel, heads=heads, norm=norm),
        grid=(r // rows,),
        in_specs=[pl.BlockSpec((rows, bw), lambda i: (i, col0 // bw)), vec, vec, tab, tab, tab],
        out_specs=pl.BlockSpec((rows, bw), lambda i: (i, 0)),
        out_shape=jax.ShapeDtypeStruct((r, bw), F32),
        compiler_params=pltpu.CompilerParams(dimension_semantics=("arbitrary",)),
        name="norm_rope",
    )(x, g, b, *tables)


def _rms(x, eps=NORM_EPS):
    xf = x.astype(F32)
    return xf * lax.rsqrt(jnp.mean(xf * xf, axis=-1, keepdims=True) + eps)


def rms_norm(x, g):
    return (_rms(x) * g.astype(F32)).astype(x.dtype)


def rwkv7_mix(sh, prev_row, s0, mu, w0, w_lora, a0, a_lora, k_k, k_a, r_k, gn_g, gn_b):
    bn, t, _ = sh.shape
    prev = jnp.concatenate([prev_row[:, None, :].astype(sh.dtype), sh[:, :-1]], axis=1)
    xm = sh + (prev - sh) * mu
    r, k, v = (xm[..., j * A_WIDTH:(j + 1) * A_WIDTH] for j in range(3))
    xw = xm[..., 3 * A_WIDTH:3 * A_WIDTH + DECAY_LORA]
    xa = xm[..., 3 * A_WIDTH + DECAY_LORA:]
    wlog = -jax.nn.softplus(-(w0 + mm(jnp.tanh(xw), w_lora))) - 0.5
    decay = jnp.exp(-jnp.exp(wlog.astype(F32)))
    a = jax.nn.sigmoid((a0 + mm(xa, a_lora)).astype(F32))

    def heads(z):
        return z.astype(F32).reshape(bn, t, A_HEADS, A_HEAD)

    def head_sum(z):
        return jnp.broadcast_to(jnp.sum(heads(z), axis=-1, keepdims=True),
                                (bn, t, A_HEADS, A_HEAD)).reshape(bn, t, A_WIDTH)

    kk = heads(k * k_k)
    kk = kk / jnp.maximum(jnp.sqrt(jnp.sum(kk * kk, axis=-1, keepdims=True)), 1e-12)
    kk = kk.reshape(bn, t, A_WIDTH)
    k2 = k * (1.0 + (a - 1.0) * k_a.astype(F32))
    kka = kk * a
    q = decay * r - kk * head_sum(kka * r)
    vc = v * head_sum(k2 * r)

    y, s_fin = rwkv_scan((-kk, decay, kka, k2, v, q, vc), s0.astype(F32))
    y = heads(y)
    ym = jnp.mean(y, axis=-1, keepdims=True)
    yc = y - ym
    yn = yc * lax.rsqrt(jnp.mean(yc * yc, axis=-1, keepdims=True) + GN_EPS)
    yn = yn.reshape(bn, t, A_WIDTH) * gn_g.astype(F32) + gn_b.astype(F32)
    bonus = (jnp.sum(heads(r * k2) * r_k.astype(F32), axis=-1, keepdims=True) * heads(v)).reshape(bn, t, A_WIDTH)
    return (yn + bonus).astype(sh.dtype), s_fin.astype(s0.dtype), sh[:, -1]


KV_WIDTH = B_KV_HEADS * HEAD_DIM
E_COLS = {}
_acc = 0
for _name, _w, _pad in (('q', B_WIDTH, 0), ('g_a', A_WIDTH, 0), ('g_b', B_WIDTH, 0), ('rkv', 3 * A_WIDTH, 0),
                        ('k', KV_WIDTH, 0), ('v', KV_WIDTH, 0), ('cqi', IDX_Q_RANK, 0), ('kidx', IDX_DIM, 0),
                        ('wi', IDX_HEADS, V7X_LANES - IDX_HEADS),
                        ('lora', DECAY_LORA + AAA_LORA, 2 * V7X_LANES - DECAY_LORA - AAA_LORA)):
    E_COLS[_name] = (_acc, _w)
    _acc += _w + _pad
E_WIDTH = _acc


def repack_even_w_in(w_all, layer):
    d_in = w_all.shape[1]
    src = {'rkv': 0, 'lora': 3 * A_WIDTH, 'g_a': SHIFT_W, 'q': IN_A, 'k': IN_A + B_WIDTH,
           'v': IN_A + B_WIDTH + KV_WIDTH, 'cqi': IN_A + B_WIDTH + 2 * KV_WIDTH}
    src['kidx'] = src['cqi'] + IDX_Q_RANK
    src['wi'] = src['kidx'] + IDX_DIM
    src['g_b'] = src['wi'] + IDX_HEADS
    parts, pos = [], 0
    for name, (start, width) in E_COLS.items():
        if start > pos:
            parts.append(jnp.zeros((d_in, start - pos), w_all.dtype))
        parts.append(w_all[layer, :, src[name]:src[name] + width])
        pos = start + width
    if E_WIDTH > pos:
        parts.append(jnp.zeros((d_in, E_WIDTH - pos), w_all.dtype))
    return jnp.concatenate(parts, axis=1)


def _cols(u, name):
    start, width = E_COLS[name]
    return u[:, start:start + width]


def even_mixer(u, bn, t, pos, prev_row, s0, attend, ep):
    sh = jnp.concatenate([_cols(u, 'rkv'), _cols(u, 'lora')], axis=1).reshape(bn, t, SHIFT_W)
    y_a, s_fin, last_row = rwkv7_mix(sh, prev_row, s0, ep['shift_mu'], ep['w0'], ep['w_lora'],
                                     ep['a0'], ep['a_lora'], ep['k_k'], ep['k_a'], ep['r_k'],
                                     ep['gn_gain'], ep['gn_bias'])
    tabs = rope_tables(pos)
    q = norm_rope(u, E_COLS['q'][0], B_HEADS, ep['q_norm'], None, tabs, t, "rms")
    k = norm_rope(u, E_COLS['k'][0], B_KV_HEADS, ep['k_norm'], None, tabs, t, "rms")
    v = _cols(u, 'v')
    qi = norm_rope(matmul(rms_norm(_cols(u, 'cqi'), ep['qi_norm']).astype(BF16), ep['w_qi']),
                   0, IDX_HEADS, None, None, tabs, t, "none")
    kidx = norm_rope(u, E_COLS['kidx'][0], 1, ep['kidx_gain'], ep['kidx_bias'], tabs, t, "layer")
    wi = _cols(u, 'wi') * IDX_HEADS ** -0.5

    def seq(z):
        return z.reshape(bn, t, -1)

    y_b = attend(seq(q), seq(k), seq(v), seq(qi), seq(wi), seq(kidx))
    y = jnp.concatenate([y_a * jax.nn.silu(seq(_cols(u, 'g_a'))), y_b * jax.nn.silu(seq(_cols(u, 'g_b')))],
                        axis=-1)
    heads4 = (bn, t, B_KV_HEADS, HEAD_DIM)
    y = y.reshape(bn * t, -1).astype(BF16)
    return y, (s_fin, last_row, k.reshape(heads4), v.reshape(heads4), seq(kidx))


def dilated_sample(q, k, v, buf_k, buf_v, window, dil):
    t = q.shape[1]
    d = q.shape[-1]
    wb = buf_k.shape[1]
    wm = window // dil
    kc = jnp.concatenate([buf_k.astype(k.dtype), k], axis=1)
    vc = jnp.concatenate([buf_v.astype(v.dtype), v], axis=1)
    idx = wb + jnp.arange(t)[:, None] - jnp.arange(wm + 1)[None, :] * dil
    valid = idx >= 0
    idxc = jnp.maximum(idx, 0)
    kg, vg = kc[:, idxc], vc[:, idxc]
    s = jnp.einsum('bthd,btjhd->bthj', q, kg).astype(F32) * d ** -0.5
    s = jnp.where(valid[None, :, None, :], s, -jnp.inf)
    lse = jax.nn.logsumexp(s, axis=-1)
    p = jnp.exp(s - lse[..., None])
    o = jnp.einsum('bthj,btjhd->bthd', p.astype(vg.dtype), vg)
    return o, lse, kc[:, -wb:], vc[:, -wb:]


def odd_mixer(u, bn, t, pos, sample_bufs, op):
    n_g = len(C_GROUPS)
    tabs = rope_tables(pos)
    heads4 = (bn, t, C_HEADS, HEAD_DIM)
    outs, lses, bufs = [], [], []
    for g, (win, dil) in enumerate(C_GROUPS):
        q = norm_rope(u, (3 * g) * C_WIDTH, C_HEADS, op['q_norm'][g], None, tabs, t, "rms")
        k = norm_rope(u, (3 * g + 1) * C_WIDTH, C_HEADS, op['k_norm'][g], None, tabs, t, "rms")
        v_col0 = (3 * g + 2) * C_WIDTH
        v = u[:, v_col0:v_col0 + C_WIDTH].reshape(heads4)
        if sample_bufs is None:
            o, lse = dilated_prompt_attend(q.reshape(bn, t, C_WIDTH), k.reshape(bn, t, C_WIDTH),
                                           u.reshape(bn, t, -1), v_col0, win, dil)
            keep = min(win, t)
            kb, vb = k.reshape(heads4)[:, -keep:], v[:, -keep:]
        else:
            o, lse, kb, vb = dilated_sample(q.reshape(heads4), k.reshape(heads4), v,
                                            sample_bufs[g][0], sample_bufs[g][1], win, dil)
            o = o.reshape(bn, t, C_WIDTH)
            lse = jnp.broadcast_to(lse[..., None], heads4).reshape(bn, t, C_WIDTH)
        outs.append(o)
        lses.append(lse)
        bufs += [kb, vb]
    alpha = jax.nn.softmax(jnp.stack(lses, axis=0), axis=0)
    o = jnp.sum(alpha * jnp.stack(outs, axis=0), axis=0)
    gate = u[:, 3 * n_g * C_WIDTH:].reshape(bn, t, C_WIDTH)
    return (o * jax.nn.silu(gate)).reshape(bn * t, C_WIDTH).astype(BF16), tuple(bufs)


def _ple_rows(h, m, p_l):
    return (_rms(h + m).astype(BF16), p_l.reshape(-1, p_l.shape[-1]).astype(BF16), h, m)


def kernel(x_prompt, x_sample, p_prompt, p_sample, state_wkv, state_shift, cache_k, cache_v,
           cache_kidx, page_table, cache_win_k0, cache_win_v0, cache_win_k1, cache_win_v1,
           cache_win_k2, cache_win_v2, ln_gain, e_w_in, e_shift_mu, e_w0, e_w_lora, e_a0,
           e_a_lora, e_k_k, e_k_a, e_r_k, e_gn_gain, e_gn_bias, e_q_norm, e_k_norm, e_qi_norm,
           e_w_qi, e_kidx_gain, e_kidx_bias, e_w_out, o_w_in, o_q_norm, o_k_norm, o_w_out,
           ple_w_proj, ple_w_gate):
    depth = ln_gain.shape[0]
    bp, s_len, _ = x_prompt.shape
    t_len = x_sample.shape[1]
    past = page_table.shape[1] * PAGE_SIZE
    pos_p = jnp.arange(s_len, dtype=jnp.int32)
    pos_s = past + jnp.arange(t_len, dtype=jnp.int32)
    bufs_k = (cache_win_k0, cache_win_k1, cache_win_k2)
    bufs_v = (cache_win_v0, cache_win_v1, cache_win_v2)
    bs = x_sample.shape[0]
    hp, hs = x_prompt.reshape(bp * s_len, -1), x_sample.reshape(bs * t_len, -1)
    ev_p, ev_s, od_p, od_s = [], [], [], []
    for i in range(depth):
        l = i // 2
        xp = rms_norm(hp, ln_gain[i]).astype(BF16)
        xs = rms_norm(hs, ln_gain[i]).astype(BF16)
        if i % 2 == 0:
            ep = {'shift_mu': e_shift_mu[l], 'w0': e_w0[l], 'w_lora': e_w_lora[l],
                  'a0': e_a0[l], 'a_lora': e_a_lora[l], 'k_k': e_k_k[l], 'k_a': e_k_a[l],
                  'r_k': e_r_k[l], 'gn_gain': e_gn_gain[l], 'gn_bias': e_gn_bias[l],
                  'q_norm': e_q_norm[l], 'k_norm': e_k_norm[l], 'qi_norm': e_qi_norm[l],
                  'w_qi': e_w_qi[l], 'kidx_gain': e_kidx_gain[l], 'kidx_bias': e_kidx_bias[l]}
            up, us = matmul2(xp, xs, repack_even_w_in(e_w_in, l))
            row0 = jnp.zeros((bp, SHIFT_W), F32)
            st0 = jnp.zeros((bp, A_HEADS, A_HEAD, A_HEAD), F32)
            yp, stp = even_mixer(up, bp, s_len, pos_p, row0, st0, dsa_prompt_attend, ep)
            att_s = functools.partial(dsa_sample_attend, cache_k=cache_k, cache_v=cache_v,
                                      cache_kidx=cache_kidx, page_table=page_table, layer=l)
            ys, sts = even_mixer(us, bs, t_len, pos_s, state_shift[l], state_wkv[l], att_s, ep)
            mp, ms = matmul2(yp, ys, e_w_out[l])
            ev_p.append(stp)
            ev_s.append(sts)
        else:
            op = {'q_norm': o_q_norm[l], 'k_norm': o_k_norm[l]}
            up, us = matmul2(xp, xs, o_w_in[l])
            yp, stp = odd_mixer(up, bp, s_len, pos_p, None, op)
            sample_bufs = [(bk[l], bv[l]) for bk, bv in zip(bufs_k, bufs_v)]
            ys, sts = odd_mixer(us, bs, t_len, pos_s, sample_bufs, op)
            mp, ms = matmul2(yp, ys, o_w_out[l])
            od_p.append(stp)
            od_s.append(sts)
        hp, hs = ple_update(ple_w_gate[i], ple_w_proj[i], _ple_rows(hp, mp, p_prompt[i]),
                            _ple_rows(hs, ms, p_sample[i]))
    hp, hs = hp.reshape(x_prompt.shape), hs.reshape(x_sample.shape)

    def st(lst, j):
        if len(lst) == 1:
            return lst[0][j][None]
        return jnp.stack([e[j] for e in lst], axis=0)

    outs = [hp, hs, st(ev_p, 0), st(ev_s, 0), st(ev_p, 1), st(ev_s, 1)]
    outs += [st(ev_p, j) for j in (2, 3, 4)] + [st(ev_s, j) for j in (2, 3, 4)]
    outs += [st(od_p, j) for j in range(6)] + [st(od_s, j) for j in range(6)]
    return tuple(outs)
```

```python
import functools

import jax
import jax.numpy as jnp
from jax import lax
from jax.experimental import pallas as pl
from jax.experimental.pallas import tpu as pltpu

F32 = jnp.float32
BF16 = jnp.bfloat16
I32 = jnp.int32

D_MODEL = 4096
PAGE_SIZE = 128
HEAD_DIM = 128
ROT_DIM = HEAD_DIM // 4
ROPE_THETA = 500000.0
NORM_EPS = 1e-6

A_WIDTH = D_MODEL // 2
A_HEAD = 64
A_HEADS = A_WIDTH // A_HEAD
DECAY_LORA = 96
AAA_LORA = 96
GN_EPS = 64e-5
SHIFT_W = 3 * A_WIDTH + DECAY_LORA + AAA_LORA

B_WIDTH = D_MODEL // 2
B_HEADS = B_WIDTH // HEAD_DIM
B_KV_HEADS = 4
IDX_HEADS = 16
IDX_DIM = 128
IDX_Q_RANK = 512
IDX_TOPK_MAX = 256
QBLOCK = 128

C_GROUPS = ((128, 1), (512, 4), (2048, 16))
C_HEADS = 16
C_WIDTH = C_HEADS * HEAD_DIM

IN_A = SHIFT_W + A_WIDTH
IN_B = B_WIDTH + 2 * B_KV_HEADS * HEAD_DIM + IDX_Q_RANK + IDX_DIM + IDX_HEADS + B_WIDTH

V7X_LANES = 128
V7X_VMEM_LIMIT_BYTES = 58 * 1024 * 1024
INT_MIN = -2 ** 31

_NT = (((1,), (1,)), ((), ()))


def _matmul_kernel(a_ref, b_ref, o_ref, bq_ref):
    @pl.when(pl.program_id(1) == 0)
    def _():
        bq_ref[...] = b_ref[...].astype(BF16)

    o_ref[...] = jnp.dot(a_ref[...].astype(BF16), bq_ref[...], preferred_element_type=F32)


def _pick_tile(n, cands):
    for c in cands:
        if n % c == 0:
            return c
    return n


def matmul(a, b):
    m, k = a.shape
    _, n = b.shape
    n_pad = -(-n // V7X_LANES) * V7X_LANES
    if n_pad != n:
        b = jnp.pad(b, ((0, 0), (0, n_pad - n)))
    tn = _pick_tile(n_pad, (1024, 512, 256, 128))
    tm = _pick_tile(m, (512, 256, 128, 64, 32, 16, 8))
    out = pl.pallas_call(
        _matmul_kernel,
        grid=(n_pad // tn, m // tm),
        in_specs=[pl.BlockSpec((tm, k), lambda j, i: (i, 0)),
                  pl.BlockSpec((k, tn), lambda j, i: (0, j))],
        out_specs=pl.BlockSpec((tm, tn), lambda j, i: (i, j)),
        out_shape=jax.ShapeDtypeStruct((m, n_pad), F32),
        scratch_shapes=[pltpu.VMEM((k, tn), BF16)],
        compiler_params=pltpu.CompilerParams(
            dimension_semantics=("arbitrary", "arbitrary"),
            vmem_limit_bytes=V7X_VMEM_LIMIT_BYTES),
        name="matmul",
    )(a, b)
    return out[:, :n] if n_pad != n else out


def mm(x, w):
    lead = x.shape[:-1]
    return matmul(x.reshape(-1, x.shape[-1]).astype(BF16), w).reshape(lead + (w.shape[-1],))


def _matmul2_kernel(a_ref, as_ref, b_ref, o_ref, os_ref, bq_ref):
    @pl.when(pl.program_id(1) == 0)
    def _():
        bq_ref[...] = b_ref[...].astype(BF16)
        os_ref[...] = jnp.dot(as_ref[...], bq_ref[...], preferred_element_type=F32)

    o_ref[...] = jnp.dot(a_ref[...], bq_ref[...], preferred_element_type=F32)


def matmul2(a, a_s, b):
    m, k = a.shape
    ms = a_s.shape[0]
    n = b.shape[1]
    tn = _pick_tile(n, (1024, 512, 256, 128))
    tm = _pick_tile(m, (512, 256, 128, 64, 32, 16, 8))
    assert n % tn == 0 and m % tm == 0
    return pl.pallas_call(
        _matmul2_kernel,
        grid=(n // tn, m // tm),
        in_specs=[pl.BlockSpec((tm, k), lambda j, i: (i, 0)),
                  pl.BlockSpec((ms, k), lambda j, i: (0, 0)),
                  pl.BlockSpec((k, tn), lambda j, i: (0, j))],
        out_specs=[pl.BlockSpec((tm, tn), lambda j, i: (i, j)),
                   pl.BlockSpec((ms, tn), lambda j, i: (0, j))],
        out_shape=[jax.ShapeDtypeStruct((m, n), F32), jax.ShapeDtypeStruct((ms, n), F32)],
        scratch_shapes=[pltpu.VMEM((k, tn), BF16)],
        compiler_params=pltpu.CompilerParams(
            dimension_semantics=("arbitrary", "arbitrary"),
            vmem_limit_bytes=V7X_VMEM_LIMIT_BYTES),
        name="matmul2",
    )(a, a_s, b)


PLE_TM = 512
PLE_TN = 512


def _ple_kernel(wg_ref, wp_ref, a_ref, p_ref, h_ref, m_ref, as_ref, ps_ref, hs_ref, ms_ref,
                o_ref, os_ref, wgq_ref, wpq_ref):
    def update(a, p, h, m, o):
        gate = jnp.dot(a[...], wgq_ref[...], preferred_element_type=F32)
        proj = jnp.dot(p[...], wpq_ref[...], preferred_element_type=F32)
        o[...] = (h[...] + m[...]) + proj / (1.0 + jnp.exp(-gate))

    @pl.when(pl.program_id(1) == 0)
    def _():
        wgq_ref[...] = wg_ref[...].astype(BF16)
        wpq_ref[...] = wp_ref[...].astype(BF16)
        update(as_ref, ps_ref, hs_ref, ms_ref, os_ref)

    update(a_ref, p_ref, h_ref, m_ref, o_ref)


def ple_update(w_gate, w_proj, big, small):
    a, p, h, m = big
    a_s, p_s, h_s, m_s = small
    r, k = a.shape
    rs = a_s.shape[0]
    n = w_gate.shape[1]
    pk = p.shape[1]
    tm, tn = PLE_TM, PLE_TN
    assert n % tn == 0 and r % tm == 0

    def rows(width, full):
        if full:
            return pl.BlockSpec((tm, width), lambda j, i: (i, 0))
        return pl.BlockSpec((tm, tn), lambda j, i: (i, j))

    def rows_s(width, full):
        if full:
            return pl.BlockSpec((rs, width), lambda j, i: (0, 0))
        return pl.BlockSpec((rs, tn), lambda j, i: (0, j))

    return pl.pallas_call(
        _ple_kernel,
        grid=(n // tn, r // tm),
        in_specs=[pl.BlockSpec((k, tn), lambda j, i: (0, j)), pl.BlockSpec((pk, tn), lambda j, i: (0, j)),
                  rows(k, True), rows(pk, True), rows(tn, False), rows(tn, False),
                  rows_s(k, True), rows_s(pk, True), rows_s(tn, False), rows_s(tn, False)],
        out_specs=[rows(tn, False), rows_s(tn, False)],
        out_shape=[jax.ShapeDtypeStruct((r, n), F32), jax.ShapeDtypeStruct((rs, n), F32)],
        scratch_shapes=[pltpu.VMEM((k, tn), BF16), pltpu.VMEM((pk, tn), BF16)],
        compiler_params=pltpu.CompilerParams(dimension_semantics=("arbitrary", "arbitrary"),
                                             vmem_limit_bytes=V7X_VMEM_LIMIT_BYTES),
        name="ple_update",
    )(w_gate, w_proj, a, p, h, m, a_s, p_s, h_s, m_s)


RWKV_PAIRS_PER_STEP = 8
RWKV_UNROLL = 4
RWKV_ROW_INPUTS = 7


def _split_bf16(x):
    hi = x.astype(BF16)
    lo = (x - hi.astype(F32)).astype(BF16)
    return jnp.concatenate([hi, lo], axis=1)


def _rwkv_scan_kernel(*refs, pairs, steps, use_mxu):
    rows = [refs[j * pairs:(j + 1) * pairs] for j in range(RWKV_ROW_INPUTS)]
    nkk_r, w_r, kka_r, k_r, v_r, q_r, vc_r = rows
    s0_ref, y_ref, sout_ref, s_scr, yt_scr = refs[RWKV_ROW_INPUTS * pairs:]
    tchunk = pl.program_id(2)

    @pl.when(tchunk == 0)
    def _():
        for p in range(pairs):
            s_scr[p] = jnp.concatenate([s0_ref[0, 2 * p], s0_ref[0, 2 * p + 1]], axis=1)

    lane = lax.broadcasted_iota(I32, (A_HEAD, V7X_LANES), 1)
    row = lax.broadcasted_iota(I32, (A_HEAD, V7X_LANES), 0)
    lo = lane < A_HEAD
    eye_lo = lane == row
    eye_hi = lane == row + A_HEAD
    eye = jnp.logical_or(eye_lo, eye_hi)
    lane_t = lax.broadcasted_iota(I32, (A_HEAD, steps), 1)
    yt_scr[...] = jnp.zeros(yt_scr.shape, F32)
    if use_mxu:
        kk_i = lax.broadcasted_iota(I32, (2 * V7X_LANES, 2 * V7X_LANES), 0)
        nn_i = lax.broadcasted_iota(I32, (2 * V7X_LANES, 2 * V7X_LANES), 1)
        seg_mat = jnp.where(((kk_i % V7X_LANES) >= A_HEAD) == (nn_i >= V7X_LANES), 1.0, 0.0).astype(BF16)
        eye_bf = jnp.where(eye, 1.0, 0.0).astype(BF16)
        zero_bf = jnp.zeros((A_HEAD, V7X_LANES), BF16)

    def seg_sum(x):
        s_lo = jnp.sum(jnp.where(lo, x, 0.0), axis=1, keepdims=True)
        s_hi = jnp.sum(jnp.where(lo, 0.0, x), axis=1, keepdims=True)
        return s_lo, s_hi

    def step(t, carry):
        for p in range(pairs):
            def rowvec(group):
                return jnp.broadcast_to(group[p][pl.ds(t, 1), :], (A_HEAD, V7X_LANES))

            s = s_scr[p]
            sa_lo, sa_hi = seg_sum(s * rowvec(nkk_r))
            py = s * rowvec(q_r) + jnp.where(eye, rowvec(vc_r), 0.0)
            r0 = p * V7X_LANES
            if use_mxu:
                vrow = v_r[p][pl.ds(t, 1), :]
                vh = vrow.astype(BF16)
                vl = (vrow - vh.astype(F32)).astype(BF16)
                vd = jnp.concatenate([eye_bf * vh, eye_bf * vl], axis=1)
                py2 = jnp.concatenate([py.astype(BF16), zero_bf], axis=1)
                lhs = jnp.concatenate([py2, vd], axis=0)
                yv = jnp.dot(lhs, seg_mat, preferred_element_type=F32)
                y_lo, y_hi = yv[:A_HEAD, :V7X_LANES], yv[:A_HEAD, V7X_LANES:]
                v_b = jnp.where(lo, yv[A_HEAD:, :V7X_LANES], yv[A_HEAD:, V7X_LANES:])
            else:
                y_lo, y_hi = seg_sum(py)
                vrow = rowvec(v_r)
                v_lo = jnp.sum(jnp.where(eye_lo, vrow, 0.0), axis=1, keepdims=True)
                v_hi = jnp.sum(jnp.where(eye_hi, vrow, 0.0), axis=1, keepdims=True)
                v_b = jnp.where(lo, v_lo, v_hi)
            sa_b = jnp.where(lo, sa_lo, sa_hi)
            s_scr[p] = s * rowvec(w_r) + sa_b * rowvec(kka_r) + v_b * rowvec(k_r)
            yt_scr[r0:r0 + A_HEAD, :] = jnp.where(lane_t == t, y_lo, yt_scr[r0:r0 + A_HEAD, :])
            yt_scr[r0 + A_HEAD:r0 + V7X_LANES, :] = jnp.where(
                lane_t == t, y_hi, yt_scr[r0 + A_HEAD:r0 + V7X_LANES, :])
        return carry

    lax.fori_loop(0, steps, step, 0, unroll=RWKV_UNROLL if steps % RWKV_UNROLL == 0 else 1)
    for p in range(pairs):
        y_ref[:, p * V7X_LANES:(p + 1) * V7X_LANES] = yt_scr[p * V7X_LANES:(p + 1) * V7X_LANES, :].T

    @pl.when(tchunk == pl.num_programs(2) - 1)
    def _():
        for p in range(pairs):
            s = s_scr[p]
            sout_ref[0, 2 * p] = s[:, :A_HEAD]
            sout_ref[0, 2 * p + 1] = s[:, A_HEAD:]


def rwkv_scan(row_inputs, s0):
    b, t, c = row_inputs[0].shape
    pairs = RWKV_PAIRS_PER_STEP
    npairs = c // V7X_LANES
    tc = min(t, V7X_LANES)
    assert t % tc == 0 and npairs % pairs == 0 and len(row_inputs) == RWKV_ROW_INPUTS
    bw = pairs * V7X_LANES

    def pair_spec(p):
        return pl.BlockSpec((None, tc, V7X_LANES), lambda bi, hi, ti: (bi, ti, hi * pairs + p))

    st_spec = pl.BlockSpec((1, 2 * pairs, A_HEAD, A_HEAD), lambda bi, hi, ti: (bi, hi, 0, 0))
    operands = [x for x in row_inputs for _ in range(pairs)]
    return pl.pallas_call(
        functools.partial(_rwkv_scan_kernel, pairs=pairs, steps=tc, use_mxu=(tc == V7X_LANES)),
        grid=(b, npairs // pairs, t // tc),
        in_specs=[pair_spec(p) for _ in range(RWKV_ROW_INPUTS) for p in range(pairs)] + [st_spec],
        out_specs=[pl.BlockSpec((None, tc, bw), lambda bi, hi, ti: (bi, ti, hi)), st_spec],
        out_shape=[jax.ShapeDtypeStruct((b, t, c), F32), jax.ShapeDtypeStruct(s0.shape, F32)],
        scratch_shapes=[pltpu.VMEM((pairs, A_HEAD, V7X_LANES), F32),
                        pltpu.VMEM((bw, tc), F32)],
        compiler_params=pltpu.CompilerParams(dimension_semantics=("arbitrary", "arbitrary", "arbitrary")),
        name="rwkv_scan",
    )(*operands, s0)


def _select_topk_mask(score, allowed, topk):
    r, l = score.shape
    score = jnp.where(score == 0.0, 0.0, score)
    bits = pltpu.bitcast(score, I32)
    key = jnp.where(bits < 0, bits ^ jnp.int32(0x7FFFFFFF), bits)
    key = jnp.where(allowed, key, jnp.int32(INT_MIN))
    kf = jnp.float32(topk)

    def count(pred):
        return jnp.sum(jnp.where(pred, 1.0, 0.0), axis=1, keepdims=True)

    def bit_step(i, prefix):
        cand = prefix | lax.shift_left(jnp.int32(1), jnp.int32(31) - i)
        ok = count(key >= (cand ^ jnp.int32(INT_MIN))) >= kf
        return jnp.where(ok, cand, prefix)

    prefix = lax.fori_loop(0, 32, bit_step, jnp.zeros((r, 1), I32))
    thr = prefix ^ jnp.int32(INT_MIN)
    gt = key > thr
    eq = jnp.logical_and(key == thr, allowed)
    need = kf - count(gt)
    li = lax.broadcasted_iota(I32, (V7X_LANES, 2 * V7X_LANES), 0)
    lj = lax.broadcasted_iota(I32, (V7X_LANES, 2 * V7X_LANES), 1)
    tri_ones = jnp.where(jnp.logical_or(lj >= V7X_LANES, li < lj), 1.0, 0.0).astype(BF16)
    running = jnp.zeros((r, V7X_LANES), F32)
    sel = []
    for c in range(l // V7X_LANES):
        sl = slice(c * V7X_LANES, (c + 1) * V7X_LANES)
        eq_c = eq[:, sl]
        res = jnp.dot(jnp.where(eq_c, 1.0, 0.0).astype(BF16), tri_ones, preferred_element_type=F32)
        before = res[:, :V7X_LANES] + running
        running = running + res[:, V7X_LANES:]
        sel.append(jnp.logical_or(gt[:, sl], jnp.logical_and(eq_c, before < need)))
    return jnp.concatenate(sel, axis=1)


def _dsa_prompt_kernel(qi_ref, qil_ref, wi_ref, kidx_ref, kidxl_ref, q_ref, k_ref, v_ref, o_ref, *, topk):
    qb = pl.program_id(1)
    s_len = kidx_ref.shape[0]
    group = B_HEADS // B_KV_HEADS

    def attend(l):
        kidx = kidx_ref[0:l, :]
        kidx_lo = kidxl_ref[0:l, :]
        score = jnp.zeros((QBLOCK, l), F32)
        for h in range(IDX_HEADS):
            hs = slice(h * IDX_DIM, (h + 1) * IDX_DIM)
            s = (lax.dot_general(qi_ref[:, hs], kidx, _NT, preferred_element_type=F32)
                 + lax.dot_general(qi_ref[:, hs], kidx_lo, _NT, preferred_element_type=F32)
                 + lax.dot_general(qil_ref[:, hs], kidx, _NT, preferred_element_type=F32))
            s = jnp.maximum(s * IDX_DIM ** -0.5, 0.0)
            score = score + s * wi_ref[:, h:h + 1]
        qpos = qb * QBLOCK + lax.broadcasted_iota(I32, (QBLOCK, l), 0)
        kpos = lax.broadcasted_iota(I32, (QBLOCK, l), 1)
        sel = _select_topk_mask(score, kpos <= qpos, topk)
        bias = jnp.where(sel, 0.0, -jnp.inf)
        for g in range(B_KV_HEADS):
            kg = k_ref[0:l, g * HEAD_DIM:(g + 1) * HEAD_DIM]
            vg = v_ref[0:l, g * HEAD_DIM:(g + 1) * HEAD_DIM]
            for j in range(group):
                h = g * group + j
                s = lax.dot_general(q_ref[:, h * HEAD_DIM:(h + 1) * HEAD_DIM], kg, _NT,
                                    preferred_element_type=F32)
                s = s * HEAD_DIM ** -0.5 + bias
                m = jnp.max(s, axis=1, keepdims=True)
                p = jnp.exp(s - m)
                denom = jnp.sum(p, axis=1, keepdims=True)
                o = jnp.dot(p.astype(BF16), vg, preferred_element_type=F32)
                o_ref[:, h * HEAD_DIM:(h + 1) * HEAD_DIM] = o / denom

    attend(s_len)


def dsa_prompt_attend(q, k, v, qi, wi, kidx):
    b, s_len = q.shape[:2]
    topk = min(IDX_TOPK_MAX, s_len // 4)

    def flat16(z):
        return z.reshape(b, s_len, -1).astype(BF16)

    def flat16_lo(z):
        z = z.reshape(b, s_len, -1)
        return (z - z.astype(BF16).astype(F32)).astype(BF16)

    def qspec(w):
        return pl.BlockSpec((None, QBLOCK, w), lambda bi, qb: (bi, qb, 0))

    def kspec(w):
        return pl.BlockSpec((None, s_len, w), lambda bi, qb: (bi, 0, 0))

    return pl.pallas_call(
        functools.partial(_dsa_prompt_kernel, topk=topk),
        grid=(b, s_len // QBLOCK),
        in_specs=[qspec(B_WIDTH), qspec(B_WIDTH), qspec(IDX_HEADS), kspec(IDX_DIM), kspec(IDX_DIM),
                  qspec(B_WIDTH), kspec(B_KV_HEADS * HEAD_DIM), kspec(B_KV_HEADS * HEAD_DIM)],
        out_specs=qspec(B_WIDTH),
        out_shape=jax.ShapeDtypeStruct((b, s_len, B_WIDTH), F32),
        compiler_params=pltpu.CompilerParams(dimension_semantics=("arbitrary", "arbitrary"),
                                             vmem_limit_bytes=V7X_VMEM_LIMIT_BYTES),
        name="dsa_prompt",
    )(flat16(qi), flat16_lo(qi), wi, flat16(kidx), flat16_lo(kidx), flat16(q), flat16(k), flat16(v))


SAMPLE_PAGES_PER_STEP = 8
SAMPLE_T_PAD = 8


def _hi_lo(x):
    hi = x.astype(BF16)
    return hi, (x - hi.astype(F32)).astype(BF16)


def _dsa_sample_score_kernel(pt_ref, qi_ref, qil_ref, wi_ref, *refs, n_steps):
    j = pl.program_id(1)
    pages, new_ref, o_ref = refs[:-2], refs[-2], refs[-1]
    qh, ql, wcol = qi_ref[...], qil_ref[...], wi_ref[...]
    for i, pref in enumerate(pages):
        kidx = pref[...]
        if i == 0:
            kidx = jnp.where(j == n_steps - 1, new_ref[...], kidx)
        kh, kl = _hi_lo(kidx)
        s = (lax.dot_general(qh, kh, _NT, preferred_element_type=F32)
             + lax.dot_general(qh, kl, _NT, preferred_element_type=F32)
             + lax.dot_general(ql, kh, _NT, preferred_element_type=F32))
        s = jnp.maximum(s * IDX_DIM ** -0.5, 0.0) * wcol
        tot = s[0:SAMPLE_T_PAD]
        for h in range(1, IDX_HEADS):
            tot = tot + s[h * SAMPLE_T_PAD:(h + 1) * SAMPLE_T_PAD]
        o_ref[:, i * PAGE_SIZE:(i + 1) * PAGE_SIZE] = tot


def _dsa_sample_attn_kernel(pt_ref, score_ref, q_ref, *refs, n_steps, topk, past, t_len):
    pps = SAMPLE_PAGES_PER_STEP
    k_pages, v_pages = refs[:pps], refs[pps:2 * pps]
    kn_ref, vn_ref, o_ref, bias_scr, m_scr, l_scr, acc_scr = refs[2 * pps:]
    j = pl.program_id(1)
    width = score_ref.shape[1]
    group = B_HEADS // B_KV_HEADS

    @pl.when(j == 0)
    def _():
        qpos = past + lax.broadcasted_iota(I32, (SAMPLE_T_PAD, width), 0)
        kpos = lax.broadcasted_iota(I32, (SAMPLE_T_PAD, width), 1)
        sel = _select_topk_mask(score_ref[...], kpos <= qpos, topk)
        bias_scr[...] = jnp.where(sel, 0.0, -jnp.inf)
        m_scr[...] = jnp.full(m_scr.shape, -jnp.inf, F32)
        l_scr[...] = jnp.zeros(l_scr.shape, F32)
        acc_scr[...] = jnp.zeros(acc_scr.shape, F32)

    last = j == n_steps - 1
    col = pl.multiple_of(j * (pps * PAGE_SIZE), pps * PAGE_SIZE)
    b4 = bias_scr[0:t_len, pl.ds(col, pps * PAGE_SIZE)]
    bias = jnp.concatenate([b4] * group, axis=0)
    for g in range(B_KV_HEADS):
        def head_rows(pages, new_ref):
            first = jnp.where(last, new_ref[:, g, :], pages[0][:, g, :])
            return jnp.concatenate([first] + [r[:, g, :] for r in pages[1:]], axis=0).astype(BF16)

        kg = head_rows(k_pages, kn_ref)
        vg = head_rows(v_pages, vn_ref)
        s = lax.dot_general(q_ref[g], kg, _NT, preferred_element_type=F32) * HEAD_DIM ** -0.5 + bias
        m_old = m_scr[g]
        m_new = jnp.maximum(m_old, jnp.max(s, axis=1, keepdims=True))
        m_safe = jnp.where(m_new == -jnp.inf, 0.0, m_new)
        alpha = jnp.exp(m_old - m_safe)
        p = jnp.exp(s - m_safe)
        l_scr[g] = alpha * l_scr[g] + jnp.sum(p, axis=1, keepdims=True)
        acc_scr[g] = alpha * acc_scr[g] + jnp.dot(p.astype(BF16), vg, preferred_element_type=F32)
        m_scr[g] = m_new

    @pl.when(last)
    def _():
        for g in range(B_KV_HEADS):
            o = acc_scr[g] / l_scr[g]
            for hq in range(group):
                h = g * group + hq
                o_ref[:, h * HEAD_DIM:(h + 1) * HEAD_DIM] = o[hq * t_len:(hq + 1) * t_len]


def dsa_sample_attend(q, k, v, qi, wi, kidx, cache_k, cache_v, cache_kidx, page_table, layer):
    b, t = q.shape[:2]
    q = q.reshape(b, t, B_HEADS, HEAD_DIM)
    k = k.reshape(b, t, B_KV_HEADS, HEAD_DIM)
    v = v.reshape(b, t, B_KV_HEADS, HEAD_DIM)
    qi = qi.reshape(b, t, IDX_HEADS, IDX_DIM)
    n_pages = page_table.shape[1]
    past = n_pages * PAGE_SIZE
    topk = min(IDX_TOPK_MAX, (past + t) // 4)
    pps = SAMPLE_PAGES_PER_STEP
    assert n_pages % pps == 0 and t <= SAMPLE_T_PAD
    n_steps = n_pages // pps + 1
    width = n_steps * pps * PAGE_SIZE

    qi_r = jnp.pad(jnp.swapaxes(qi, 1, 2), ((0, 0), (0, 0), (0, SAMPLE_T_PAD - t), (0, 0)))
    qi_hi, qi_lo = _hi_lo(qi_r.reshape(b, IDX_HEADS * SAMPLE_T_PAD, IDX_DIM))
    wi_r = jnp.pad(jnp.swapaxes(wi, 1, 2), ((0, 0), (0, 0), (0, SAMPLE_T_PAD - t)))
    wi_r = wi_r.reshape(b, IDX_HEADS * SAMPLE_T_PAD, 1)
    kidx_new = jnp.pad(kidx, ((0, 0), (0, PAGE_SIZE - t), (0, 0)))

    def page_spec(i, *tail):
        def imap(bi, j, pt):
            return (layer, pt[bi, jnp.minimum(j * pps + i, n_pages - 1)]) + (0,) * (1 + len(tail))
        return pl.BlockSpec((None, None, PAGE_SIZE) + tail, imap)

    def per_b(shape):
        return pl.BlockSpec((None,) + shape, lambda bi, j, pt: (bi,) + (0,) * len(shape))

    score = pl.pallas_call(
        functools.partial(_dsa_sample_score_kernel, n_steps=n_steps),
        grid_spec=pltpu.PrefetchScalarGridSpec(
            num_scalar_prefetch=1, grid=(b, n_steps),
            in_specs=[per_b((IDX_HEADS * SAMPLE_T_PAD, IDX_DIM)), per_b((IDX_HEADS * SAMPLE_T_PAD, IDX_DIM)),
                      per_b((IDX_HEADS * SAMPLE_T_PAD, 1))]
                     + [page_spec(i, IDX_DIM) for i in range(pps)] + [per_b((PAGE_SIZE, IDX_DIM))],
            out_specs=pl.BlockSpec((None, SAMPLE_T_PAD, pps * PAGE_SIZE), lambda bi, j, pt: (bi, 0, j))),
        out_shape=jax.ShapeDtypeStruct((b, SAMPLE_T_PAD, width), F32),
        compiler_params=pltpu.CompilerParams(dimension_semantics=("arbitrary", "arbitrary")),
        name="dsa_sample_score",
    )(page_table, qi_hi, qi_lo, wi_r, *([cache_kidx] * pps), kidx_new)

    group = B_HEADS // B_KV_HEADS
    q_r = q.reshape(b, t, B_KV_HEADS, group, HEAD_DIM)
    q_r = jnp.transpose(q_r, (0, 2, 3, 1, 4)).reshape(b, B_KV_HEADS, group * t, HEAD_DIM).astype(BF16)
    k_new = jnp.pad(k, ((0, 0), (0, PAGE_SIZE - t), (0, 0), (0, 0)))
    v_new = jnp.pad(v, ((0, 0), (0, PAGE_SIZE - t), (0, 0), (0, 0)))
    return pl.pallas_call(
        functools.partial(_dsa_sample_attn_kernel, n_steps=n_steps, topk=topk, past=past, t_len=t),
        grid_spec=pltpu.PrefetchScalarGridSpec(
            num_scalar_prefetch=1, grid=(b, n_steps),
            in_specs=[per_b((SAMPLE_T_PAD, width)), per_b((B_KV_HEADS, group * t, HEAD_DIM))]
                     + [page_spec(i, B_KV_HEADS, HEAD_DIM) for i in range(pps)] * 2
                     + [per_b((PAGE_SIZE, B_KV_HEADS, HEAD_DIM))] * 2,
            out_specs=pl.BlockSpec((None, t, B_HEADS * HEAD_DIM), lambda bi, j, pt: (bi, 0, 0)),
            scratch_shapes=[pltpu.VMEM((SAMPLE_T_PAD, width), F32),
                            pltpu.VMEM((B_KV_HEADS, group * t, 1), F32),
                            pltpu.VMEM((B_KV_HEADS, group * t, 1), F32),
                            pltpu.VMEM((B_KV_HEADS, group * t, HEAD_DIM), F32)]),
        out_shape=jax.ShapeDtypeStruct((b, t, B_HEADS * HEAD_DIM), F32),
        compiler_params=pltpu.CompilerParams(dimension_semantics=("arbitrary", "arbitrary")),
        name="dsa_sample_attn",
    )(page_table, score, q_r, *([cache_k] * pps), *([cache_v] * pps), k_new, v_new)


DIL_WM = 128


def _dilated_prompt_kernel(q_ref, kp_ref, kc_ref, vp_ref, vc_ref, o_ref, lse_ref, *, dil, heads):
    n = pl.program_id(1)
    iq = lax.broadcasted_iota(I32, (DIL_WM, 2 * DIL_WM), 0)
    ik = lax.broadcasted_iota(I32, (DIL_WM, 2 * DIL_WM), 1)
    dist = iq + DIL_WM - ik
    ok = jnp.logical_and(dist >= 0, dist <= DIL_WM)
    ok = jnp.logical_and(ok, jnp.logical_or(ik >= DIL_WM, n > 0))
    bias = jnp.where(ok, 0.0, -jnp.inf)
    for r in range(dil):
        def rows(ref, hs):
            if dil == 1:
                return ref[:, hs]
            return ref[pl.ds(r, DIL_WM, stride=dil), :]

        for h in range(heads):
            hs = slice(h * HEAD_DIM, (h + 1) * HEAD_DIM)
            q = rows(q_ref, hs).astype(BF16)
            k = jnp.concatenate([rows(kp_ref, hs), rows(kc_ref, hs)], axis=0).astype(BF16)
            v = jnp.concatenate([rows(vp_ref, hs), rows(vc_ref, hs)], axis=0).astype(BF16)
            s = lax.dot_general(q, k, _NT, preferred_element_type=F32) * HEAD_DIM ** -0.5 + bias
            m = jnp.max(s, axis=1, keepdims=True)
            p = jnp.exp(s - m)
            denom = jnp.sum(p, axis=1, keepdims=True)
            o = jnp.dot(p.astype(BF16), v, preferred_element_type=F32) / denom
            lse = jnp.broadcast_to(m + jnp.log(denom), (DIL_WM, HEAD_DIM))
            if dil == 1:
                o_ref[:, hs] = o
                lse_ref[:, hs] = lse
            else:
                o_ref[pl.ds(r, DIL_WM, stride=dil), :] = o
                lse_ref[pl.ds(r, DIL_WM, stride=dil), :] = lse


def dilated_prompt_attend(q, k, v_src, v_col0, window, dil):
    b, s_len, width = q.shape
    assert window // dil == DIL_WM and s_len % (DIL_WM * dil) == 0
    rows = DIL_WM * dil
    heads = C_HEADS if dil == 1 else 1
    bw = heads * HEAD_DIM
    assert v_col0 % bw == 0
    vb = v_col0 // bw
    cur = pl.BlockSpec((None, rows, bw), lambda bi, n, hi: (bi, n, hi))
    prev = pl.BlockSpec((None, rows, bw), lambda bi, n, hi: (bi, jnp.maximum(n - 1, 0), hi))
    vcur = pl.BlockSpec((None, rows, bw), lambda bi, n, hi: (bi, n, vb + hi))
    vprev = pl.BlockSpec((None, rows, bw), lambda bi, n, hi: (bi, jnp.maximum(n - 1, 0), vb + hi))
    return pl.pallas_call(
        functools.partial(_dilated_prompt_kernel, dil=dil, heads=heads),
        grid=(b, s_len // rows, width // bw),
        in_specs=[cur, prev, cur, vprev, vcur],
        out_specs=[cur, cur],
        out_shape=[jax.ShapeDtypeStruct(q.shape, F32)] * 2,
        compiler_params=pltpu.CompilerParams(dimension_semantics=("arbitrary",) * 3),
        name="dilated_prompt",
    )(q, k, k, v_src, v_src)


NORM_ROPE_ROWS = 256


def rope_tables(pos):
    half = ROT_DIM // 2
    freqs = ROPE_THETA ** (-jnp.arange(half, dtype=F32) / half)
    ang = pos.astype(F32)[:, None] * freqs[None, :]
    cos, sin = jnp.cos(ang), jnp.sin(ang)
    t = pos.shape[0]
    ones = jnp.ones((t, HEAD_DIM - ROT_DIM), F32)
    zeros = jnp.zeros((t, HEAD_DIM - ROT_DIM), F32)
    c = jnp.concatenate([cos, cos, ones], axis=1)
    s_dn = jnp.concatenate([-sin, jnp.zeros_like(sin), zeros], axis=1)
    s_up = jnp.concatenate([jnp.zeros_like(sin), sin, zeros], axis=1)
    return c, s_dn, s_up


def _norm_rope_kernel(x_ref, g_ref, b_ref, c_ref, sd_ref, su_ref, o_ref, *, heads, norm):
    half = ROT_DIM // 2
    c, sd, su = c_ref[...], sd_ref[...], su_ref[...]
    for h in range(heads):
        hs = slice(h * HEAD_DIM, (h + 1) * HEAD_DIM)
        x = x_ref[:, hs]
        if norm == "rms":
            x = x * lax.rsqrt(jnp.mean(x * x, axis=1, keepdims=True) + NORM_EPS) * g_ref[...]
        elif norm == "layer":
            xc = x - jnp.mean(x, axis=1, keepdims=True)
            x = xc * lax.rsqrt(jnp.mean(xc * xc, axis=1, keepdims=True) + NORM_EPS) * g_ref[...] + b_ref[...]
        dn = pltpu.roll(x, HEAD_DIM - half, axis=1)
        up = pltpu.roll(x, half, axis=1)
        o_ref[:, hs] = x * c + dn * sd + up * su


def norm_rope(x, col0, heads, gain, bias, tables, t_len, norm):
    r, _ = x.shape
    bw = heads * HEAD_DIM
    assert col0 % bw == 0
    if t_len % 8 == 0:
        rows = min(NORM_ROPE_ROWS, t_len)
    else:
        rows, tables = r, tuple(jnp.tile(t, (r // t_len, 1)) for t in tables)
        t_len = r
    assert t_len % rows == 0 and r % rows == 0
    tb = t_len // rows
    tab = pl.BlockSpec((rows, HEAD_DIM), lambda i: (i % tb, 0))
    vec = pl.BlockSpec((1, HEAD_DIM), lambda i: (0, 0))
    g = (jnp.ones((HEAD_DIM,), F32) if gain is None else gain).reshape(1, HEAD_DIM).astype(F32)
    b = (jnp.zeros((HEAD_DIM,), F32) if bias is None else bias).reshape(1, HEAD_DIM).astype(F32)
    return pl.pallas_call(
        functools.partial(_norm_rope_kernel, heads=heads, norm=norm),
        grid=(r // rows,),
        in_specs=[pl.BlockSpec((rows, bw), lambda i: (i, col0 // bw)), vec, vec, tab, tab, tab],
        out_specs=pl.BlockSpec((rows, bw), lambda i: (i, 0)),
        out_shape=jax.ShapeDtypeStruct((r, bw), F32),
        compiler_params=pltpu.CompilerParams(dimension_semantics=("arbitrary",)),
        name="norm_rope",
    )(x, g, b, *tables)


def _rms(x, eps=NORM_EPS):
    xf = x.astype(F32)
    return xf * lax.rsqrt(jnp.mean(xf * xf, axis=-1, keepdims=True) + eps)


def rms_norm(x, g):
    return (_rms(x) * g.astype(F32)).astype(x.dtype)


def rwkv7_mix(sh, prev_row, s0, mu, w0, w_lora, a0, a_lora, k_k, k_a, r_k, gn_g, gn_b):
    bn, t, _ = sh.shape
    prev = jnp.concatenate([prev_row[:, None, :].astype(sh.dtype), sh[:, :-1]], axis=1)
    xm = sh + (prev - sh) * mu
    r, k, v = (xm[..., j * A_WIDTH:(j + 1) * A_WIDTH] for j in range(3))
    xw = xm[..., 3 * A_WIDTH:3 * A_WIDTH + DECAY_LORA]
    xa = xm[..., 3 * A_WIDTH + DECAY_LORA:]
    wlog = -jax.nn.softplus(-(w0 + mm(jnp.tanh(xw), w_lora))) - 0.5
    decay = jnp.exp(-jnp.exp(wlog.astype(F32)))
    a = jax.nn.sigmoid((a0 + mm(xa, a_lora)).astype(F32))

    def heads(z):
        return z.astype(F32).reshape(bn, t, A_HEADS, A_HEAD)

    def head_sum(z):
        return jnp.broadcast_to(jnp.sum(heads(z), axis=-1, keepdims=True),
                                (bn, t, A_HEADS, A_HEAD)).reshape(bn, t, A_WIDTH)

    kk = heads(k * k_k)
    kk = kk / jnp.maximum(jnp.sqrt(jnp.sum(kk * kk, axis=-1, keepdims=True)), 1e-12)
    kk = kk.reshape(bn, t, A_WIDTH)
    k2 = k * (1.0 + (a - 1.0) * k_a.astype(F32))
    kka = kk * a
    q = decay * r - kk * head_sum(kka * r)
    vc = v * head_sum(k2 * r)

    y, s_fin = rwkv_scan((-kk, decay, kka, k2, v, q, vc), s0.astype(F32))
    y = heads(y)
    ym = jnp.mean(y, axis=-1, keepdims=True)
    yc = y - ym
    yn = yc * lax.rsqrt(jnp.mean(yc * yc, axis=-1, keepdims=True) + GN_EPS)
    yn = yn.reshape(bn, t, A_WIDTH) * gn_g.astype(F32) + gn_b.astype(F32)
    bonus = (jnp.sum(heads(r * k2) * r_k.astype(F32), axis=-1, keepdims=True) * heads(v)).reshape(bn, t, A_WIDTH)
    return (yn + bonus).astype(sh.dtype), s_fin.astype(s0.dtype), sh[:, -1]


KV_WIDTH = B_KV_HEADS * HEAD_DIM
E_COLS = {}
_acc = 0
for _name, _w, _pad in (('q', B_WIDTH, 0), ('g_a', A_WIDTH, 0), ('g_b', B_WIDTH, 0), ('rkv', 3 * A_WIDTH, 0),
                        ('k', KV_WIDTH, 0), ('v', KV_WIDTH, 0), ('cqi', IDX_Q_RANK, 0), ('kidx', IDX_DIM, 0),
                        ('wi', IDX_HEADS, V7X_LANES - IDX_HEADS),
                        ('lora', DECAY_LORA + AAA_LORA, 2 * V7X_LANES - DECAY_LORA - AAA_LORA)):
    E_COLS[_name] = (_acc, _w)
    _acc += _w + _pad
E_WIDTH = _acc


def repack_even_w_in(w_all, layer):
    d_in = w_all.shape[1]
    src = {'rkv': 0, 'lora': 3 * A_WIDTH, 'g_a': SHIFT_W, 'q': IN_A, 'k': IN_A + B_WIDTH,
           'v': IN_A + B_WIDTH + KV_WIDTH, 'cqi': IN_A + B_WIDTH + 2 * KV_WIDTH}
    src['kidx'] = src['cqi'] + IDX_Q_RANK
    src['wi'] = src['kidx'] + IDX_DIM
    src['g_b'] = src['wi'] + IDX_HEADS
    parts, pos = [], 0
    for name, (start, width) in E_COLS.items():
        if start > pos:
            parts.append(jnp.zeros((d_in, start - pos), w_all.dtype))
        parts.append(w_all[layer, :, src[name]:src[name] + width])
        pos = start + width
    if E_WIDTH > pos:
        parts.append(jnp.zeros((d_in, E_WIDTH - pos), w_all.dtype))
    return jnp.concatenate(parts, axis=1)


def _cols(u, name):
    start, width = E_COLS[name]
    return u[:, start:start + width]


def even_mixer(u, bn, t, pos, prev_row, s0, attend, ep):
    sh = jnp.concatenate([_cols(u, 'rkv'), _cols(u, 'lora')], axis=1).reshape(bn, t, SHIFT_W)
    y_a, s_fin, last_row = rwkv7_mix(sh, prev_row, s0, ep['shift_mu'], ep['w0'], ep['w_lora'],
                                     ep['a0'], ep['a_lora'], ep['k_k'], ep['k_a'], ep['r_k'],
                                     ep['gn_gain'], ep['gn_bias'])
    tabs = rope_tables(pos)
    q = norm_rope(u, E_COLS['q'][0], B_HEADS, ep['q_norm'], None, tabs, t, "rms")
    k = norm_rope(u, E_COLS['k'][0], B_KV_HEADS, ep['k_norm'], None, tabs, t, "rms")
    v = _cols(u, 'v')
    qi = norm_rope(matmul(rms_norm(_cols(u, 'cqi'), ep['qi_norm']).astype(BF16), ep['w_qi']),
                   0, IDX_HEADS, None, None, tabs, t, "none")
    kidx = norm_rope(u, E_COLS['kidx'][0], 1, ep['kidx_gain'], ep['kidx_bias'], tabs, t, "layer")
    wi = _cols(u, 'wi') * IDX_HEADS ** -0.5

    def seq(z):
        return z.reshape(bn, t, -1)

    y_b = attend(seq(q), seq(k), seq(v), seq(qi), seq(wi), seq(kidx))
    y = jnp.concatenate([y_a * jax.nn.silu(seq(_cols(u, 'g_a'))), y_b * jax.nn.silu(seq(_cols(u, 'g_b')))],
                        axis=-1)
    heads4 = (bn, t, B_KV_HEADS, HEAD_DIM)
    y = y.reshape(bn * t, -1).astype(BF16)
    return y, (s_fin, last_row, k.reshape(heads4), v.reshape(heads4), seq(kidx))


def dilated_sample(q, k, v, buf_k, buf_v, window, dil):
    t = q.shape[1]
    d = q.shape[-1]
    wb = buf_k.shape[1]
    wm = window // dil
    kc = jnp.concatenate([buf_k.astype(k.dtype), k], axis=1)
    vc = jnp.concatenate([buf_v.astype(v.dtype), v], axis=1)
    idx = wb + jnp.arange(t)[:, None] - jnp.arange(wm + 1)[None, :] * dil
    valid = idx >= 0
    idxc = jnp.maximum(idx, 0)
    kg, vg = kc[:, idxc], vc[:, idxc]
    s = jnp.einsum('bthd,btjhd->bthj', q, kg).astype(F32) * d ** -0.5
    s = jnp.where(valid[None, :, None, :], s, -jnp.inf)
    lse = jax.nn.logsumexp(s, axis=-1)
    p = jnp.exp(s - lse[..., None])
    o = jnp.einsum('bthj,btjhd->bthd', p.astype(vg.dtype), vg)
    return o, lse, kc[:, -wb:], vc[:, -wb:]


def odd_mixer(u, bn, t, pos, sample_bufs, op):
    n_g = len(C_GROUPS)
    tabs = rope_tables(pos)
    heads4 = (bn, t, C_HEADS, HEAD_DIM)
    outs, lses, bufs = [], [], []
    for g, (win, dil) in enumerate(C_GROUPS):
        q = norm_rope(u, (3 * g) * C_WIDTH, C_HEADS, op['q_norm'][g], None, tabs, t, "rms")
        k = norm_rope(u, (3 * g + 1) * C_WIDTH, C_HEADS, op['k_norm'][g], None, tabs, t, "rms")
        v_col0 = (3 * g + 2) * C_WIDTH
        v = u[:, v_col0:v_col0 + C_WIDTH].reshape(heads4)
        if sample_bufs is None:
            o, lse = dilated_prompt_attend(q.reshape(bn, t, C_WIDTH), k.reshape(bn, t, C_WIDTH),
                                           u.reshape(bn, t, -1), v_col0, win, dil)
            keep = min(win, t)
            kb, vb = k.reshape(heads4)[:, -keep:], v[:, -keep:]
        else:
            o, lse, kb, vb = dilated_sample(q.reshape(heads4), k.reshape(heads4), v,
                                            sample_bufs[g][0], sample_bufs[g][1], win, dil)
            o = o.reshape(bn, t, C_WIDTH)
            lse = jnp.broadcast_to(lse[..., None], heads4).reshape(bn, t, C_WIDTH)
        outs.append(o)
        lses.append(lse)
        bufs += [kb, vb]
    alpha = jax.nn.softmax(jnp.stack(lses, axis=0), axis=0)
    o = jnp.sum(alpha * jnp.stack(outs, axis=0), axis=0)
    gate = u[:, 3 * n_g * C_WIDTH:].reshape(bn, t, C_WIDTH)
    return (o * jax.nn.silu(gate)).reshape(bn * t, C_WIDTH).astype(BF16), tuple(bufs)


def _ple_rows(h, m, p_l):
    return (_rms(h + m).astype(BF16), p_l.reshape(-1, p_l.shape[-1]).astype(BF16), h, m)


def kernel(x_prompt, x_sample, p_prompt, p_sample, state_wkv, state_shift, cache_k, cache_v,
           cache_kidx, page_table, cache_win_k0, cache_win_v0, cache_win_k1, cache_win_v1,
           cache_win_k2, cache_win_v2, ln_gain, e_w_in, e_shift_mu, e_w0, e_w_lora, e_a0,
           e_a_lora, e_k_k, e_k_a, e_r_k, e_gn_gain, e_gn_bias, e_q_norm, e_k_norm, e_qi_norm,
           e_w_qi, e_kidx_gain, e_kidx_bias, e_w_out, o_w_in, o_q_norm, o_k_norm, o_w_out,
           ple_w_proj, ple_w_gate):
    depth = ln_gain.shape[0]
    bp, s_len, _ = x_prompt.shape
    t_len = x_sample.shape[1]
    past = page_table.shape[1] * PAGE_SIZE
    pos_p = jnp.arange(s_len, dtype=jnp.int32)
    pos_s = past + jnp.arange(t_len, dtype=jnp.int32)
    bufs_k = (cache_win_k0, cache_win_k1, cache_win_k2)
    bufs_v = (cache_win_v0, cache_win_v1, cache_win_v2)
    bs = x_sample.shape[0]
    hp, hs = x_prompt.reshape(bp * s_len, -1), x_sample.reshape(bs * t_len, -1)
    ev_p, ev_s, od_p, od_s = [], [], [], []
    for i in range(depth):
        l = i // 2
        xp = rms_norm(hp, ln_gain[i]).astype(BF16)
        xs = rms_norm(hs, ln_gain[i]).astype(BF16)
        if i % 2 == 0:
            ep = {'shift_mu': e_shift_mu[l], 'w0': e_w0[l], 'w_lora': e_w_lora[l],
                  'a0': e_a0[l], 'a_lora': e_a_lora[l], 'k_k': e_k_k[l], 'k_a': e_k_a[l],
                  'r_k': e_r_k[l], 'gn_gain': e_gn_gain[l], 'gn_bias': e_gn_bias[l],
                  'q_norm': e_q_norm[l], 'k_norm': e_k_norm[l], 'qi_norm': e_qi_norm[l],
                  'w_qi': e_w_qi[l], 'kidx_gain': e_kidx_gain[l], 'kidx_bias': e_kidx_bias[l]}
            up, us = matmul2(xp, xs, repack_even_w_in(e_w_in, l))
            row0 = jnp.zeros((bp, SHIFT_W), F32)
            st0 = jnp.zeros((bp, A_HEADS, A_HEAD, A_HEAD), F32)
            yp, stp = even_mixer(up, bp, s_len, pos_p, row0, st0, dsa_prompt_attend, ep)
            att_s = functools.partial(dsa_sample_attend, cache_k=cache_k, cache_v=cache_v,
                                      cache_kidx=cache_kidx, page_table=page_table, layer=l)
            ys, sts = even_mixer(us, bs, t_len, pos_s, state_shift[l], state_wkv[l], att_s, ep)
            mp, ms = matmul2(yp, ys, e_w_out[l])
            ev_p.append(stp)
            ev_s.append(sts)
        else:
            op = {'q_norm': o_q_norm[l], 'k_norm': o_k_norm[l]}
            up, us = matmul2(xp, xs, o_w_in[l])
            yp, stp = odd_mixer(up, bp, s_len, pos_p, None, op)
            sample_bufs = [(bk[l], bv[l]) for bk, bv in zip(bufs_k, bufs_v)]
            ys, sts = odd_mixer(us, bs, t_len, pos_s, sample_bufs, op)
            mp, ms = matmul2(yp, ys, o_w_out[l])
            od_p.append(stp)
            od_s.append(sts)
        hp, hs = ple_update(ple_w_gate[i], ple_w_proj[i], _ple_rows(hp, mp, p_prompt[i]),
                            _ple_rows(hs, ms, p_sample[i]))
    hp, hs = hp.reshape(x_prompt.shape), hs.reshape(x_sample.shape)

    def st(lst, j):
        if len(lst) == 1:
            return lst[0][j][None]
        return jnp.stack([e[j] for e in lst], axis=0)

    outs = [hp, hs, st(ev_p, 0), st(ev_s, 0), st(ev_p, 1), st(ev_s, 1)]
    outs += [st(ev_p, j) for j in (2, 3, 4)] + [st(ev_s, j) for j in (2, 3, 4)]
    outs += [st(od_p, j) for j in range(6)] + [st(od_s, j) for j in range(6)]
    return tuple(outs)
```

```python
import functools

import jax
import jax.numpy as jnp
from jax import lax
from jax.experimental import pallas as pl
from jax.experimental.pallas import tpu as pltpu

F32 = jnp.float32
BF16 = jnp.bfloat16
I32 = jnp.int32

D_MODEL = 4096
PAGE_SIZE = 128
HEAD_DIM = 128
ROT_DIM = HEAD_DIM // 4
ROPE_THETA = 500000.0
NORM_EPS = 1e-6

A_WIDTH = D_MODEL // 2
A_HEAD = 64
A_HEADS = A_WIDTH // A_HEAD
DECAY_LORA = 96
AAA_LORA = 96
GN_EPS = 64e-5
SHIFT_W = 3 * A_WIDTH + DECAY_LORA + AAA_LORA

B_WIDTH = D_MODEL // 2
B_HEADS = B_WIDTH // HEAD_DIM
B_KV_HEADS = 4
IDX_HEADS = 16
IDX_DIM = 128
IDX_Q_RANK = 512
IDX_TOPK_MAX = 256
QBLOCK = 128

C_GROUPS = ((128, 1), (512, 4), (2048, 16))
C_HEADS = 16
C_WIDTH = C_HEADS * HEAD_DIM

IN_A = SHIFT_W + A_WIDTH
IN_B = B_WIDTH + 2 * B_KV_HEADS * HEAD_DIM + IDX_Q_RANK + IDX_DIM + IDX_HEADS + B_WIDTH

V7X_LANES = 128
V7X_VMEM_LIMIT_BYTES = 58 * 1024 * 1024
INT_MIN = -2 ** 31

_NT = (((1,), (1,)), ((), ()))


def _matmul_kernel(a_ref, b_ref, o_ref, bq_ref):
    @pl.when(pl.program_id(1) == 0)
    def _():
        bq_ref[...] = b_ref[...].astype(BF16)

    o_ref[...] = jnp.dot(a_ref[...].astype(BF16), bq_ref[...], preferred_element_type=F32)


def _pick_tile(n, cands):
    for c in cands:
        if n % c == 0:
            return c
    return n


def matmul(a, b):
    m, k = a.shape
    _, n = b.shape
    n_pad = -(-n // V7X_LANES) * V7X_LANES
    if n_pad != n:
        b = jnp.pad(b, ((0, 0), (0, n_pad - n)))
    tn = _pick_tile(n_pad, (1024, 512, 256, 128))
    tm = _pick_tile(m, (512, 256, 128, 64, 32, 16, 8))
    out = pl.pallas_call(
        _matmul_kernel,
        grid=(n_pad // tn, m // tm),
        in_specs=[pl.BlockSpec((tm, k), lambda j, i: (i, 0)),
                  pl.BlockSpec((k, tn), lambda j, i: (0, j))],
        out_specs=pl.BlockSpec((tm, tn), lambda j, i: (i, j)),
        out_shape=jax.ShapeDtypeStruct((m, n_pad), F32),
        scratch_shapes=[pltpu.VMEM((k, tn), BF16)],
        compiler_params=pltpu.CompilerParams(
            dimension_semantics=("arbitrary", "arbitrary"),
            vmem_limit_bytes=V7X_VMEM_LIMIT_BYTES),
        name="matmul",
    )(a, b)
    return out[:, :n] if n_pad != n else out


def mm(x, w):
    lead = x.shape[:-1]
    return matmul(x.reshape(-1, x.shape[-1]).astype(BF16), w).reshape(lead + (w.shape[-1],))


def _matmul2_kernel(a_ref, as_ref, b_ref, o_ref, os_ref, bq_ref):
    @pl.when(pl.program_id(1) == 0)
    def _():
        bq_ref[...] = b_ref[...].astype(BF16)
        os_ref[...] = jnp.dot(as_ref[...], bq_ref[...], preferred_element_type=F32)

    o_ref[...] = jnp.dot(a_ref[...], bq_ref[...], preferred_element_type=F32)


def matmul2(a, a_s, b):
    m, k = a.shape
    ms = a_s.shape[0]
    n = b.shape[1]
    tn = _pick_tile(n, (1024, 512, 256, 128))
    tm = _pick_tile(m, (512, 256, 128, 64, 32, 16, 8))
    assert n % tn == 0 and m % tm == 0
    return pl.pallas_call(
        _matmul2_kernel,
        grid=(n // tn, m // tm),
        in_specs=[pl.BlockSpec((tm, k), lambda j, i: (i, 0)),
                  pl.BlockSpec((ms, k), lambda j, i: (0, 0)),
                  pl.BlockSpec((k, tn), lambda j, i: (0, j))],
        out_specs=[pl.BlockSpec((tm, tn), lambda j, i: (i, j)),
                   pl.BlockSpec((ms, tn), lambda j, i: (0, j))],
        out_shape=[jax.ShapeDtypeStruct((m, n), F32), jax.ShapeDtypeStruct((ms, n), F32)],
        scratch_shapes=[pltpu.VMEM((k, tn), BF16)],
        compiler_params=pltpu.CompilerParams(
            dimension_semantics=("arbitrary", "arbitrary"),
            vmem_limit_bytes=V7X_VMEM_LIMIT_BYTES),
        name="matmul2",
    )(a, a_s, b)


PLE_TM = 512
PLE_TN = 512


def _ple_kernel(wg_ref, wp_ref, a_ref, p_ref, h_ref, m_ref, as_ref, ps_ref, hs_ref, ms_ref,
                o_ref, os_ref, wgq_ref, wpq_ref):
    def update(a, p, h, m, o):
        gate = jnp.dot(a[...], wgq_ref[...], preferred_element_type=F32)
        proj = jnp.dot(p[...], wpq_ref[...], preferred_element_type=F32)
        o[...] = (h[...] + m[...]) + proj / (1.0 + jnp.exp(-gate))

    @pl.when(pl.program_id(1) == 0)
    def _():
        wgq_ref[...] = wg_ref[...].astype(BF16)
        wpq_ref[...] = wp_ref[...].astype(BF16)
        update(as_ref, ps_ref, hs_ref, ms_ref, os_ref)

    update(a_ref, p_ref, h_ref, m_ref, o_ref)


def ple_update(w_gate, w_proj, big, small):
    a, p, h, m = big
    a_s, p_s, h_s, m_s = small
    r, k = a.shape
    rs = a_s.shape[0]
    n = w_gate.shape[1]
    pk = p.shape[1]
    tm, tn = PLE_TM, PLE_TN
    assert n % tn == 0 and r % tm == 0

    def rows(width, full):
        if full:
            return pl.BlockSpec((tm, width), lambda j, i: (i, 0))
        return pl.BlockSpec((tm, tn), lambda j, i: (i, j))

    def rows_s(width, full):
        if full:
            return pl.BlockSpec((rs, width), lambda j, i: (0, 0))
        return pl.BlockSpec((rs, tn), lambda j, i: (0, j))

    return pl.pallas_call(
        _ple_kernel,
        grid=(n // tn, r // tm),
        in_specs=[pl.BlockSpec((k, tn), lambda j, i: (0, j)), pl.BlockSpec((pk, tn), lambda j, i: (0, j)),
                  rows(k, True), rows(pk, True), rows(tn, False), rows(tn, False),
                  rows_s(k, True), rows_s(pk, True), rows_s(tn, False), rows_s(tn, False)],
        out_specs=[rows(tn, False), rows_s(tn, False)],
        out_shape=[jax.ShapeDtypeStruct((r, n), F32), jax.ShapeDtypeStruct((rs, n), F32)],
        scratch_shapes=[pltpu.VMEM((k, tn), BF16), pltpu.VMEM((pk, tn), BF16)],
        compiler_params=pltpu.CompilerParams(dimension_semantics=("arbitrary", "arbitrary"),
                                             vmem_limit_bytes=V7X_VMEM_LIMIT_BYTES),
        name="ple_update",
    )(w_gate, w_proj, a, p, h, m, a_s, p_s, h_s, m_s)


RWKV_PAIRS_PER_STEP = 8
RWKV_UNROLL = 8
RWKV_ROW_INPUTS = 7


def _split_bf16(x):
    hi = x.astype(BF16)
    lo = (x - hi.astype(F32)).astype(BF16)
    return jnp.concatenate([hi, lo], axis=1)


def _rwkv_scan_kernel(*refs, pairs, steps, use_mxu):
    rows = [refs[j * pairs:(j + 1) * pairs] for j in range(RWKV_ROW_INPUTS)]
    nkk_r, w_r, kka_r, k_r, v_r, q_r, vc_r = rows
    s0_ref, y_ref, sout_ref, s_scr, yt_scr = refs[RWKV_ROW_INPUTS * pairs:]
    tchunk = pl.program_id(2)

    @pl.when(tchunk == 0)
    def _():
        for p in range(pairs):
            s_scr[p] = jnp.concatenate([s0_ref[0, 2 * p], s0_ref[0, 2 * p + 1]], axis=1)

    lane = lax.broadcasted_iota(I32, (A_HEAD, V7X_LANES), 1)
    row = lax.broadcasted_iota(I32, (A_HEAD, V7X_LANES), 0)
    lo = lane < A_HEAD
    eye_lo = lane == row
    eye_hi = lane == row + A_HEAD
    eye = jnp.logical_or(eye_lo, eye_hi)
    lane_t = lax.broadcasted_iota(I32, (A_HEAD, steps), 1)
    yt_scr[...] = jnp.zeros(yt_scr.shape, F32)
    if use_mxu:
        kk_i = lax.broadcasted_iota(I32, (2 * V7X_LANES, 2 * V7X_LANES), 0)
        nn_i = lax.broadcasted_iota(I32, (2 * V7X_LANES, 2 * V7X_LANES), 1)
        seg_mat = jnp.where(((kk_i % V7X_LANES) >= A_HEAD) == (nn_i >= V7X_LANES), 1.0, 0.0).astype(BF16)
        eye_bf = jnp.where(eye, 1.0, 0.0).astype(BF16)
        zero_bf = jnp.zeros((A_HEAD, V7X_LANES), BF16)

    def seg_sum(x):
        s_lo = jnp.sum(jnp.where(lo, x, 0.0), axis=1, keepdims=True)
        s_hi = jnp.sum(jnp.where(lo, 0.0, x), axis=1, keepdims=True)
        return s_lo, s_hi

    def step(t, carry):
        for p in range(pairs):
            def rowvec(group):
                return jnp.broadcast_to(group[p][pl.ds(t, 1), :], (A_HEAD, V7X_LANES))

            s = s_scr[p]
            sa_lo, sa_hi = seg_sum(s * rowvec(nkk_r))
            py = s * rowvec(q_r) + jnp.where(eye, rowvec(vc_r), 0.0)
            r0 = p * V7X_LANES
            if use_mxu:
                vrow = v_r[p][pl.ds(t, 1), :]
                vh = vrow.astype(BF16)
                vl = (vrow - vh.astype(F32)).astype(BF16)
                vd = jnp.concatenate([eye_bf * vh, eye_bf * vl], axis=1)
                py2 = jnp.concatenate([py.astype(BF16), zero_bf], axis=1)
                lhs = jnp.concatenate([py2, vd], axis=0)
                yv = jnp.dot(lhs, seg_mat, preferred_element_type=F32)
                y_lo, y_hi = yv[:A_HEAD, :V7X_LANES], yv[:A_HEAD, V7X_LANES:]
                v_b = jnp.where(lo, yv[A_HEAD:, :V7X_LANES], yv[A_HEAD:, V7X_LANES:])
            else:
                y_lo, y_hi = seg_sum(py)
                vrow = rowvec(v_r)
                v_lo = jnp.sum(jnp.where(eye_lo, vrow, 0.0), axis=1, keepdims=True)
                v_hi = jnp.sum(jnp.where(eye_hi, vrow, 0.0), axis=1, keepdims=True)
                v_b = jnp.where(lo, v_lo, v_hi)
            sa_b = jnp.where(lo, sa_lo, sa_hi)
            s_scr[p] = s * rowvec(w_r) + sa_b * rowvec(kka_r) + v_b * rowvec(k_r)
            yt_scr[r0:r0 + A_HEAD, :] = jnp.where(lane_t == t, y_lo, yt_scr[r0:r0 + A_HEAD, :])
            yt_scr[r0 + A_HEAD:r0 + V7X_LANES, :] = jnp.where(
                lane_t == t, y_hi, yt_scr[r0 + A_HEAD:r0 + V7X_LANES, :])
        return carry

    lax.fori_loop(0, steps, step, 0, unroll=RWKV_UNROLL if steps % RWKV_UNROLL == 0 else 1)
    for p in range(pairs):
        y_ref[:, p * V7X_LANES:(p + 1) * V7X_LANES] = yt_scr[p * V7X_LANES:(p + 1) * V7X_LANES, :].T

    @pl.when(tchunk == pl.num_programs(2) - 1)
    def _():
        for p in range(pairs):
            s = s_scr[p]
            sout_ref[0, 2 * p] = s[:, :A_HEAD]
            sout_ref[0, 2 * p + 1] = s[:, A_HEAD:]


def rwkv_scan(row_inputs, s0):
    b, t, c = row_inputs[0].shape
    pairs = RWKV_PAIRS_PER_STEP
    npairs = c // V7X_LANES
    tc = min(t, V7X_LANES)
    assert t % tc == 0 and npairs % pairs == 0 and len(row_inputs) == RWKV_ROW_INPUTS
    bw = pairs * V7X_LANES

    def pair_spec(p):
        return pl.BlockSpec((None, tc, V7X_LANES), lambda bi, hi, ti: (bi, ti, hi * pairs + p))

    st_spec = pl.BlockSpec((1, 2 * pairs, A_HEAD, A_HEAD), lambda bi, hi, ti: (bi, hi, 0, 0))
    operands = [x for x in row_inputs for _ in range(pairs)]
    return pl.pallas_call(
        functools.partial(_rwkv_scan_kernel, pairs=pairs, steps=tc, use_mxu=(tc == V7X_LANES)),
        grid=(b, npairs // pairs, t // tc),
        in_specs=[pair_spec(p) for _ in range(RWKV_ROW_INPUTS) for p in range(pairs)] + [st_spec],
        out_specs=[pl.BlockSpec((None, tc, bw), lambda bi, hi, ti: (bi, ti, hi)), st_spec],
        out_shape=[jax.ShapeDtypeStruct((b, t, c), F32), jax.ShapeDtypeStruct(s0.shape, F32)],
        scratch_shapes=[pltpu.VMEM((pairs, A_HEAD, V7X_LANES), F32),
                        pltpu.VMEM((bw, tc), F32)],
        compiler_params=pltpu.CompilerParams(dimension_semantics=("arbitrary", "arbitrary", "arbitrary")),
        name="rwkv_scan",
    )(*operands, s0)


def _select_topk_mask(score, allowed, topk):
    r, l = score.shape
    score = jnp.where(score == 0.0, 0.0, score)
    bits = pltpu.bitcast(score, I32)
    key = jnp.where(bits < 0, bits ^ jnp.int32(0x7FFFFFFF), bits)
    key = jnp.where(allowed, key, jnp.int32(INT_MIN))
    kf = jnp.float32(topk)

    def count(pred):
        return jnp.sum(jnp.where(pred, 1.0, 0.0), axis=1, keepdims=True)

    def bit_step(i, prefix):
        cand = prefix | lax.shift_left(jnp.int32(1), jnp.int32(31) - i)
        ok = count(key >= (cand ^ jnp.int32(INT_MIN))) >= kf
        return jnp.where(ok, cand, prefix)

    prefix = lax.fori_loop(0, 32, bit_step, jnp.zeros((r, 1), I32))
    thr = prefix ^ jnp.int32(INT_MIN)
    gt = key > thr
    eq = jnp.logical_and(key == thr, allowed)
    need = kf - count(gt)
    li = lax.broadcasted_iota(I32, (V7X_LANES, 2 * V7X_LANES), 0)
    lj = lax.broadcasted_iota(I32, (V7X_LANES, 2 * V7X_LANES), 1)
    tri_ones = jnp.where(jnp.logical_or(lj >= V7X_LANES, li < lj), 1.0, 0.0).astype(BF16)
    running = jnp.zeros((r, V7X_LANES), F32)
    sel = []
    for c in range(l // V7X_LANES):
        sl = slice(c * V7X_LANES, (c + 1) * V7X_LANES)
        eq_c = eq[:, sl]
        res = jnp.dot(jnp.where(eq_c, 1.0, 0.0).astype(BF16), tri_ones, preferred_element_type=F32)
        before = res[:, :V7X_LANES] + running
        running = running + res[:, V7X_LANES:]
        sel.append(jnp.logical_or(gt[:, sl], jnp.logical_and(eq_c, before < need)))
    return jnp.concatenate(sel, axis=1)


def _dsa_prompt_kernel(qi_ref, qil_ref, wi_ref, kidx_ref, kidxl_ref, q_ref, k_ref, v_ref, o_ref, *, topk):
    qb = pl.program_id(1)
    s_len = kidx_ref.shape[0]
    group = B_HEADS // B_KV_HEADS

    def attend(l):
        kidx = kidx_ref[0:l, :]
        kidx_lo = kidxl_ref[0:l, :]
        score = jnp.zeros((QBLOCK, l), F32)
        for h in range(IDX_HEADS):
            hs = slice(h * IDX_DIM, (h + 1) * IDX_DIM)
            s = (lax.dot_general(qi_ref[:, hs], kidx, _NT, preferred_element_type=F32)
                 + lax.dot_general(qi_ref[:, hs], kidx_lo, _NT, preferred_element_type=F32)
                 + lax.dot_general(qil_ref[:, hs], kidx, _NT, preferred_element_type=F32))
            s = jnp.maximum(s * IDX_DIM ** -0.5, 0.0)
            score = score + s * wi_ref[:, h:h + 1]
        qpos = qb * QBLOCK + lax.broadcasted_iota(I32, (QBLOCK, l), 0)
        kpos = lax.broadcasted_iota(I32, (QBLOCK, l), 1)
        sel = _select_topk_mask(score, kpos <= qpos, topk)
        bias = jnp.where(sel, 0.0, -jnp.inf)
        for g in range(B_KV_HEADS):
            kg = k_ref[0:l, g * HEAD_DIM:(g + 1) * HEAD_DIM]
            vg = v_ref[0:l, g * HEAD_DIM:(g + 1) * HEAD_DIM]
            for j in range(group):
                h = g * group + j
                s = lax.dot_general(q_ref[:, h * HEAD_DIM:(h + 1) * HEAD_DIM], kg, _NT,
                                    preferred_element_type=F32)
                s = s * HEAD_DIM ** -0.5 + bias
                m = jnp.max(s, axis=1, keepdims=True)
                p = jnp.exp(s - m)
                denom = jnp.sum(p, axis=1, keepdims=True)
                o = jnp.dot(p.astype(BF16), vg, preferred_element_type=F32)
                o_ref[:, h * HEAD_DIM:(h + 1) * HEAD_DIM] = o / denom

    attend(s_len)


def dsa_prompt_attend(q, k, v, qi, wi, kidx):
    b, s_len = q.shape[:2]
    topk = min(IDX_TOPK_MAX, s_len // 4)

    def flat16(z):
        return z.reshape(b, s_len, -1).astype(BF16)

    def flat16_lo(z):
        z = z.reshape(b, s_len, -1)
        return (z - z.astype(BF16).astype(F32)).astype(BF16)

    def qspec(w):
        return pl.BlockSpec((None, QBLOCK, w), lambda bi, qb: (bi, qb, 0))

    def kspec(w):
        return pl.BlockSpec((None, s_len, w), lambda bi, qb: (bi, 0, 0))

    return pl.pallas_call(
        functools.partial(_dsa_prompt_kernel, topk=topk),
        grid=(b, s_len // QBLOCK),
        in_specs=[qspec(B_WIDTH), qspec(B_WIDTH), qspec(IDX_HEADS), kspec(IDX_DIM), kspec(IDX_DIM),
                  qspec(B_WIDTH), kspec(B_KV_HEADS * HEAD_DIM), kspec(B_KV_HEADS * HEAD_DIM)],
        out_specs=qspec(B_WIDTH),
        out_shape=jax.ShapeDtypeStruct((b, s_len, B_WIDTH), F32),
        compiler_params=pltpu.CompilerParams(dimension_semantics=("arbitrary", "arbitrary"),
                                             vmem_limit_bytes=V7X_VMEM_LIMIT_BYTES),
        name="dsa_prompt",
    )(flat16(qi), flat16_lo(qi), wi, flat16(kidx), flat16_lo(kidx), flat16(q), flat16(k), flat16(v))


SAMPLE_PAGES_PER_STEP = 8
SAMPLE_T_PAD = 8


def _hi_lo(x):
    hi = x.astype(BF16)
    return hi, (x - hi.astype(F32)).astype(BF16)


def _dsa_sample_score_kernel(pt_ref, qi_ref, qil_ref, wi_ref, *refs, n_steps):
    j = pl.program_id(1)
    pages, new_ref, o_ref = refs[:-2], refs[-2], refs[-1]
    qh, ql, wcol = qi_ref[...], qil_ref[...], wi_ref[...]
    for i, pref in enumerate(pages):
        kidx = pref[...]
        if i == 0:
            kidx = jnp.where(j == n_steps - 1, new_ref[...], kidx)
        kh, kl = _hi_lo(kidx)
        s = (lax.dot_general(qh, kh, _NT, preferred_element_type=F32)
             + lax.dot_general(qh, kl, _NT, preferred_element_type=F32)
             + lax.dot_general(ql, kh, _NT, preferred_element_type=F32))
        s = jnp.maximum(s * IDX_DIM ** -0.5, 0.0) * wcol
        tot = s[0:SAMPLE_T_PAD]
        for h in range(1, IDX_HEADS):
            tot = tot + s[h * SAMPLE_T_PAD:(h + 1) * SAMPLE_T_PAD]
        o_ref[:, i * PAGE_SIZE:(i + 1) * PAGE_SIZE] = tot


def _dsa_sample_attn_kernel(pt_ref, score_ref, q_ref, *refs, n_steps, topk, past, t_len):
    pps = SAMPLE_PAGES_PER_STEP
    k_pages, v_pages = refs[:pps], refs[pps:2 * pps]
    kn_ref, vn_ref, o_ref, bias_scr, m_scr, l_scr, acc_scr = refs[2 * pps:]
    j = pl.program_id(1)
    width = score_ref.shape[1]
    group = B_HEADS // B_KV_HEADS

    @pl.when(j == 0)
    def _():
        qpos = past + lax.broadcasted_iota(I32, (SAMPLE_T_PAD, width), 0)
        kpos = lax.broadcasted_iota(I32, (SAMPLE_T_PAD, width), 1)
        sel = _select_topk_mask(score_ref[...], kpos <= qpos, topk)
        bias_scr[...] = jnp.where(sel, 0.0, -jnp.inf)
        m_scr[...] = jnp.full(m_scr.shape, -jnp.inf, F32)
        l_scr[...] = jnp.zeros(l_scr.shape, F32)
        acc_scr[...] = jnp.zeros(acc_scr.shape, F32)

    last = j == n_steps - 1
    col = pl.multiple_of(j * (pps * PAGE_SIZE), pps * PAGE_SIZE)
    b4 = bias_scr[0:t_len, pl.ds(col, pps * PAGE_SIZE)]
    bias = jnp.concatenate([b4] * group, axis=0)
    for g in range(B_KV_HEADS):
        def head_rows(pages, new_ref):
            first = jnp.where(last, new_ref[:, g, :], pages[0][:, g, :])
            return jnp.concatenate([first] + [r[:, g, :] for r in pages[1:]], axis=0).astype(BF16)

        kg = head_rows(k_pages, kn_ref)
        vg = head_rows(v_pages, vn_ref)
        s = lax.dot_general(q_ref[g], kg, _NT, preferred_element_type=F32) * HEAD_DIM ** -0.5 + bias
        m_old = m_scr[g]
        m_new = jnp.maximum(m_old, jnp.max(s, axis=1, keepdims=True))
        m_safe = jnp.where(m_new == -jnp.inf, 0.0, m_new)
        alpha = jnp.exp(m_old - m_safe)
        p = jnp.exp(s - m_safe)
        l_scr[g] = alpha * l_scr[g] + jnp.sum(p, axis=1, keepdims=True)
        acc_scr[g] = alpha * acc_scr[g] + jnp.dot(p.astype(BF16), vg, preferred_element_type=F32)
        m_scr[g] = m_new

    @pl.when(last)
    def _():
        for g in range(B_KV_HEADS):
            o = acc_scr[g] / l_scr[g]
            for hq in range(group):
                h = g * group + hq
                o_ref[:, h * HEAD_DIM:(h + 1) * HEAD_DIM] = o[hq * t_len:(hq + 1) * t_len]


def dsa_sample_attend(q, k, v, qi, wi, kidx, cache_k, cache_v, cache_kidx, page_table, layer):
    b, t = q.shape[:2]
    q = q.reshape(b, t, B_HEADS, HEAD_DIM)
    k = k.reshape(b, t, B_KV_HEADS, HEAD_DIM)
    v = v.reshape(b, t, B_KV_HEADS, HEAD_DIM)
    qi = qi.reshape(b, t, IDX_HEADS, IDX_DIM)
    n_pages = page_table.shape[1]
    past = n_pages * PAGE_SIZE
    topk = min(IDX_TOPK_MAX, (past + t) // 4)
    pps = SAMPLE_PAGES_PER_STEP
    assert n_pages % pps == 0 and t <= SAMPLE_T_PAD
    n_steps = n_pages // pps + 1
    width = n_steps * pps * PAGE_SIZE

    qi_r = jnp.pad(jnp.swapaxes(qi, 1, 2), ((0, 0), (0, 0), (0, SAMPLE_T_PAD - t), (0, 0)))
    qi_hi, qi_lo = _hi_lo(qi_r.reshape(b, IDX_HEADS * SAMPLE_T_PAD, IDX_DIM))
    wi_r = jnp.pad(jnp.swapaxes(wi, 1, 2), ((0, 0), (0, 0), (0, SAMPLE_T_PAD - t)))
    wi_r = wi_r.reshape(b, IDX_HEADS * SAMPLE_T_PAD, 1)
    kidx_new = jnp.pad(kidx, ((0, 0), (0, PAGE_SIZE - t), (0, 0)))

    def page_spec(i, *tail):
        def imap(bi, j, pt):
            return (layer, pt[bi, jnp.minimum(j * pps + i, n_pages - 1)]) + (0,) * (1 + len(tail))
        return pl.BlockSpec((None, None, PAGE_SIZE) + tail, imap)

    def per_b(shape):
        return pl.BlockSpec((None,) + shape, lambda bi, j, pt: (bi,) + (0,) * len(shape))

    score = pl.pallas_call(
        functools.partial(_dsa_sample_score_kernel, n_steps=n_steps),
        grid_spec=pltpu.PrefetchScalarGridSpec(
            num_scalar_prefetch=1, grid=(b, n_steps),
            in_specs=[per_b((IDX_HEADS * SAMPLE_T_PAD, IDX_DIM)), per_b((IDX_HEADS * SAMPLE_T_PAD, IDX_DIM)),
                      per_b((IDX_HEADS * SAMPLE_T_PAD, 1))]
                     + [page_spec(i, IDX_DIM) for i in range(pps)] + [per_b((PAGE_SIZE, IDX_DIM))],
            out_specs=pl.BlockSpec((None, SAMPLE_T_PAD, pps * PAGE_SIZE), lambda bi, j, pt: (bi, 0, j))),
        out_shape=jax.ShapeDtypeStruct((b, SAMPLE_T_PAD, width), F32),
        compiler_params=pltpu.CompilerParams(dimension_semantics=("arbitrary", "arbitrary")),
        name="dsa_sample_score",
    )(page_table, qi_hi, qi_lo, wi_r, *([cache_kidx] * pps), kidx_new)

    group = B_HEADS // B_KV_HEADS
    q_r = q.reshape(b, t, B_KV_HEADS, group, HEAD_DIM)
    q_r = jnp.transpose(q_r, (0, 2, 3, 1, 4)).reshape(b, B_KV_HEADS, group * t, HEAD_DIM).astype(BF16)
    k_new = jnp.pad(k, ((0, 0), (0, PAGE_SIZE - t), (0, 0), (0, 0)))
    v_new = jnp.pad(v, ((0, 0), (0, PAGE_SIZE - t), (0, 0), (0, 0)))
    return pl.pallas_call(
        functools.partial(_dsa_sample_attn_kernel, n_steps=n_steps, topk=topk, past=past, t_len=t),
        grid_spec=pltpu.PrefetchScalarGridSpec(
            num_scalar_prefetch=1, grid=(b, n_steps),
            in_specs=[per_b((SAMPLE_T_PAD, width)), per_b((B_KV_HEADS, group * t, HEAD_DIM))]
                     + [page_spec(i, B_KV_HEADS, HEAD_DIM) for i in range(pps)] * 2
                     + [per_b((PAGE_SIZE, B_KV_HEADS, HEAD_DIM))] * 2,
            out_specs=pl.BlockSpec((None, t, B_HEADS * HEAD_DIM), lambda bi, j, pt: (bi, 0, 0)),
            scratch_shapes=[pltpu.VMEM((SAMPLE_T_PAD, width), F32),
                            pltpu.VMEM((B_KV_HEADS, group * t, 1), F32),
                            pltpu.VMEM((B_KV_HEADS, group * t, 1), F32),
                            pltpu.VMEM((B_KV_HEADS, group * t, HEAD_DIM), F32)]),
        out_shape=jax.ShapeDtypeStruct((b, t, B_HEADS * HEAD_DIM), F32),
        compiler_params=pltpu.CompilerParams(dimension_semantics=("arbitrary", "arbitrary")),
        name="dsa_sample_attn",
    )(page_table, score, q_r, *([cache_k] * pps), *([cache_v] * pps), k_new, v_new)


DIL_WM = 128


def _dilated_prompt_kernel(q_ref, kp_ref, kc_ref, vp_ref, vc_ref, o_ref, lse_ref, *, dil, heads):
    n = pl.program_id(1)
    iq = lax.broadcasted_iota(I32, (DIL_WM, 2 * DIL_WM), 0)
    ik = lax.broadcasted_iota(I32, (DIL_WM, 2 * DIL_WM), 1)
    dist = iq + DIL_WM - ik
    ok = jnp.logical_and(dist >= 0, dist <= DIL_WM)
    ok = jnp.logical_and(ok, jnp.logical_or(ik >= DIL_WM, n > 0))
    bias = jnp.where(ok, 0.0, -jnp.inf)
    for r in range(dil):
        def rows(ref, hs):
            if dil == 1:
                return ref[:, hs]
            return ref[pl.ds(r, DIL_WM, stride=dil), :]

        for h in range(heads):
            hs = slice(h * HEAD_DIM, (h + 1) * HEAD_DIM)
            q = rows(q_ref, hs).astype(BF16)
            k = jnp.concatenate([rows(kp_ref, hs), rows(kc_ref, hs)], axis=0).astype(BF16)
            v = jnp.concatenate([rows(vp_ref, hs), rows(vc_ref, hs)], axis=0).astype(BF16)
            s = lax.dot_general(q, k, _NT, preferred_element_type=F32) * HEAD_DIM ** -0.5 + bias
            m = jnp.max(s, axis=1, keepdims=True)
            p = jnp.exp(s - m)
            denom = jnp.sum(p, axis=1, keepdims=True)
            o = jnp.dot(p.astype(BF16), v, preferred_element_type=F32) / denom
            lse = jnp.broadcast_to(m + jnp.log(denom), (DIL_WM, HEAD_DIM))
            if dil == 1:
                o_ref[:, hs] = o
                lse_ref[:, hs] = lse
            else:
                o_ref[pl.ds(r, DIL_WM, stride=dil), :] = o
                lse_ref[pl.ds(r, DIL_WM, stride=dil), :] = lse


def dilated_prompt_attend(q, k, v_src, v_col0, window, dil):
    b, s_len, width = q.shape
    assert window // dil == DIL_WM and s_len % (DIL_WM * dil) == 0
    rows = DIL_WM * dil
    heads = C_HEADS if dil == 1 else 1
    bw = heads * HEAD_DIM
    assert v_col0 % bw == 0
    vb = v_col0 // bw
    cur = pl.BlockSpec((None, rows, bw), lambda bi, n, hi: (bi, n, hi))
    prev = pl.BlockSpec((None, rows, bw), lambda bi, n, hi: (bi, jnp.maximum(n - 1, 0), hi))
    vcur = pl.BlockSpec((None, rows, bw), lambda bi, n, hi: (bi, n, vb + hi))
    vprev = pl.BlockSpec((None, rows, bw), lambda bi, n, hi: (bi, jnp.maximum(n - 1, 0), vb + hi))
    return pl.pallas_call(
        functools.partial(_dilated_prompt_kernel, dil=dil, heads=heads),
        grid=(b, s_len // rows, width // bw),
        in_specs=[cur, prev, cur, vprev, vcur],
        out_specs=[cur, cur],
        out_shape=[jax.ShapeDtypeStruct(q.shape, F32)] * 2,
        compiler_params=pltpu.CompilerParams(dimension_semantics=("arbitrary",) * 3),
        name="dilated_prompt",
    )(q, k, k, v_src, v_src)


NORM_ROPE_ROWS = 256


def rope_tables(pos):
    half = ROT_DIM // 2
    freqs = ROPE_THETA ** (-jnp.arange(half, dtype=F32) / half)
    ang = pos.astype(F32)[:, None] * freqs[None, :]
    cos, sin = jnp.cos(ang), jnp.sin(ang)
    t = pos.shape[0]
    ones = jnp.ones((t, HEAD_DIM - ROT_DIM), F32)
    zeros = jnp.zeros((t, HEAD_DIM - ROT_DIM), F32)
    c = jnp.concatenate([cos, cos, ones], axis=1)
    s_dn = jnp.concatenate([-sin, jnp.zeros_like(sin), zeros], axis=1)
    s_up = jnp.concatenate([jnp.zeros_like(sin), sin, zeros], axis=1)
    return c, s_dn, s_up


def _norm_rope_kernel(x_ref, g_ref, b_ref, c_ref, sd_ref, su_ref, o_ref, *, heads, norm):
    half = ROT_DIM // 2
    c, sd, su = c_ref[...], sd_ref[...], su_ref[...]
    for h in range(heads):
        hs = slice(h * HEAD_DIM, (h + 1) * HEAD_DIM)
        x = x_ref[:, hs]
        if norm == "rms":
            x = x * lax.rsqrt(jnp.mean(x * x, axis=1, keepdims=True) + NORM_EPS) * g_ref[...]
        elif norm == "layer":
            xc = x - jnp.mean(x, axis=1, keepdims=True)
            x = xc * lax.rsqrt(jnp.mean(xc * xc, axis=1, keepdims=True) + NORM_EPS) * g_ref[...] + b_ref[...]
        dn = pltpu.roll(x, HEAD_DIM - half, axis=1)
        up = pltpu.roll(x, half, axis=1)
        o_ref[:, hs] = x * c + dn * sd + up * su


def norm_rope(x, col0, heads, gain, bias, tables, t_len, norm):
    r, _ = x.shape
    bw = heads * HEAD_DIM
    assert col0 % bw == 0
    if t_len % 8 == 0:
        rows = min(NORM_ROPE_ROWS, t_len)
    else:
        rows, tables = r, tuple(jnp.tile(t, (r // t_len, 1)) for t in tables)
        t_len = r
    assert t_len % rows == 0 and r % rows == 0
    tb = t_len // rows
    tab = pl.BlockSpec((rows, HEAD_DIM), lambda i: (i % tb, 0))
    vec = pl.BlockSpec((1, HEAD_DIM), lambda i: (0, 0))
    g = (jnp.ones((HEAD_DIM,), F32) if gain is None else gain).reshape(1, HEAD_DIM).astype(F32)
    b = (jnp.zeros((HEAD_DIM,), F32) if bias is None else bias).reshape(1, HEAD_DIM).astype(F32)
    return pl.pallas_call(
        functools.partial(_norm_rope_kernel, heads=heads, norm=norm),
        grid=(r // rows,),
        in_specs=[pl.BlockSpec((rows, bw), lambda i: (i, col0 // bw)), vec, vec, tab, tab, tab],
        out_specs=pl.BlockSpec((rows, bw), lambda i: (i, 0)),
        out_shape=jax.ShapeDtypeStruct((r, bw), F32),
        compiler_params=pltpu.CompilerParams(dimension_semantics=("arbitrary",)),
        name="norm_rope",
    )(x, g, b, *tables)


def _rms(x, eps=NORM_EPS):
    xf = x.astype(F32)
    return xf * lax.rsqrt(jnp.mean(xf * xf, axis=-1, keepdims=True) + eps)


def rms_norm(x, g):
    return (_rms(x) * g.astype(F32)).astype(x.dtype)


def rwkv7_mix(sh, prev_row, s0, mu, w0, w_lora, a0, a_lora, k_k, k_a, r_k, gn_g, gn_b):
    bn, t, _ = sh.shape
    prev = jnp.concatenate([prev_row[:, None, :].astype(sh.dtype), sh[:, :-1]], axis=1)
    xm = sh + (prev - sh) * mu
    r, k, v = (xm[..., j * A_WIDTH:(j + 1) * A_WIDTH] for j in range(3))
    xw = xm[..., 3 * A_WIDTH:3 * A_WIDTH + DECAY_LORA]
    xa = xm[..., 3 * A_WIDTH + DECAY_LORA:]
    wlog = -jax.nn.softplus(-(w0 + mm(jnp.tanh(xw), w_lora))) - 0.5
    decay = jnp.exp(-jnp.exp(wlog.astype(F32)))
    a = jax.nn.sigmoid((a0 + mm(xa, a_lora)).astype(F32))

    def heads(z):
        return z.astype(F32).reshape(bn, t, A_HEADS, A_HEAD)

    def head_sum(z):
        return jnp.broadcast_to(jnp.sum(heads(z), axis=-1, keepdims=True),
                                (bn, t, A_HEADS, A_HEAD)).reshape(bn, t, A_WIDTH)

    kk = heads(k * k_k)
    kk = kk / jnp.maximum(jnp.sqrt(jnp.sum(kk * kk, axis=-1, keepdims=True)), 1e-12)
    kk = kk.reshape(bn, t, A_WIDTH)
    k2 = k * (1.0 + (a - 1.0) * k_a.astype(F32))
    kka = kk * a
    q = decay * r - kk * head_sum(kka * r)
    vc = v * head_sum(k2 * r)

    y, s_fin = rwkv_scan((-kk, decay, kka, k2, v, q, vc), s0.astype(F32))
    y = heads(y)
    ym = jnp.mean(y, axis=-1, keepdims=True)
    yc = y - ym
    yn = yc * lax.rsqrt(jnp.mean(yc * yc, axis=-1, keepdims=True) + GN_EPS)
    yn = yn.reshape(bn, t, A_WIDTH) * gn_g.astype(F32) + gn_b.astype(F32)
    bonus = (jnp.sum(heads(r * k2) * r_k.astype(F32), axis=-1, keepdims=True) * heads(v)).reshape(bn, t, A_WIDTH)
    return (yn + bonus).astype(sh.dtype), s_fin.astype(s0.dtype), sh[:, -1]


KV_WIDTH = B_KV_HEADS * HEAD_DIM
E_COLS = {}
_acc = 0
for _name, _w, _pad in (('q', B_WIDTH, 0), ('g_a', A_WIDTH, 0), ('g_b', B_WIDTH, 0), ('rkv', 3 * A_WIDTH, 0),
                        ('k', KV_WIDTH, 0), ('v', KV_WIDTH, 0), ('cqi', IDX_Q_RANK, 0), ('kidx', IDX_DIM, 0),
                        ('wi', IDX_HEADS, V7X_LANES - IDX_HEADS),
                        ('lora', DECAY_LORA + AAA_LORA, 2 * V7X_LANES - DECAY_LORA - AAA_LORA)):
    E_COLS[_name] = (_acc, _w)
    _acc += _w + _pad
E_WIDTH = _acc


def repack_even_w_in(w_all, layer):
    d_in = w_all.shape[1]
    src = {'rkv': 0, 'lora': 3 * A_WIDTH, 'g_a': SHIFT_W, 'q': IN_A, 'k': IN_A + B_WIDTH,
           'v': IN_A + B_WIDTH + KV_WIDTH, 'cqi': IN_A + B_WIDTH + 2 * KV_WIDTH}
    src['kidx'] = src['cqi'] + IDX_Q_RANK
    src['wi'] = src['kidx'] + IDX_DIM
    src['g_b'] = src['wi'] + IDX_HEADS
    parts, pos = [], 0
    for name, (start, width) in E_COLS.items():
        if start > pos:
            parts.append(jnp.zeros((d_in, start - pos), w_all.dtype))
        parts.append(w_all[layer, :, src[name]:src[name] + width])
        pos = start + width
    if E_WIDTH > pos:
        parts.append(jnp.zeros((d_in, E_WIDTH - pos), w_all.dtype))
    return jnp.concatenate(parts, axis=1)


def _cols(u, name):
    start, width = E_COLS[name]
    return u[:, start:start + width]


def even_mixer(u, bn, t, pos, prev_row, s0, attend, ep):
    sh = jnp.concatenate([_cols(u, 'rkv'), _cols(u, 'lora')], axis=1).reshape(bn, t, SHIFT_W)
    y_a, s_fin, last_row = rwkv7_mix(sh, prev_row, s0, ep['shift_mu'], ep['w0'], ep['w_lora'],
                                     ep['a0'], ep['a_lora'], ep['k_k'], ep['k_a'], ep['r_k'],
                                     ep['gn_gain'], ep['gn_bias'])
    tabs = rope_tables(pos)
    q = norm_rope(u, E_COLS['q'][0], B_HEADS, ep['q_norm'], None, tabs, t, "rms")
    k = norm_rope(u, E_COLS['k'][0], B_KV_HEADS, ep['k_norm'], None, tabs, t, "rms")
    v = _cols(u, 'v')
    qi = norm_rope(matmul(rms_norm(_cols(u, 'cqi'), ep['qi_norm']).astype(BF16), ep['w_qi']),
                   0, IDX_HEADS, None, None, tabs, t, "none")
    kidx = norm_rope(u, E_COLS['kidx'][0], 1, ep['kidx_gain'], ep['kidx_bias'], tabs, t, "layer")
    wi = _cols(u, 'wi') * IDX_HEADS ** -0.5

    def seq(z):
        return z.reshape(bn, t, -1)

    y_b = attend(seq(q), seq(k), seq(v), seq(qi), seq(wi), seq(kidx))
    y = jnp.concatenate([y_a * jax.nn.silu(seq(_cols(u, 'g_a'))), y_b * jax.nn.silu(seq(_cols(u, 'g_b')))],
                        axis=-1)
    heads4 = (bn, t, B_KV_HEADS, HEAD_DIM)
    y = y.reshape(bn * t, -1).astype(BF16)
    return y, (s_fin, last_row, k.reshape(heads4), v.reshape(heads4), seq(kidx))


def dilated_sample(q, k, v, buf_k, buf_v, window, dil):
    t = q.shape[1]
    d = q.shape[-1]
    wb = buf_k.shape[1]
    wm = window // dil
    kc = jnp.concatenate([buf_k.astype(k.dtype), k], axis=1)
    vc = jnp.concatenate([buf_v.astype(v.dtype), v], axis=1)
    idx = wb + jnp.arange(t)[:, None] - jnp.arange(wm + 1)[None, :] * dil
    valid = idx >= 0
    idxc = jnp.maximum(idx, 0)
    kg, vg = kc[:, idxc], vc[:, idxc]
    s = jnp.einsum('bthd,btjhd->bthj', q, kg).astype(F32) * d ** -0.5
    s = jnp.where(valid[None, :, None, :], s, -jnp.inf)
    lse = jax.nn.logsumexp(s, axis=-1)
    p = jnp.exp(s - lse[..., None])
    o = jnp.einsum('bthj,btjhd->bthd', p.astype(vg.dtype), vg)
    return o, lse, kc[:, -wb:], vc[:, -wb:]


def odd_mixer(u, bn, t, pos, sample_bufs, op):
    n_g = len(C_GROUPS)
    tabs = rope_tables(pos)
    heads4 = (bn, t, C_HEADS, HEAD_DIM)
    outs, lses, bufs = [], [], []
    for g, (win, dil) in enumerate(C_GROUPS):
        q = norm_rope(u, (3 * g) * C_WIDTH, C_HEADS, op['q_norm'][g], None, tabs, t, "rms")
        k = norm_rope(u, (3 * g + 1) * C_WIDTH, C_HEADS, op['k_norm'][g], None, tabs, t, "rms")
        v_col0 = (3 * g + 2) * C_WIDTH
        v = u[:, v_col0:v_col0 + C_WIDTH].reshape(heads4)
        if sample_bufs is None:
            o, lse = dilated_prompt_attend(q.reshape(bn, t, C_WIDTH), k.reshape(bn, t, C_WIDTH),
                                           u.reshape(bn, t, -1), v_col0, win, dil)
            keep = min(win, t)
            kb, vb = k.reshape(heads4)[:, -keep:], v[:, -keep:]
        else:
            o, lse, kb, vb = dilated_sample(q.reshape(heads4), k.reshape(heads4), v,
                                            sample_bufs[g][0], sample_bufs[g][1], win, dil)
            o = o.reshape(bn, t, C_WIDTH)
            lse = jnp.broadcast_to(lse[..., None], heads4).reshape(bn, t, C_WIDTH)
        outs.append(o)
        lses.append(lse)
        bufs += [kb, vb]
    alpha = jax.nn.softmax(jnp.stack(lses, axis=0), axis=0)
    o = jnp.sum(alpha * jnp.stack(outs, axis=0), axis=0)
    gate = u[:, 3 * n_g * C_WIDTH:].reshape(bn, t, C_WIDTH)
    return (o * jax.nn.silu(gate)).reshape(bn * t, C_WIDTH).astype(BF16), tuple(bufs)


def _ple_rows(h, m, p_l):
    return (_rms(h + m).astype(BF16), p_l.reshape(-1, p_l.shape[-1]).astype(BF16), h, m)


def kernel(x_prompt, x_sample, p_prompt, p_sample, state_wkv, state_shift, cache_k, cache_v,
           cache_kidx, page_table, cache_win_k0, cache_win_v0, cache_win_k1, cache_win_v1,
           cache_win_k2, cache_win_v2, ln_gain, e_w_in, e_shift_mu, e_w0, e_w_lora, e_a0,
           e_a_lora, e_k_k, e_k_a, e_r_k, e_gn_gain, e_gn_bias, e_q_norm, e_k_norm, e_qi_norm,
           e_w_qi, e_kidx_gain, e_kidx_bias, e_w_out, o_w_in, o_q_norm, o_k_norm, o_w_out,
           ple_w_proj, ple_w_gate):
    depth = ln_gain.shape[0]
    bp, s_len, _ = x_prompt.shape
    t_len = x_sample.shape[1]
    past = page_table.shape[1] * PAGE_SIZE
    pos_p = jnp.arange(s_len, dtype=jnp.int32)
    pos_s = past + jnp.arange(t_len, dtype=jnp.int32)
    bufs_k = (cache_win_k0, cache_win_k1, cache_win_k2)
    bufs_v = (cache_win_v0, cache_win_v1, cache_win_v2)
    bs = x_sample.shape[0]
    hp, hs = x_prompt.reshape(bp * s_len, -1), x_sample.reshape(bs * t_len, -1)
    ev_p, ev_s, od_p, od_s = [], [], [], []
    for i in range(depth):
        l = i // 2
        xp = rms_norm(hp, ln_gain[i]).astype(BF16)
        xs = rms_norm(hs, ln_gain[i]).astype(BF16)
        if i % 2 == 0:
            ep = {'shift_mu': e_shift_mu[l], 'w0': e_w0[l], 'w_lora': e_w_lora[l],
                  'a0': e_a0[l], 'a_lora': e_a_lora[l], 'k_k': e_k_k[l], 'k_a': e_k_a[l],
                  'r_k': e_r_k[l], 'gn_gain': e_gn_gain[l], 'gn_bias': e_gn_bias[l],
                  'q_norm': e_q_norm[l], 'k_norm': e_k_norm[l], 'qi_norm': e_qi_norm[l],
                  'w_qi': e_w_qi[l], 'kidx_gain': e_kidx_gain[l], 'kidx_bias': e_kidx_bias[l]}
            up, us = matmul2(xp, xs, repack_even_w_in(e_w_in, l))
            row0 = jnp.zeros((bp, SHIFT_W), F32)
            st0 = jnp.zeros((bp, A_HEADS, A_HEAD, A_HEAD), F32)
            yp, stp = even_mixer(up, bp, s_len, pos_p, row0, st0, dsa_prompt_attend, ep)
            att_s = functools.partial(dsa_sample_attend, cache_k=cache_k, cache_v=cache_v,
                                      cache_kidx=cache_kidx, page_table=page_table, layer=l)
            ys, sts = even_mixer(us, bs, t_len, pos_s, state_shift[l], state_wkv[l], att_s, ep)
            mp, ms = matmul2(yp, ys, e_w_out[l])
            ev_p.append(stp)
            ev_s.append(sts)
        else:
            op = {'q_norm': o_q_norm[l], 'k_norm': o_k_norm[l]}
            up, us = matmul2(xp, xs, o_w_in[l])
            yp, stp = odd_mixer(up, bp, s_len, pos_p, None, op)
            sample_bufs = [(bk[l], bv[l]) for bk, bv in zip(bufs_k, bufs_v)]
            ys, sts = odd_mixer(us, bs, t_len, pos_s, sample_bufs, op)
            mp, ms = matmul2(yp, ys, o_w_out[l])
            od_p.append(stp)
            od_s.append(sts)
        hp, hs = ple_update(ple_w_gate[i], ple_w_proj[i], _ple_rows(hp, mp, p_prompt[i]),
                            _ple_rows(hs, ms, p_sample[i]))
    hp, hs = hp.reshape(x_prompt.shape), hs.reshape(x_sample.shape)

    def st(lst, j):
        if len(lst) == 1:
            return lst[0][j][None]
        return jnp.stack([e[j] for e in lst], axis=0)

    outs = [hp, hs, st(ev_p, 0), st(ev_s, 0), st(ev_p, 1), st(ev_s, 1)]
    outs += [st(ev_p, j) for j in (2, 3, 4)] + [st(ev_s, j) for j in (2, 3, 4)]
    outs += [st(od_p, j) for j in range(6)] + [st(od_s, j) for j in range(6)]
    return tuple(outs)
```

```python
import functools

import jax
import jax.numpy as jnp
from jax import lax
from jax.experimental import pallas as pl
from jax.experimental.pallas import tpu as pltpu

F32 = jnp.float32
BF16 = jnp.bfloat16
I32 = jnp.int32

D_MODEL = 4096
PAGE_SIZE = 128
HEAD_DIM = 128
ROT_DIM = HEAD_DIM // 4
ROPE_THETA = 500000.0
NORM_EPS = 1e-6

A_WIDTH = D_MODEL // 2
A_HEAD = 64
A_HEADS = A_WIDTH // A_HEAD
DECAY_LORA = 96
AAA_LORA = 96
GN_EPS = 64e-5
SHIFT_W = 3 * A_WIDTH + DECAY_LORA + AAA_LORA

B_WIDTH = D_MODEL // 2
B_HEADS = B_WIDTH // HEAD_DIM
B_KV_HEADS = 4
IDX_HEADS = 16
IDX_DIM = 128
IDX_Q_RANK = 512
IDX_TOPK_MAX = 256
QBLOCK = 128

C_GROUPS = ((128, 1), (512, 4), (2048, 16))
C_HEADS = 16
C_WIDTH = C_HEADS * HEAD_DIM

IN_A = SHIFT_W + A_WIDTH
IN_B = B_WIDTH + 2 * B_KV_HEADS * HEAD_DIM + IDX_Q_RANK + IDX_DIM + IDX_HEADS + B_WIDTH

V7X_LANES = 128
V7X_VMEM_LIMIT_BYTES = 58 * 1024 * 1024
INT_MIN = -2 ** 31

_NT = (((1,), (1,)), ((), ()))


def _matmul_kernel(a_ref, b_ref, o_ref, bq_ref):
    @pl.when(pl.program_id(1) == 0)
    def _():
        bq_ref[...] = b_ref[...].astype(BF16)

    o_ref[...] = jnp.dot(a_ref[...].astype(BF16), bq_ref[...], preferred_element_type=F32)


def _pick_tile(n, cands):
    for c in cands:
        if n % c == 0:
            return c
    return n


def matmul(a, b):
    m, k = a.shape
    _, n = b.shape
    n_pad = -(-n // V7X_LANES) * V7X_LANES
    if n_pad != n:
        b = jnp.pad(b, ((0, 0), (0, n_pad - n)))
    tn = _pick_tile(n_pad, (1024, 512, 256, 128))
    tm = _pick_tile(m, (512, 256, 128, 64, 32, 16, 8))
    out = pl.pallas_call(
        _matmul_kernel,
        grid=(n_pad // tn, m // tm),
        in_specs=[pl.BlockSpec((tm, k), lambda j, i: (i, 0)),
                  pl.BlockSpec((k, tn), lambda j, i: (0, j))],
        out_specs=pl.BlockSpec((tm, tn), lambda j, i: (i, j)),
        out_shape=jax.ShapeDtypeStruct((m, n_pad), F32),
        scratch_shapes=[pltpu.VMEM((k, tn), BF16)],
        compiler_params=pltpu.CompilerParams(
            dimension_semantics=("arbitrary", "arbitrary"),
            vmem_limit_bytes=V7X_VMEM_LIMIT_BYTES),
        name="matmul",
    )(a, b)
    return out[:, :n] if n_pad != n else out


def mm(x, w):
    lead = x.shape[:-1]
    return matmul(x.reshape(-1, x.shape[-1]).astype(BF16), w).reshape(lead + (w.shape[-1],))


def _matmul2_kernel(a_ref, as_ref, b_ref, o_ref, os_ref, bq_ref):
    @pl.when(pl.program_id(1) == 0)
    def _():
        bq_ref[...] = b_ref[...].astype(BF16)
        os_ref[...] = jnp.dot(as_ref[...], bq_ref[...], preferred_element_type=F32)

    o_ref[...] = jnp.dot(a_ref[...], bq_ref[...], preferred_element_type=F32)


def matmul2(a, a_s, b):
    m, k = a.shape
    ms = a_s.shape[0]
    n = b.shape[1]
    tn = _pick_tile(n, (1024, 512, 256, 128))
    tm = _pick_tile(m, (512, 256, 128, 64, 32, 16, 8))
    assert n % tn == 0 and m % tm == 0
    return pl.pallas_call(
        _matmul2_kernel,
        grid=(n // tn, m // tm),
        in_specs=[pl.BlockSpec((tm, k), lambda j, i: (i, 0)),
                  pl.BlockSpec((ms, k), lambda j, i: (0, 0)),
                  pl.BlockSpec((k, tn), lambda j, i: (0, j))],
        out_specs=[pl.BlockSpec((tm, tn), lambda j, i: (i, j)),
                   pl.BlockSpec((ms, tn), lambda j, i: (0, j))],
        out_shape=[jax.ShapeDtypeStruct((m, n), F32), jax.ShapeDtypeStruct((ms, n), F32)],
        scratch_shapes=[pltpu.VMEM((k, tn), BF16)],
        compiler_params=pltpu.CompilerParams(
            dimension_semantics=("arbitrary", "arbitrary"),
            vmem_limit_bytes=V7X_VMEM_LIMIT_BYTES),
        name="matmul2",
    )(a, a_s, b)


PLE_TM = 512
PLE_TN = 512


def _ple_kernel(wg_ref, wp_ref, a_ref, p_ref, h_ref, m_ref, as_ref, ps_ref, hs_ref, ms_ref,
                o_ref, os_ref, wgq_ref, wpq_ref):
    def update(a, p, h, m, o):
        gate = jnp.dot(a[...], wgq_ref[...], preferred_element_type=F32)
        proj = jnp.dot(p[...], wpq_ref[...], preferred_element_type=F32)
        o[...] = (h[...] + m[...]) + proj / (1.0 + jnp.exp(-gate))

    @pl.when(pl.program_id(1) == 0)
    def _():
        wgq_ref[...] = wg_ref[...].astype(BF16)
        wpq_ref[...] = wp_ref[...].astype(BF16)
        update(as_ref, ps_ref, hs_ref, ms_ref, os_ref)

    update(a_ref, p_ref, h_ref, m_ref, o_ref)


def ple_update(w_gate, w_proj, big, small):
    a, p, h, m = big
    a_s, p_s, h_s, m_s = small
    r, k = a.shape
    rs = a_s.shape[0]
    n = w_gate.shape[1]
    pk = p.shape[1]
    tm, tn = PLE_TM, PLE_TN
    assert n % tn == 0 and r % tm == 0

    def rows(width, full):
        if full:
            return pl.BlockSpec((tm, width), lambda j, i: (i, 0))
        return pl.BlockSpec((tm, tn), lambda j, i: (i, j))

    def rows_s(width, full):
        if full:
            return pl.BlockSpec((rs, width), lambda j, i: (0, 0))
        return pl.BlockSpec((rs, tn), lambda j, i: (0, j))

    return pl.pallas_call(
        _ple_kernel,
        grid=(n // tn, r // tm),
        in_specs=[pl.BlockSpec((k, tn), lambda j, i: (0, j)), pl.BlockSpec((pk, tn), lambda j, i: (0, j)),
                  rows(k, True), rows(pk, True), rows(tn, False), rows(tn, False),
                  rows_s(k, True), rows_s(pk, True), rows_s(tn, False), rows_s(tn, False)],
        out_specs=[rows(tn, False), rows_s(tn, False)],
        out_shape=[jax.ShapeDtypeStruct((r, n), F32), jax.ShapeDtypeStruct((rs, n), F32)],
        scratch_shapes=[pltpu.VMEM((k, tn), BF16), pltpu.VMEM((pk, tn), BF16)],
        compiler_params=pltpu.CompilerParams(dimension_semantics=("arbitrary", "arbitrary"),
                                             vmem_limit_bytes=V7X_VMEM_LIMIT_BYTES),
        name="ple_update",
    )(w_gate, w_proj, a, p, h, m, a_s, p_s, h_s, m_s)


RWKV_PAIRS_PER_STEP = 8
RWKV_UNROLL = 16
RWKV_ROW_INPUTS = 7


def _split_bf16(x):
    hi = x.astype(BF16)
    lo = (x - hi.astype(F32)).astype(BF16)
    return jnp.concatenate([hi, lo], axis=1)


def _rwkv_scan_kernel(*refs, pairs, steps, use_mxu):
    rows = [refs[j * pairs:(j + 1) * pairs] for j in range(RWKV_ROW_INPUTS)]
    nkk_r, w_r, kka_r, k_r, v_r, q_r, vc_r = rows
    s0_ref, y_ref, sout_ref, s_scr, yt_scr = refs[RWKV_ROW_INPUTS * pairs:]
    tchunk = pl.program_id(2)

    @pl.when(tchunk == 0)
    def _():
        for p in range(pairs):
            s_scr[p] = jnp.concatenate([s0_ref[0, 2 * p], s0_ref[0, 2 * p + 1]], axis=1)

    lane = lax.broadcasted_iota(I32, (A_HEAD, V7X_LANES), 1)
    row = lax.broadcasted_iota(I32, (A_HEAD, V7X_LANES), 0)
    lo = lane < A_HEAD
    eye_lo = lane == row
    eye_hi = lane == row + A_HEAD
    eye = jnp.logical_or(eye_lo, eye_hi)
    lane_t = lax.broadcasted_iota(I32, (A_HEAD, steps), 1)
    yt_scr[...] = jnp.zeros(yt_scr.shape, F32)
    if use_mxu:
        kk_i = lax.broadcasted_iota(I32, (2 * V7X_LANES, 2 * V7X_LANES), 0)
        nn_i = lax.broadcasted_iota(I32, (2 * V7X_LANES, 2 * V7X_LANES), 1)
        seg_mat = jnp.where(((kk_i % V7X_LANES) >= A_HEAD) == (nn_i >= V7X_LANES), 1.0, 0.0).astype(BF16)
        eye_bf = jnp.where(eye, 1.0, 0.0).astype(BF16)
        zero_bf = jnp.zeros((A_HEAD, V7X_LANES), BF16)

    def seg_sum(x):
        s_lo = jnp.sum(jnp.where(lo, x, 0.0), axis=1, keepdims=True)
        s_hi = jnp.sum(jnp.where(lo, 0.0, x), axis=1, keepdims=True)
        return s_lo, s_hi

    def step(t, carry):
        for p in range(pairs):
            def rowvec(group):
                return jnp.broadcast_to(group[p][pl.ds(t, 1), :], (A_HEAD, V7X_LANES))

            s = s_scr[p]
            sa_lo, sa_hi = seg_sum(s * rowvec(nkk_r))
            py = s * rowvec(q_r) + jnp.where(eye, rowvec(vc_r), 0.0)
            r0 = p * V7X_LANES
            if use_mxu:
                vrow = v_r[p][pl.ds(t, 1), :]
                vh = vrow.astype(BF16)
                vl = (vrow - vh.astype(F32)).astype(BF16)
                vd = jnp.concatenate([eye_bf * vh, eye_bf * vl], axis=1)
                py2 = jnp.concatenate([py.astype(BF16), zero_bf], axis=1)
                lhs = jnp.concatenate([py2, vd], axis=0)
                yv = jnp.dot(lhs, seg_mat, preferred_element_type=F32)
                y_lo, y_hi = yv[:A_HEAD, :V7X_LANES], yv[:A_HEAD, V7X_LANES:]
                v_b = jnp.where(lo, yv[A_HEAD:, :V7X_LANES], yv[A_HEAD:, V7X_LANES:])
            else:
                y_lo, y_hi = seg_sum(py)
                vrow = rowvec(v_r)
                v_lo = jnp.sum(jnp.where(eye_lo, vrow, 0.0), axis=1, keepdims=True)
                v_hi = jnp.sum(jnp.where(eye_hi, vrow, 0.0), axis=1, keepdims=True)
                v_b = jnp.where(lo, v_lo, v_hi)
            sa_b = jnp.where(lo, sa_lo, sa_hi)
            s_scr[p] = s * rowvec(w_r) + sa_b * rowvec(kka_r) + v_b * rowvec(k_r)
            yt_scr[r0:r0 + A_HEAD, :] = jnp.where(lane_t == t, y_lo, yt_scr[r0:r0 + A_HEAD, :])
            yt_scr[r0 + A_HEAD:r0 + V7X_LANES, :] = jnp.where(
                lane_t == t, y_hi, yt_scr[r0 + A_HEAD:r0 + V7X_LANES, :])
        return carry

    lax.fori_loop(0, steps, step, 0, unroll=RWKV_UNROLL if steps % RWKV_UNROLL == 0 else 1)
    for p in range(pairs):
        y_ref[:, p * V7X_LANES:(p + 1) * V7X_LANES] = yt_scr[p * V7X_LANES:(p + 1) * V7X_LANES, :].T

    @pl.when(tchunk == pl.num_programs(2) - 1)
    def _():
        for p in range(pairs):
            s = s_scr[p]
            sout_ref[0, 2 * p] = s[:, :A_HEAD]
            sout_ref[0, 2 * p + 1] = s[:, A_HEAD:]


def rwkv_scan(row_inputs, s0):
    b, t, c = row_inputs[0].shape
    pairs = RWKV_PAIRS_PER_STEP
    npairs = c // V7X_LANES
    tc = min(t, V7X_LANES)
    assert t % tc == 0 and npairs % pairs == 0 and len(row_inputs) == RWKV_ROW_INPUTS
    bw = pairs * V7X_LANES

    def pair_spec(p):
        return pl.BlockSpec((None, tc, V7X_LANES), lambda bi, hi, ti: (bi, ti, hi * pairs + p))

    st_spec = pl.BlockSpec((1, 2 * pairs, A_HEAD, A_HEAD), lambda bi, hi, ti: (bi, hi, 0, 0))
    operands = [x for x in row_inputs for _ in range(pairs)]
    return pl.pallas_call(
        functools.partial(_rwkv_scan_kernel, pairs=pairs, steps=tc, use_mxu=(tc == V7X_LANES)),
        grid=(b, npairs // pairs, t // tc),
        in_specs=[pair_spec(p) for _ in range(RWKV_ROW_INPUTS) for p in range(pairs)] + [st_spec],
        out_specs=[pl.BlockSpec((None, tc, bw), lambda bi, hi, ti: (bi, ti, hi)), st_spec],
        out_shape=[jax.ShapeDtypeStruct((b, t, c), F32), jax.ShapeDtypeStruct(s0.shape, F32)],
        scratch_shapes=[pltpu.VMEM((pairs, A_HEAD, V7X_LANES), F32),
                        pltpu.VMEM((bw, tc), F32)],
        compiler_params=pltpu.CompilerParams(dimension_semantics=("arbitrary", "arbitrary", "arbitrary")),
        name="rwkv_scan",
    )(*operands, s0)


def _select_topk_mask(score, allowed, topk):
    r, l = score.shape
    score = jnp.where(score == 0.0, 0.0, score)
    bits = pltpu.bitcast(score, I32)
    key = jnp.where(bits < 0, bits ^ jnp.int32(0x7FFFFFFF), bits)
    key = jnp.where(allowed, key, jnp.int32(INT_MIN))
    kf = jnp.float32(topk)

    def count(pred):
        return jnp.sum(jnp.where(pred, 1.0, 0.0), axis=1, keepdims=True)

    def bit_step(i, prefix):
        cand = prefix | lax.shift_left(jnp.int32(1), jnp.int32(31) - i)
        ok = count(key >= (cand ^ jnp.int32(INT_MIN))) >= kf
        return jnp.where(ok, cand, prefix)

    prefix = lax.fori_loop(0, 32, bit_step, jnp.zeros((r, 1), I32))
    thr = prefix ^ jnp.int32(INT_MIN)
    gt = key > thr
    eq = jnp.logical_and(key == thr, allowed)
    need = kf - count(gt)
    li = lax.broadcasted_iota(I32, (V7X_LANES, 2 * V7X_LANES), 0)
    lj = lax.broadcasted_iota(I32, (V7X_LANES, 2 * V7X_LANES), 1)
    tri_ones = jnp.where(jnp.logical_or(lj >= V7X_LANES, li < lj), 1.0, 0.0).astype(BF16)
    running = jnp.zeros((r, V7X_LANES), F32)
    sel = []
    for c in range(l // V7X_LANES):
        sl = slice(c * V7X_LANES, (c + 1) * V7X_LANES)
        eq_c = eq[:, sl]
        res = jnp.dot(jnp.where(eq_c, 1.0, 0.0).astype(BF16), tri_ones, preferred_element_type=F32)
        before = res[:, :V7X_LANES] + running
        running = running + res[:, V7X_LANES:]
        sel.append(jnp.logical_or(gt[:, sl], jnp.logical_and(eq_c, before < need)))
    return jnp.concatenate(sel, axis=1)


def _dsa_prompt_kernel(qi_ref, qil_ref, wi_ref, kidx_ref, kidxl_ref, q_ref, k_ref, v_ref, o_ref, *, topk):
    qb = pl.program_id(1)
    s_len = kidx_ref.shape[0]
    group = B_HEADS // B_KV_HEADS

    def attend(l):
        kidx = kidx_ref[0:l, :]
        kidx_lo = kidxl_ref[0:l, :]
        score = jnp.zeros((QBLOCK, l), F32)
        for h in range(IDX_HEADS):
            hs = slice(h * IDX_DIM, (h + 1) * IDX_DIM)
            s = (lax.dot_general(qi_ref[:, hs], kidx, _NT, preferred_element_type=F32)
                 + lax.dot_general(qi_ref[:, hs], kidx_lo, _NT, preferred_element_type=F32)
                 + lax.dot_general(qil_ref[:, hs], kidx, _NT, preferred_element_type=F32))
            s = jnp.maximum(s * IDX_DIM ** -0.5, 0.0)
            score = score + s * wi_ref[:, h:h + 1]
        qpos = qb * QBLOCK + lax.broadcasted_iota(I32, (QBLOCK, l), 0)
        kpos = lax.broadcasted_iota(I32, (QBLOCK, l), 1)
        sel = _select_topk_mask(score, kpos <= qpos, topk)
        bias = jnp.where(sel, 0.0, -jnp.inf)
        for g in range(B_KV_HEADS):
            kg = k_ref[0:l, g * HEAD_DIM:(g + 1) * HEAD_DIM]
            vg = v_ref[0:l, g * HEAD_DIM:(g + 1) * HEAD_DIM]
            for j in range(group):
                h = g * group + j
                s = lax.dot_general(q_ref[:, h * HEAD_DIM:(h + 1) * HEAD_DIM], kg, _NT,
                                    preferred_element_type=F32)
                s = s * HEAD_DIM ** -0.5 + bias
                m = jnp.max(s, axis=1, keepdims=True)
                p = jnp.exp(s - m)
                denom = jnp.sum(p, axis=1, keepdims=True)
                o = jnp.dot(p.astype(BF16), vg, preferred_element_type=F32)
                o_ref[:, h * HEAD_DIM:(h + 1) * HEAD_DIM] = o / denom

    attend(s_len)


def dsa_prompt_attend(q, k, v, qi, wi, kidx):
    b, s_len = q.shape[:2]
    topk = min(IDX_TOPK_MAX, s_len // 4)

    def flat16(z):
        return z.reshape(b, s_len, -1).astype(BF16)

    def flat16_lo(z):
        z = z.reshape(b, s_len, -1)
        return (z - z.astype(BF16).astype(F32)).astype(BF16)

    def qspec(w):
        return pl.BlockSpec((None, QBLOCK, w), lambda bi, qb: (bi, qb, 0))

    def kspec(w):
        return pl.BlockSpec((None, s_len, w), lambda bi, qb: (bi, 0, 0))

    return pl.pallas_call(
        functools.partial(_dsa_prompt_kernel, topk=topk),
        grid=(b, s_len // QBLOCK),
        in_specs=[qspec(B_WIDTH), qspec(B_WIDTH), qspec(IDX_HEADS), kspec(IDX_DIM), kspec(IDX_DIM),
                  qspec(B_WIDTH), kspec(B_KV_HEADS * HEAD_DIM), kspec(B_KV_HEADS * HEAD_DIM)],
        out_specs=qspec(B_WIDTH),
        out_shape=jax.ShapeDtypeStruct((b, s_len, B_WIDTH), F32),
        compiler_params=pltpu.CompilerParams(dimension_semantics=("arbitrary", "arbitrary"),
                                             vmem_limit_bytes=V7X_VMEM_LIMIT_BYTES),
        name="dsa_prompt",
    )(flat16(qi), flat16_lo(qi), wi, flat16(kidx), flat16_lo(kidx), flat16(q), flat16(k), flat16(v))


SAMPLE_PAGES_PER_STEP = 8
SAMPLE_T_PAD = 8


def _hi_lo(x):
    hi = x.astype(BF16)
    return hi, (x - hi.astype(F32)).astype(BF16)


def _dsa_sample_score_kernel(pt_ref, qi_ref, qil_ref, wi_ref, *refs, n_steps):
    j = pl.program_id(1)
    pages, new_ref, o_ref = refs[:-2], refs[-2], refs[-1]
    qh, ql, wcol = qi_ref[...], qil_ref[...], wi_ref[...]
    for i, pref in enumerate(pages):
        kidx = pref[...]
        if i == 0:
            kidx = jnp.where(j == n_steps - 1, new_ref[...], kidx)
        kh, kl = _hi_lo(kidx)
        s = (lax.dot_general(qh, kh, _NT, preferred_element_type=F32)
             + lax.dot_general(qh, kl, _NT, preferred_element_type=F32)
             + lax.dot_general(ql, kh, _NT, preferred_element_type=F32))
        s = jnp.maximum(s * IDX_DIM ** -0.5, 0.0) * wcol
        tot = s[0:SAMPLE_T_PAD]
        for h in range(1, IDX_HEADS):
            tot = tot + s[h * SAMPLE_T_PAD:(h + 1) * SAMPLE_T_PAD]
        o_ref[:, i * PAGE_SIZE:(i + 1) * PAGE_SIZE] = tot


def _dsa_sample_attn_kernel(pt_ref, score_ref, q_ref, *refs, n_steps, topk, past, t_len):
    pps = SAMPLE_PAGES_PER_STEP
    k_pages, v_pages = refs[:pps], refs[pps:2 * pps]
    kn_ref, vn_ref, o_ref, bias_scr, m_scr, l_scr, acc_scr = refs[2 * pps:]
    j = pl.program_id(1)
    width = score_ref.shape[1]
    group = B_HEADS // B_KV_HEADS

    @pl.when(j == 0)
    def _():
        qpos = past + lax.broadcasted_iota(I32, (SAMPLE_T_PAD, width), 0)
        kpos = lax.broadcasted_iota(I32, (SAMPLE_T_PAD, width), 1)
        sel = _select_topk_mask(score_ref[...], kpos <= qpos, topk)
        bias_scr[...] = jnp.where(sel, 0.0, -jnp.inf)
        m_scr[...] = jnp.full(m_scr.shape, -jnp.inf, F32)
        l_scr[...] = jnp.zeros(l_scr.shape, F32)
        acc_scr[...] = jnp.zeros(acc_scr.shape, F32)

    last = j == n_steps - 1
    col = pl.multiple_of(j * (pps * PAGE_SIZE), pps * PAGE_SIZE)
    b4 = bias_scr[0:t_len, pl.ds(col, pps * PAGE_SIZE)]
    bias = jnp.concatenate([b4] * group, axis=0)
    for g in range(B_KV_HEADS):
        def head_rows(pages, new_ref):
            first = jnp.where(last, new_ref[:, g, :], pages[0][:, g, :])
            return jnp.concatenate([first] + [r[:, g, :] for r in pages[1:]], axis=0).astype(BF16)

        kg = head_rows(k_pages, kn_ref)
        vg = head_rows(v_pages, vn_ref)
        s = lax.dot_general(q_ref[g], kg, _NT, preferred_element_type=F32) * HEAD_DIM ** -0.5 + bias
        m_old = m_scr[g]
        m_new = jnp.maximum(m_old, jnp.max(s, axis=1, keepdims=True))
        m_safe = jnp.where(m_new == -jnp.inf, 0.0, m_new)
        alpha = jnp.exp(m_old - m_safe)
        p = jnp.exp(s - m_safe)
        l_scr[g] = alpha * l_scr[g] + jnp.sum(p, axis=1, keepdims=True)
        acc_scr[g] = alpha * acc_scr[g] + jnp.dot(p.astype(BF16), vg, preferred_element_type=F32)
        m_scr[g] = m_new

    @pl.when(last)
    def _():
        for g in range(B_KV_HEADS):
            o = acc_scr[g] / l_scr[g]
            for hq in range(group):
                h = g * group + hq
                o_ref[:, h * HEAD_DIM:(h + 1) * HEAD_DIM] = o[hq * t_len:(hq + 1) * t_len]


def dsa_sample_attend(q, k, v, qi, wi, kidx, cache_k, cache_v, cache_kidx, page_table, layer):
    b, t = q.shape[:2]
    q = q.reshape(b, t, B_HEADS, HEAD_DIM)
    k = k.reshape(b, t, B_KV_HEADS, HEAD_DIM)
    v = v.reshape(b, t, B_KV_HEADS, HEAD_DIM)
    qi = qi.reshape(b, t, IDX_HEADS, IDX_DIM)
    n_pages = page_table.shape[1]
    past = n_pages * PAGE_SIZE
    topk = min(IDX_TOPK_MAX, (past + t) // 4)
    pps = SAMPLE_PAGES_PER_STEP
    assert n_pages % pps == 0 and t <= SAMPLE_T_PAD
    n_steps = n_pages // pps + 1
    width = n_steps * pps * PAGE_SIZE

    qi_r = jnp.pad(jnp.swapaxes(qi, 1, 2), ((0, 0), (0, 0), (0, SAMPLE_T_PAD - t), (0, 0)))
    qi_hi, qi_lo = _hi_lo(qi_r.reshape(b, IDX_HEADS * SAMPLE_T_PAD, IDX_DIM))
    wi_r = jnp.pad(jnp.swapaxes(wi, 1, 2), ((0, 0), (0, 0), (0, SAMPLE_T_PAD - t)))
    wi_r = wi_r.reshape(b, IDX_HEADS * SAMPLE_T_PAD, 1)
    kidx_new = jnp.pad(kidx, ((0, 0), (0, PAGE_SIZE - t), (0, 0)))

    def page_spec(i, *tail):
        def imap(bi, j, pt):
            return (layer, pt[bi, jnp.minimum(j * pps + i, n_pages - 1)]) + (0,) * (1 + len(tail))
        return pl.BlockSpec((None, None, PAGE_SIZE) + tail, imap)

    def per_b(shape):
        return pl.BlockSpec((None,) + shape, lambda bi, j, pt: (bi,) + (0,) * len(shape))

    score = pl.pallas_call(
        functools.partial(_dsa_sample_score_kernel, n_steps=n_steps),
        grid_spec=pltpu.PrefetchScalarGridSpec(
            num_scalar_prefetch=1, grid=(b, n_steps),
            in_specs=[per_b((IDX_HEADS * SAMPLE_T_PAD, IDX_DIM)), per_b((IDX_HEADS * SAMPLE_T_PAD, IDX_DIM)),
                      per_b((IDX_HEADS * SAMPLE_T_PAD, 1))]
                     + [page_spec(i, IDX_DIM) for i in range(pps)] + [per_b((PAGE_SIZE, IDX_DIM))],
            out_specs=pl.BlockSpec((None, SAMPLE_T_PAD, pps * PAGE_SIZE), lambda bi, j, pt: (bi, 0, j))),
        out_shape=jax.ShapeDtypeStruct((b, SAMPLE_T_PAD, width), F32),
        compiler_params=pltpu.CompilerParams(dimension_semantics=("arbitrary", "arbitrary")),
        name="dsa_sample_score",
    )(page_table, qi_hi, qi_lo, wi_r, *([cache_kidx] * pps), kidx_new)

    group = B_HEADS // B_KV_HEADS
    q_r = q.reshape(b, t, B_KV_HEADS, group, HEAD_DIM)
    q_r = jnp.transpose(q_r, (0, 2, 3, 1, 4)).reshape(b, B_KV_HEADS, group * t, HEAD_DIM).astype(BF16)
    k_new = jnp.pad(k, ((0, 0), (0, PAGE_SIZE - t), (0, 0), (0, 0)))
    v_new = jnp.pad(v, ((0, 0), (0, PAGE_SIZE - t), (0, 0), (0, 0)))
    return pl.pallas_call(
        functools.partial(_dsa_sample_attn_kernel, n_steps=n_steps, topk=topk, past=past, t_len=t),
        grid_spec=pltpu.PrefetchScalarGridSpec(
            num_scalar_prefetch=1, grid=(b, n_steps),
            in_specs=[per_b((SAMPLE_T_PAD, width)), per_b((B_KV_HEADS, group * t, HEAD_DIM))]
                     + [page_spec(i, B_KV_HEADS, HEAD_DIM) for i in range(pps)] * 2
                     + [per_b((PAGE_SIZE, B_KV_HEADS, HEAD_DIM))] * 2,
            out_specs=pl.BlockSpec((None, t, B_HEADS * HEAD_DIM), lambda bi, j, pt: (bi, 0, 0)),
            scratch_shapes=[pltpu.VMEM((SAMPLE_T_PAD, width), F32),
                            pltpu.VMEM((B_KV_HEADS, group * t, 1), F32),
                            pltpu.VMEM((B_KV_HEADS, group * t, 1), F32),
                            pltpu.VMEM((B_KV_HEADS, group * t, HEAD_DIM), F32)]),
        out_shape=jax.ShapeDtypeStruct((b, t, B_HEADS * HEAD_DIM), F32),
        compiler_params=pltpu.CompilerParams(dimension_semantics=("arbitrary", "arbitrary")),
        name="dsa_sample_attn",
    )(page_table, score, q_r, *([cache_k] * pps), *([cache_v] * pps), k_new, v_new)


DIL_WM = 128


def _dilated_prompt_kernel(q_ref, kp_ref, kc_ref, vp_ref, vc_ref, o_ref, lse_ref, *, dil, heads):
    n = pl.program_id(1)
    iq = lax.broadcasted_iota(I32, (DIL_WM, 2 * DIL_WM), 0)
    ik = lax.broadcasted_iota(I32, (DIL_WM, 2 * DIL_WM), 1)
    dist = iq + DIL_WM - ik
    ok = jnp.logical_and(dist >= 0, dist <= DIL_WM)
    ok = jnp.logical_and(ok, jnp.logical_or(ik >= DIL_WM, n > 0))
    bias = jnp.where(ok, 0.0, -jnp.inf)
    for r in range(dil):
        def rows(ref, hs):
            if dil == 1:
                return ref[:, hs]
            return ref[pl.ds(r, DIL_WM, stride=dil), :]

        for h in range(heads):
            hs = slice(h * HEAD_DIM, (h + 1) * HEAD_DIM)
            q = rows(q_ref, hs).astype(BF16)
            k = jnp.concatenate([rows(kp_ref, hs), rows(kc_ref, hs)], axis=0).astype(BF16)
            v = jnp.concatenate([rows(vp_ref, hs), rows(vc_ref, hs)], axis=0).astype(BF16)
            s = lax.dot_general(q, k, _NT, preferred_element_type=F32) * HEAD_DIM ** -0.5 + bias
            m = jnp.max(s, axis=1, keepdims=True)
            p = jnp.exp(s - m)
            denom = jnp.sum(p, axis=1, keepdims=True)
            o = jnp.dot(p.astype(BF16), v, preferred_element_type=F32) / denom
            lse = jnp.broadcast_to(m + jnp.log(denom), (DIL_WM, HEAD_DIM))
            if dil == 1:
                o_ref[:, hs] = o
                lse_ref[:, hs] = lse
            else:
                o_ref[pl.ds(r, DIL_WM, stride=dil), :] = o
                lse_ref[pl.ds(r, DIL_WM, stride=dil), :] = lse


def dilated_prompt_attend(q, k, v_src, v_col0, window, dil):
    b, s_len, width = q.shape
    assert window // dil == DIL_WM and s_len % (DIL_WM * dil) == 0
    rows = DIL_WM * dil
    heads = C_HEADS if dil == 1 else 1
    bw = heads * HEAD_DIM
    assert v_col0 % bw == 0
    vb = v_col0 // bw
    cur = pl.BlockSpec((None, rows, bw), lambda bi, n, hi: (bi, n, hi))
    prev = pl.BlockSpec((None, rows, bw), lambda bi, n, hi: (bi, jnp.maximum(n - 1, 0), hi))
    vcur = pl.BlockSpec((None, rows, bw), lambda bi, n, hi: (bi, n, vb + hi))
    vprev = pl.BlockSpec((None, rows, bw), lambda bi, n, hi: (bi, jnp.maximum(n - 1, 0), vb + hi))
    return pl.pallas_call(
        functools.partial(_dilated_prompt_kernel, dil=dil, heads=heads),
        grid=(b, s_len // rows, width // bw),
        in_specs=[cur, prev, cur, vprev, vcur],
        out_specs=[cur, cur],
        out_shape=[jax.ShapeDtypeStruct(q.shape, F32)] * 2,
        compiler_params=pltpu.CompilerParams(dimension_semantics=("arbitrary",) * 3),
        name="dilated_prompt",
    )(q, k, k, v_src, v_src)


NORM_ROPE_ROWS = 256


def rope_tables(pos):
    half = ROT_DIM // 2
    freqs = ROPE_THETA ** (-jnp.arange(half, dtype=F32) / half)
    ang = pos.astype(F32)[:, None] * freqs[None, :]
    cos, sin = jnp.cos(ang), jnp.sin(ang)
    t = pos.shape[0]
    ones = jnp.ones((t, HEAD_DIM - ROT_DIM), F32)
    zeros = jnp.zeros((t, HEAD_DIM - ROT_DIM), F32)
    c = jnp.concatenate([cos, cos, ones], axis=1)
    s_dn = jnp.concatenate([-sin, jnp.zeros_like(sin), zeros], axis=1)
    s_up = jnp.concatenate([jnp.zeros_like(sin), sin, zeros], axis=1)
    return c, s_dn, s_up


def _norm_rope_kernel(x_ref, g_ref, b_ref, c_ref, sd_ref, su_ref, o_ref, *, heads, norm):
    half = ROT_DIM // 2
    c, sd, su = c_ref[...], sd_ref[...], su_ref[...]
    for h in range(heads):
        hs = slice(h * HEAD_DIM, (h + 1) * HEAD_DIM)
        x = x_ref[:, hs]
        if norm == "rms":
            x = x * lax.rsqrt(jnp.mean(x * x, axis=1, keepdims=True) + NORM_EPS) * g_ref[...]
        elif norm == "layer":
            xc = x - jnp.mean(x, axis=1, keepdims=True)
            x = xc * lax.rsqrt(jnp.mean(xc * xc, axis=1, keepdims=True) + NORM_EPS) * g_ref[...] + b_ref[...]
        dn = pltpu.roll(x, HEAD_DIM - half, axis=1)
        up = pltpu.roll(x, half, axis=1)
        o_ref[:, hs] = x * c + dn * sd + up * su


def norm_rope(x, col0, heads, gain, bias, tables, t_len, norm):
    r, _ = x.shape
    bw = heads * HEAD_DIM
    assert col0 % bw == 0
    if t_len % 8 == 0:
        rows = min(NORM_ROPE_ROWS, t_len)
    else:
        rows, tables = r, tuple(jnp.tile(t, (r // t_len, 1)) for t in tables)
        t_len = r
    assert t_len % rows == 0 and r % rows == 0
    tb = t_len // rows
    tab = pl.BlockSpec((rows, HEAD_DIM), lambda i: (i % tb, 0))
    vec = pl.BlockSpec((1, HEAD_DIM), lambda i: (0, 0))
    g = (jnp.ones((HEAD_DIM,), F32) if gain is None else gain).reshape(1, HEAD_DIM).astype(F32)
    b = (jnp.zeros((HEAD_DIM,), F32) if bias is None else bias).reshape(1, HEAD_DIM).astype(F32)
    return pl.pallas_call(
        functools.partial(_norm_rope_kernel, heads=heads, norm=norm),
        grid=(r // rows,),
        in_specs=[pl.BlockSpec((rows, bw), lambda i: (i, col0 // bw)), vec, vec, tab, tab, tab],
        out_specs=pl.BlockSpec((rows, bw), lambda i: (i, 0)),
        out_shape=jax.ShapeDtypeStruct((r, bw), F32),
        compiler_params=pltpu.CompilerParams(dimension_semantics=("arbitrary",)),
        name="norm_rope",
    )(x, g, b, *tables)


def _rms(x, eps=NORM_EPS):
    xf = x.astype(F32)
    return xf * lax.rsqrt(jnp.mean(xf * xf, axis=-1, keepdims=True) + eps)


def rms_norm(x, g):
    return (_rms(x) * g.astype(F32)).astype(x.dtype)


def rwkv7_mix(sh, prev_row, s0, mu, w0, w_lora, a0, a_lora, k_k, k_a, r_k, gn_g, gn_b):
    bn, t, _ = sh.shape
    prev = jnp.concatenate([prev_row[:, None, :].astype(sh.dtype), sh[:, :-1]], axis=1)
    xm = sh + (prev - sh) * mu
    r, k, v = (xm[..., j * A_WIDTH:(j + 1) * A_WIDTH] for j in range(3))
    xw = xm[..., 3 * A_WIDTH:3 * A_WIDTH + DECAY_LORA]
    xa = xm[..., 3 * A_WIDTH + DECAY_LORA:]
    wlog = -jax.nn.softplus(-(w0 + mm(jnp.tanh(xw), w_lora))) - 0.5
    decay = jnp.exp(-jnp.exp(wlog.astype(F32)))
    a = jax.nn.sigmoid((a0 + mm(xa, a_lora)).astype(F32))

    def heads(z):
        return z.astype(F32).reshape(bn, t, A_HEADS, A_HEAD)

    def head_sum(z):
        return jnp.broadcast_to(jnp.sum(heads(z), axis=-1, keepdims=True),
                                (bn, t, A_HEADS, A_HEAD)).reshape(bn, t, A_WIDTH)

    kk = heads(k * k_k)
    kk = kk / jnp.maximum(jnp.sqrt(jnp.sum(kk * kk, axis=-1, keepdims=True)), 1e-12)
    kk = kk.reshape(bn, t, A_WIDTH)
    k2 = k * (1.0 + (a - 1.0) * k_a.astype(F32))
    kka = kk * a
    q = decay * r - kk * head_sum(kka * r)
    vc = v * head_sum(k2 * r)

    y, s_fin = rwkv_scan((-kk, decay, kka, k2, v, q, vc), s0.astype(F32))
    y = heads(y)
    ym = jnp.mean(y, axis=-1, keepdims=True)
    yc = y - ym
    yn = yc * lax.rsqrt(jnp.mean(yc * yc, axis=-1, keepdims=True) + GN_EPS)
    yn = yn.reshape(bn, t, A_WIDTH) * gn_g.astype(F32) + gn_b.astype(F32)
    bonus = (jnp.sum(heads(r * k2) * r_k.astype(F32), axis=-1, keepdims=True) * heads(v)).reshape(bn, t, A_WIDTH)
    return (yn + bonus).astype(sh.dtype), s_fin.astype(s0.dtype), sh[:, -1]


KV_WIDTH = B_KV_HEADS * HEAD_DIM
E_COLS = {}
_acc = 0
for _name, _w, _pad in (('q', B_WIDTH, 0), ('g_a', A_WIDTH, 0), ('g_b', B_WIDTH, 0), ('rkv', 3 * A_WIDTH, 0),
                        ('k', KV_WIDTH, 0), ('v', KV_WIDTH, 0), ('cqi', IDX_Q_RANK, 0), ('kidx', IDX_DIM, 0),
                        ('wi', IDX_HEADS, V7X_LANES - IDX_HEADS),
                        ('lora', DECAY_LORA + AAA_LORA, 2 * V7X_LANES - DECAY_LORA - AAA_LORA)):
    E_COLS[_name] = (_acc, _w)
    _acc += _w + _pad
E_WIDTH = _acc


def repack_even_w_in(w_all, layer):
    d_in = w_all.shape[1]
    src = {'rkv': 0, 'lora': 3 * A_WIDTH, 'g_a': SHIFT_W, 'q': IN_A, 'k': IN_A + B_WIDTH,
           'v': IN_A + B_WIDTH + KV_WIDTH, 'cqi': IN_A + B_WIDTH + 2 * KV_WIDTH}
    src['kidx'] = src['cqi'] + IDX_Q_RANK
    src['wi'] = src['kidx'] + IDX_DIM
    src['g_b'] = src['wi'] + IDX_HEADS
    parts, pos = [], 0
    for name, (start, width) in E_COLS.items():
        if start > pos:
            parts.append(jnp.zeros((d_in, start - pos), w_all.dtype))
        parts.append(w_all[layer, :, src[name]:src[name] + width])
        pos = start + width
    if E_WIDTH > pos:
        parts.append(jnp.zeros((d_in, E_WIDTH - pos), w_all.dtype))
    return jnp.concatenate(parts, axis=1)


def _cols(u, name):
    start, width = E_COLS[name]
    return u[:, start:start + width]


def even_mixer(u, bn, t, pos, prev_row, s0, attend, ep):
    sh = jnp.concatenate([_cols(u, 'rkv'), _cols(u, 'lora')], axis=1).reshape(bn, t, SHIFT_W)
    y_a, s_fin, last_row = rwkv7_mix(sh, prev_row, s0, ep['shift_mu'], ep['w0'], ep['w_lora'],
                                     ep['a0'], ep['a_lora'], ep['k_k'], ep['k_a'], ep['r_k'],
                                     ep['gn_gain'], ep['gn_bias'])
    tabs = rope_tables(pos)
    q = norm_rope(u, E_COLS['q'][0], B_HEADS, ep['q_norm'], None, tabs, t, "rms")
    k = norm_rope(u, E_COLS['k'][0], B_KV_HEADS, ep['k_norm'], None, tabs, t, "rms")
    v = _cols(u, 'v')
    qi = norm_rope(matmul(rms_norm(_cols(u, 'cqi'), ep['qi_norm']).astype(BF16), ep['w_qi']),
                   0, IDX_HEADS, None, None, tabs, t, "none")
    kidx = norm_rope(u, E_COLS['kidx'][0], 1, ep['kidx_gain'], ep['kidx_bias'], tabs, t, "layer")
    wi = _cols(u, 'wi') * IDX_HEADS ** -0.5

    def seq(z):
        return z.reshape(bn, t, -1)

    y_b = attend(seq(q), seq(k), seq(v), seq(qi), seq(wi), seq(kidx))
    y = jnp.concatenate([y_a * jax.nn.silu(seq(_cols(u, 'g_a'))), y_b * jax.nn.silu(seq(_cols(u, 'g_b')))],
                        axis=-1)
    heads4 = (bn, t, B_KV_HEADS, HEAD_DIM)
    y = y.reshape(bn * t, -1).astype(BF16)
    return y, (s_fin, last_row, k.reshape(heads4), v.reshape(heads4), seq(kidx))


def dilated_sample(q, k, v, buf_k, buf_v, window, dil):
    t = q.shape[1]
    d = q.shape[-1]
    wb = buf_k.shape[1]
    wm = window // dil
    kc = jnp.concatenate([buf_k.astype(k.dtype), k], axis=1)
    vc = jnp.concatenate([buf_v.astype(v.dtype), v], axis=1)
    idx = wb + jnp.arange(t)[:, None] - jnp.arange(wm + 1)[None, :] * dil
    valid = idx >= 0
    idxc = jnp.maximum(idx, 0)
    kg, vg = kc[:, idxc], vc[:, idxc]
    s = jnp.einsum('bthd,btjhd->bthj', q, kg).astype(F32) * d ** -0.5
    s = jnp.where(valid[None, :, None, :], s, -jnp.inf)
    lse = jax.nn.logsumexp(s, axis=-1)
    p = jnp.exp(s - lse[..., None])
    o = jnp.einsum('bthj,btjhd->bthd', p.astype(vg.dtype), vg)
    return o, lse, kc[:, -wb:], vc[:, -wb:]


def odd_mixer(u, bn, t, pos, sample_bufs, op):
    n_g = len(C_GROUPS)
    tabs = rope_tables(pos)
    heads4 = (bn, t, C_HEADS, HEAD_DIM)
    outs, lses, bufs = [], [], []
    for g, (win, dil) in enumerate(C_GROUPS):
        q = norm_rope(u, (3 * g) * C_WIDTH, C_HEADS, op['q_norm'][g], None, tabs, t, "rms")
        k = norm_rope(u, (3 * g + 1) * C_WIDTH, C_HEADS, op['k_norm'][g], None, tabs, t, "rms")
        v_col0 = (3 * g + 2) * C_WIDTH
        v = u[:, v_col0:v_col0 + C_WIDTH].reshape(heads4)
        if sample_bufs is None:
            o, lse = dilated_prompt_attend(q.reshape(bn, t, C_WIDTH), k.reshape(bn, t, C_WIDTH),
                                           u.reshape(bn, t, -1), v_col0, win, dil)
            keep = min(win, t)
            kb, vb = k.reshape(heads4)[:, -keep:], v[:, -keep:]
        else:
            o, lse, kb, vb = dilated_sample(q.reshape(heads4), k.reshape(heads4), v,
                                            sample_bufs[g][0], sample_bufs[g][1], win, dil)
            o = o.reshape(bn, t, C_WIDTH)
            lse = jnp.broadcast_to(lse[..., None], heads4).reshape(bn, t, C_WIDTH)
        outs.append(o)
        lses.append(lse)
        bufs += [kb, vb]
    alpha = jax.nn.softmax(jnp.stack(lses, axis=0), axis=0)
    o = jnp.sum(alpha * jnp.stack(outs, axis=0), axis=0)
    gate = u[:, 3 * n_g * C_WIDTH:].reshape(bn, t, C_WIDTH)
    return (o * jax.nn.silu(gate)).reshape(bn * t, C_WIDTH).astype(BF16), tuple(bufs)


def _ple_rows(h, m, p_l):
    return (_rms(h + m).astype(BF16), p_l.reshape(-1, p_l.shape[-1]).astype(BF16), h, m)


def kernel(x_prompt, x_sample, p_prompt, p_sample, state_wkv, state_shift, cache_k, cache_v,
           cache_kidx, page_table, cache_win_k0, cache_win_v0, cache_win_k1, cache_win_v1,
           cache_win_k2, cache_win_v2, ln_gain, e_w_in, e_shift_mu, e_w0, e_w_lora, e_a0,
           e_a_lora, e_k_k, e_k_a, e_r_k, e_gn_gain, e_gn_bias, e_q_norm, e_k_norm, e_qi_norm,
           e_w_qi, e_kidx_gain, e_kidx_bias, e_w_out, o_w_in, o_q_norm, o_k_norm, o_w_out,
           ple_w_proj, ple_w_gate):
    depth = ln_gain.shape[0]
    bp, s_len, _ = x_prompt.shape
    t_len = x_sample.shape[1]
    past = page_table.shape[1] * PAGE_SIZE
    pos_p = jnp.arange(s_len, dtype=jnp.int32)
    pos_s = past + jnp.arange(t_len, dtype=jnp.int32)
    bufs_k = (cache_win_k0, cache_win_k1, cache_win_k2)
    bufs_v = (cache_win_v0, cache_win_v1, cache_win_v2)
    bs = x_sample.shape[0]
    hp, hs = x_prompt.reshape(bp * s_len, -1), x_sample.reshape(bs * t_len, -1)
    ev_p, ev_s, od_p, od_s = [], [], [], []
    for i in range(depth):
        l = i // 2
        xp = rms_norm(hp, ln_gain[i]).astype(BF16)
        xs = rms_norm(hs, ln_gain[i]).astype(BF16)
        if i % 2 == 0:
            ep = {'shift_mu': e_shift_mu[l], 'w0': e_w0[l], 'w_lora': e_w_lora[l],
                  'a0': e_a0[l], 'a_lora': e_a_lora[l], 'k_k': e_k_k[l], 'k_a': e_k_a[l],
                  'r_k': e_r_k[l], 'gn_gain': e_gn_gain[l], 'gn_bias': e_gn_bias[l],
                  'q_norm': e_q_norm[l], 'k_norm': e_k_norm[l], 'qi_norm': e_qi_norm[l],
                  'w_qi': e_w_qi[l], 'kidx_gain': e_kidx_gain[l], 'kidx_bias': e_kidx_bias[l]}
            up, us = matmul2(xp, xs, repack_even_w_in(e_w_in, l))
            row0 = jnp.zeros((bp, SHIFT_W), F32)
            st0 = jnp.zeros((bp, A_HEADS, A_HEAD, A_HEAD), F32)
            yp, stp = even_mixer(up, bp, s_len, pos_p, row0, st0, dsa_prompt_attend, ep)
            att_s = functools.partial(dsa_sample_attend, cache_k=cache_k, cache_v=cache_v,
                                      cache_kidx=cache_kidx, page_table=page_table, layer=l)
            ys, sts = even_mixer(us, bs, t_len, pos_s, state_shift[l], state_wkv[l], att_s, ep)
            mp, ms = matmul2(yp, ys, e_w_out[l])
            ev_p.append(stp)
            ev_s.append(sts)
        else:
            op = {'q_norm': o_q_norm[l], 'k_norm': o_k_norm[l]}
            up, us = matmul2(xp, xs, o_w_in[l])
            yp, stp = odd_mixer(up, bp, s_len, pos_p, None, op)
            sample_bufs = [(bk[l], bv[l]) for bk, bv in zip(bufs_k, bufs_v)]
            ys, sts = odd_mixer(us, bs, t_len, pos_s, sample_bufs, op)
            mp, ms = matmul2(yp, ys, o_w_out[l])
            od_p.append(stp)
            od_s.append(sts)
        hp, hs = ple_update(ple_w_gate[i], ple_w_proj[i], _ple_rows(hp, mp, p_prompt[i]),
                            _ple_rows(hs, ms, p_sample[i]))
    hp, hs = hp.reshape(x_prompt.shape), hs.reshape(x_sample.shape)

    def st(lst, j):
        if len(lst) == 1:
            return lst[0][j][None]
        return jnp.stack([e[j] for e in lst], axis=0)

    outs = [hp, hs, st(ev_p, 0), st(ev_s, 0), st(ev_p, 1), st(ev_s, 1)]
    outs += [st(ev_p, j) for j in (2, 3, 4)] + [st(ev_s, j) for j in (2, 3, 4)]
    outs += [st(od_p, j) for j in range(6)] + [st(od_s, j) for j in range(6)]
    return tuple(outs)
```
